```python
import math
import jax, jax.numpy as jnp
from jax import lax
import numpy as np

D_MODEL = 4096
BATCH = 4
SEQ = 4096
DEPTH = 1

CHUNK = 64
Q_BLOCK = 128
MIX_WIDTH = D_MODEL
DA_WIDTH = MIX_WIDTH // 2
DA_HEAD_DIM = 128
DA_HEADS = DA_WIDTH // (2 * DA_HEAD_DIM)
DA_QK = DA_HEADS * 2 * DA_HEAD_DIM
ROT_DIM = DA_HEAD_DIM // 4
ROPE_THETA = 500000.0
RW_WIDTH = MIX_WIDTH - DA_WIDTH
RW_HEAD_DIM = 64
RW_HEADS = RW_WIDTH // RW_HEAD_DIM
RW_DECAY_LORA = max(32, int(round(1.8 * math.sqrt(RW_WIDTH) / 32)) * 32)
RW_A_LORA = max(32, int(round(1.8 * math.sqrt(RW_WIDTH) / 32)) * 32)
RW_GATE_LORA = max(32, int(round(0.6 * RW_WIDTH ** 0.8 / 32)) * 32)
RW_PROJ = 3 * RW_WIDTH + RW_DECAY_LORA + RW_A_LORA + RW_GATE_LORA
IN_PROJ = 2 * DA_QK + DA_WIDTH + RW_PROJ
RW_LN_EPS = 64e-5
D_FF = int(round(8 * D_MODEL / 3 / 256)) * 256
PLE_DIM = 256
RMS_EPS = 1e-6

kernel_name = "hybrid_diffattn_rwkv7_macaron_block"


def rms_norm(x, g, eps=RMS_EPS):
    xf = x.astype(jnp.float32)
    y = xf * lax.rsqrt(jnp.mean(xf * xf, axis=-1, keepdims=True) + eps)
    return (y * g.astype(jnp.float32)).astype(x.dtype)


def swiglu(h, w_gate, w_up, w_down):
    return (jax.nn.silu(h @ w_gate) * (h @ w_up)) @ w_down


def partial_rotary(t, pos):
    inv = ROPE_THETA ** (-jnp.arange(0, ROT_DIM, 2, dtype=jnp.float32) / ROT_DIM)
    ang = pos.astype(jnp.float32)[:, None] * inv[None, :]
    cos = jnp.cos(ang).astype(t.dtype)
    sin = jnp.sin(ang).astype(t.dtype)
    half = ROT_DIM // 2
    t1, t2, rest = t[..., :half], t[..., half:ROT_DIM], t[..., ROT_DIM:]
    return jnp.concatenate([t1 * cos - t2 * sin, t2 * cos + t1 * sin, rest], axis=-1)


def diff_attention(zq, zk, zv, lam, lambda_init, subln_g):
    B, T = zq.shape[0], zq.shape[1]
    H, d = DA_HEADS, DA_HEAD_DIM
    pos = jnp.arange(T)
    q = partial_rotary(zq.reshape(B, T, H, 2, d).transpose(0, 2, 3, 1, 4), pos)
    k = partial_rotary(zk.reshape(B, T, H, 2, d).transpose(0, 2, 3, 1, 4), pos)
    v = zv.reshape(B, T, H, 2 * d).transpose(0, 2, 1, 3)
    nb = T // Q_BLOCK
    qb = q.reshape(B, H, 2, nb, Q_BLOCK, d).transpose(3, 0, 1, 2, 4, 5)
    key_chunk = pos // CHUNK
    scale = d ** -0.5

    def block(args):
        i, qi = args
        s = jnp.einsum('bhcqd,bhckd->bhcqk', qi, k).astype(jnp.float32) * scale
        q_chunk = (i * Q_BLOCK + jnp.arange(Q_BLOCK)) // CHUNK
        mask = key_chunk[None, :] <= q_chunk[:, None]
        s = jnp.where(mask[None, None, None], s, -jnp.inf)
        probs = jax.nn.softmax(s, axis=-1)
        attn = probs[:, :, 0] - lam * probs[:, :, 1]
        return jnp.einsum('bhqk,bhkv->bhqv', attn.astype(v.dtype), v)

    o = lax.map(block, (jnp.arange(nb), qb))
    o = o.transpose(1, 0, 3, 2, 4).reshape(B, T, H, 2 * d)
    o = rms_norm(o, subln_g) * (1.0 - lambda_init)
    return o.reshape(B, T, DA_WIDTH)


def rwkv7_scan(r, w, k, v, a, b):
    B, T, H, N = r.shape

    def step(S, inp):
        rt, wt, kt, vt, at, bt = inp
        sa = jnp.einsum('bhij,bhj->bhi', S, at)
        S = S * wt[:, :, None, :] + sa[..., None] * bt[:, :, None, :] + vt[..., None] * kt[:, :, None, :]
        return S, jnp.einsum('bhij,bhj->bhi', S, rt)

    seq = tuple(jnp.moveaxis(t.astype(jnp.float32), 1, 0) for t in (r, w, k, v, a, b))
    S0 = jnp.zeros((B, H, N, N), jnp.float32)
    _, y = lax.scan(step, S0, seq)
    return jnp.moveaxis(y, 0, 1)


def rwkv7_mix(z, mu, w0, w2, a0, a2, g2, k_k, k_a, r_k, ln_w, ln_b):
    B, T = z.shape[0], z.shape[1]
    z_prev = jnp.pad(z[:, :-1], ((0, 0), (1, 0), (0, 0)))
    z = z + (z_prev - z) * mu
    splits = [RW_WIDTH, 2 * RW_WIDTH, 3 * RW_WIDTH, 3 * RW_WIDTH + RW_DECAY_LORA,
              3 * RW_WIDTH + RW_DECAY_LORA + RW_A_LORA]
    r, k, v, wl, al, gl = jnp.split(z, splits, axis=-1)
    w = -jax.nn.softplus(-(w0 + jnp.tanh(wl) @ w2)) - 0.5
    decay = jnp.exp(-jnp.exp(w.astype(jnp.float32)))
    a = jax.nn.sigmoid(a0 + al @ a2)
    g = jax.nn.sigmoid(gl) @ g2

    def heads(t):
        return t.reshape(B, T, RW_HEADS, RW_HEAD_DIM)

    kk = heads(k * k_k).astype(jnp.float32)
    kk = kk / jnp.maximum(jnp.sqrt(jnp.sum(kk * kk, axis=-1, keepdims=True)), 1e-12)
    k = k * (1.0 + (a - 1.0) * k_a)
    y = rwkv7_scan(heads(r), heads(decay), heads(k), heads(v), -kk, kk * heads(a).astype(jnp.float32))
    mean = jnp.mean(y, axis=-1, keepdims=True)
    var = jnp.mean(jnp.square(y - mean), axis=-1, keepdims=True)
    y = ((y - mean) * lax.rsqrt(var + RW_LN_EPS)).reshape(B, T, RW_WIDTH)
    y = y * ln_w.astype(jnp.float32) + ln_b.astype(jnp.float32)
    bonus = jnp.sum((heads(r) * heads(k) * r_k).astype(jnp.float32), axis=-1, keepdims=True) * heads(v).astype(jnp.float32)
    y = y + bonus.reshape(B, T, RW_WIDTH)
    return (y * g.astype(jnp.float32)).astype(z.dtype)


def setup_inputs(seed: int = 0) -> dict:
    key = jax.random.key(seed)
    ks = jax.random.split(key, 40)
    L = DEPTH
    f32 = jnp.float32

    def nrm(k, shape, scale):
        return jax.random.normal(k, shape, f32) * scale

    def gain(k, n):
        return 1.0 + 0.05 * jax.random.normal(k, (L, n), f32)

    return {
        "x": jax.random.normal(ks[0], (BATCH, SEQ, D_MODEL), f32),
        "p": jax.random.normal(ks[1], (DEPTH, BATCH, SEQ, PLE_DIM), f32),
        "ffn1_pre_g": gain(ks[2], D_MODEL),
        "ffn1_w_gate": nrm(ks[3], (L, D_MODEL, D_FF), D_MODEL ** -0.5),
        "ffn1_w_up": nrm(ks[4], (L, D_MODEL, D_FF), D_MODEL ** -0.5),
        "ffn1_w_down": nrm(ks[5], (L, D_FF, D_MODEL), D_FF ** -0.5),
        "ffn1_post_g": gain(ks[6], D_MODEL),
        "mix_pre_g": gain(ks[7], D_MODEL),
        "w_in": nrm(ks[8], (L, D_MODEL, IN_PROJ), D_MODEL ** -0.5),
        "diff_lambda_q1": nrm(ks[9], (L, DA_HEAD_DIM), 0.1),
        "diff_lambda_k1": nrm(ks[10], (L, DA_HEAD_DIM), 0.1),
        "diff_lambda_q2": nrm(ks[11], (L, DA_HEAD_DIM), 0.1),
        "diff_lambda_k2": nrm(ks[12], (L, DA_HEAD_DIM), 0.1),
        "diff_subln_g": gain(ks[13], 2 * DA_HEAD_DIM),
        "rwkv_mu": jax.random.uniform(ks[14], (L, RW_PROJ), f32),
        "rwkv_w0": jax.random.uniform(ks[15], (L, RW_WIDTH), f32, minval=-6.0, maxval=0.5),
        "rwkv_w2": nrm(ks[16], (L, RW_DECAY_LORA, RW_WIDTH), 0.5 * RW_DECAY_LORA ** -0.5),
        "rwkv_a0": nrm(ks[17], (L, RW_WIDTH), 0.5),
        "rwkv_a2": nrm(ks[18], (L, RW_A_LORA, RW_WIDTH), 0.5 * RW_A_LORA ** -0.5),
        "rwkv_g2": nrm(ks[19], (L, RW_GATE_LORA, RW_WIDTH), RW_GATE_LORA ** -0.5),
        "rwkv_k_k": 0.85 + 0.05 * jax.random.normal(ks[20], (L, RW_WIDTH), f32),
        "rwkv_k_a": 1.0 + 0.05 * jax.random.normal(ks[21], (L, RW_WIDTH), f32),
        "rwkv_r_k": nrm(ks[22], (L, RW_HEADS, RW_HEAD_DIM), 0.1),
        "rwkv_ln_w": gain(ks[23], RW_WIDTH),
        "rwkv_ln_b": nrm(ks[24], (L, RW_WIDTH), 0.02),
        "w_out": nrm(ks[25], (L, MIX_WIDTH, D_MODEL), MIX_WIDTH ** -0.5),
        "mix_post_g": gain(ks[26], D_MODEL),
        "ffn2_pre_g": gain(ks[27], D_MODEL),
        "ffn2_w_gate": nrm(ks[28], (L, D_MODEL, D_FF), D_MODEL ** -0.5),
        "ffn2_w_up": nrm(ks[29], (L, D_MODEL, D_FF), D_MODEL ** -0.5),
        "ffn2_w_down": nrm(ks[30], (L, D_FF, D_MODEL), D_FF ** -0.5),
        "ffn2_post_g": gain(ks[31], D_MODEL),
        "ple_pre_g": gain(ks[32], D_MODEL),
        "ple_w_gate": nrm(ks[33], (L, D_MODEL, D_MODEL), D_MODEL ** -0.5),
        "ple_w_proj": nrm(ks[34], (L, PLE_DIM, D_MODEL), PLE_DIM ** -0.5),
        "ple_post_g": gain(ks[35], D_MODEL),
    }


def reference(x, p, ffn1_pre_g, ffn1_w_gate, ffn1_w_up, ffn1_w_down, ffn1_post_g,
              mix_pre_g, w_in, diff_lambda_q1, diff_lambda_k1, diff_lambda_q2, diff_lambda_k2,
              diff_subln_g, rwkv_mu, rwkv_w0, rwkv_w2, rwkv_a0, rwkv_a2, rwkv_g2, rwkv_k_k,
              rwkv_k_a, rwkv_r_k, rwkv_ln_w, rwkv_ln_b, w_out, mix_post_g,
              ffn2_pre_g, ffn2_w_gate, ffn2_w_up, ffn2_w_down, ffn2_post_g,
              ple_pre_g, ple_w_gate, ple_w_proj, ple_post_g):
    for i in range(DEPTH):
        f = swiglu(rms_norm(x, ffn1_pre_g[i]), ffn1_w_gate[i], ffn1_w_up[i], ffn1_w_down[i])
        x = x + 0.5 * rms_norm(f, ffn1_post_g[i])

        h = rms_norm(x, mix_pre_g[i])
        z = h @ w_in[i]
        zq, zk, zv, zr = jnp.split(z, [DA_QK, 2 * DA_QK, 2 * DA_QK + DA_WIDTH], axis=-1)

        lambda_init = 0.8 - 0.6 * math.exp(-0.3 * i)
        lam = (jnp.exp(jnp.sum((diff_lambda_q1[i] * diff_lambda_k1[i]).astype(jnp.float32)))
               - jnp.exp(jnp.sum((diff_lambda_q2[i] * diff_lambda_k2[i]).astype(jnp.float32)))
               + lambda_init)
        o_diff = diff_attention(zq, zk, zv, lam, lambda_init, diff_subln_g[i])
        o_rwkv = rwkv7_mix(zr, rwkv_mu[i], rwkv_w0[i], rwkv_w2[i], rwkv_a0[i], rwkv_a2[i],
                           rwkv_g2[i], rwkv_k_k[i], rwkv_k_a[i], rwkv_r_k[i],
                           rwkv_ln_w[i], rwkv_ln_b[i])
        mixed = jnp.concatenate([o_diff, o_rwkv], axis=-1) @ w_out[i]
        x = x + rms_norm(mixed, mix_post_g[i])

        f = swiglu(rms_norm(x, ffn2_pre_g[i]), ffn2_w_gate[i], ffn2_w_up[i], ffn2_w_down[i])
        x = x + 0.5 * rms_norm(f, ffn2_post_g[i])

        gate = jax.nn.sigmoid(rms_norm(x, ple_pre_g[i]) @ ple_w_gate[i])
        x = x + rms_norm((p[i] @ ple_w_proj[i]) * gate, ple_post_g[i])
    return x
```

```python
import functools
import math

import jax
import jax.numpy as jnp
from jax import lax
from jax.experimental import pallas as pl
from jax.experimental.pallas import tpu as pltpu

F32 = jnp.float32
BF16 = jnp.bfloat16

RMS_EPS = 1e-6
RW_LN_EPS = 64e-5
ATTN_CHUNK = 64
ROPE_THETA = 500000.0
RW_CHUNK = 64

V7X_LANES = 128
V7X_VMEM_CAP_BYTES = 58 * 1024 * 1024


def _tile(n, pref, align):
    t = (min(n, pref) // align) * align
    while t >= align:
        if n % t == 0:
            return t
        t -= align
    return n


def _round_up(n, m):
    return (n + m - 1) // m * m


def _vmem_limit(pipelined_bytes, resident_bytes):
    need = 2 * pipelined_bytes + resident_bytes + (4 << 20)
    return int(min(V7X_VMEM_CAP_BYTES, max(need, 16 << 20)))


def _rms(x, g, eps=RMS_EPS):
    return x * lax.rsqrt(jnp.mean(x * x, axis=-1, keepdims=True) + eps) * g


def _dot(a, b):
    return jnp.dot(a, b, preferred_element_type=F32)


def _dot_nt(a, b):
    return lax.dot_general(a, b, (((1,), (1,)), ((), ())), preferred_element_type=F32)


def _dot_tn(a, b):
    return lax.dot_general(a, b, (((0,), (0,)), ((), ())), preferred_element_type=F32)


def _prenorm_kernel(x_ref, g_ref, h_ref):
    h_ref[...] = _rms(x_ref[...], g_ref[...]).astype(h_ref.dtype)


def _prenorm(x, g):
    m, d = x.shape
    tm = _tile(m, 256, 8)
    return pl.pallas_call(
        _prenorm_kernel,
        out_shape=jax.ShapeDtypeStruct((m, d), BF16),
        grid=(m // tm,),
        in_specs=[pl.BlockSpec((tm, d), lambda i: (i, 0)),
                  pl.BlockSpec((1, d), lambda i: (0, 0))],
        out_specs=pl.BlockSpec((tm, d), lambda i: (i, 0)),
        compiler_params=pltpu.CompilerParams(
            dimension_semantics=("arbitrary",),
            vmem_limit_bytes=_vmem_limit(tm * d * 6, 3 * tm * d * 4)),
        name="prenorm",
    )(x, g)


def _gateup_kernel(h_ref, wg_ref, wu_ref, o_ref):
    h = h_ref[...]
    g = _dot(h, wg_ref[...])
    u = _dot(h, wu_ref[...])
    o_ref[...] = (g * jax.nn.sigmoid(g) * u).astype(o_ref.dtype)


def _gateup(h, wg, wu):
    m, d = h.shape
    f = wg.shape[1]
    tm = _tile(m, 1024, 16)
    tn = _tile(f, 512, V7X_LANES)
    return pl.pallas_call(
        _gateup_kernel,
        out_shape=jax.ShapeDtypeStruct((m, f), BF16),
        grid=(m // tm, f // tn),
        in_specs=[pl.BlockSpec((tm, d), lambda i, j: (i, 0)),
                  pl.BlockSpec((d, tn), lambda i, j: (0, j)),
                  pl.BlockSpec((d, tn), lambda i, j: (0, j))],
        out_specs=pl.BlockSpec((tm, tn), lambda i, j: (i, j)),
        compiler_params=pltpu.CompilerParams(
            dimension_semantics=("arbitrary", "arbitrary"),
            vmem_limit_bytes=_vmem_limit(tm * d * 2 + 2 * d * tn * 2 + tm * tn * 2, 4 * tm * tn * 4)),
        name="swiglu_up",
    )(h, wg, wu)


def _accumulate(a_ref, w_ref, o_ref):
    k = pl.program_id(1)
    d = _dot(a_ref[...], w_ref[...])

    @pl.when(k == 0)
    def _():
        o_ref[...] = d

    @pl.when(k > 0)
    def _():
        o_ref[...] += d


def _down_kernel(a_ref, w_ref, x_ref, gpost_ref, gnext_ref, o_ref, h_ref, *, nk, scale, rows):
    _accumulate(a_ref, w_ref, o_ref)

    @pl.when(pl.program_id(1) == nk - 1)
    def _():
        def body(s, carry):
            sl = pl.ds(pl.multiple_of(s * rows, rows), rows)
            xn = x_ref[sl, :] + scale * _rms(o_ref[sl, :], gpost_ref[...])
            o_ref[sl, :] = xn
            h_ref[sl, :] = _rms(xn, gnext_ref[...]).astype(h_ref.dtype)
            return carry
        lax.fori_loop(0, o_ref.shape[0] // rows, body, 0)


def _down(a, w, x, gpost, gnext, scale):
    m, kdim = a.shape
    d = w.shape[1]
    tm = _tile(m, 512, 16)
    tk = _tile(kdim, 512, V7X_LANES)
    rows = _tile(tm, 64, 8)
    nk = kdim // tk
    return pl.pallas_call(
        functools.partial(_down_kernel, nk=nk, scale=scale, rows=rows),
        out_shape=(jax.ShapeDtypeStruct((m, d), F32), jax.ShapeDtypeStruct((m, d), BF16)),
        grid=(m // tm, nk),
        in_specs=[pl.BlockSpec((tm, tk), lambda i, k: (i, k)),
                  pl.BlockSpec((tk, d), lambda i, k: (k, 0)),
                  pl.BlockSpec((tm, d), lambda i, k: (i, 0)),
                  pl.BlockSpec((1, d), lambda i, k: (0, 0)),
                  pl.BlockSpec((1, d), lambda i, k: (0, 0))],
        out_specs=(pl.BlockSpec((tm, d), lambda i, k: (i, 0)),
                   pl.BlockSpec((tm, d), lambda i, k: (i, 0))),
        compiler_params=pltpu.CompilerParams(
            dimension_semantics=("arbitrary", "arbitrary"),
            vmem_limit_bytes=_vmem_limit(tm * tk * 2 + tk * d * 2 + tm * d * 10, 6 * rows * d * 4)),
        name="down_norm_residual",
    )(a, w, x, gpost, gnext)


def _ple_kernel(a_ref, w_ref, x_ref, p_ref, wp_ref, gpost_ref, o_ref, *, nk, rows):
    _accumulate(a_ref, w_ref, o_ref)

    @pl.when(pl.program_id(1) == nk - 1)
    def _():
        def body(s, carry):
            sl = pl.ds(pl.multiple_of(s * rows, rows), rows)
            gate = jax.nn.sigmoid(o_ref[sl, :])
            proj = _dot(p_ref[sl, :], wp_ref[...])
            o_ref[sl, :] = x_ref[sl, :] + _rms(proj * gate, gpost_ref[...])
            return carry
        lax.fori_loop(0, o_ref.shape[0] // rows, body, 0)


def _ple(a, w, x, p, wp, gpost):
    m, kdim = a.shape
    d = w.shape[1]
    pd = p.shape[1]
    tm = _tile(m, 512, 16)
    tk = _tile(kdim, 512, V7X_LANES)
    rows = _tile(tm, 64, 8)
    nk = kdim // tk
    return pl.pallas_call(
        functools.partial(_ple_kernel, nk=nk, rows=rows),
        out_shape=jax.ShapeDtypeStruct((m, d), F32),
        grid=(m // tm, nk),
        in_specs=[pl.BlockSpec((tm, tk), lambda i, k: (i, k)),
                  pl.BlockSpec((tk, d), lambda i, k: (k, 0)),
                  pl.BlockSpec((tm, d), lambda i, k: (i, 0)),
                  pl.BlockSpec((tm, pd), lambda i, k: (i, 0)),
                  pl.BlockSpec((pd, d), lambda i, k: (0, 0)),
                  pl.BlockSpec((1, d), lambda i, k: (0, 0))],
        out_specs=pl.BlockSpec((tm, d), lambda i, k: (i, 0)),
        compiler_params=pltpu.CompilerParams(
            dimension_semantics=("arbitrary", "arbitrary"),
            vmem_limit_bytes=_vmem_limit(tm * tk * 2 + tk * d * 2 + tm * d * 8 + tm * pd * 2 + pd * d * 2,
                                         6 * rows * d * 4)),
        name="ple_gate_norm_residual",
    )(a, w, x, p, wp, gpost)


def _qkv_kernel(h_ref, w_ref, cos_ref, sa_ref, sb_ref, o_ref, *, rot_tiles, q_tiles, qscale):
    j = pl.program_id(1)
    z = _dot(h_ref[...], w_ref[...])

    @pl.when(j < rot_tiles)
    def _():
        c, sa, sb = cos_ref[...], sa_ref[...], sb_ref[...]
        sc = jnp.where(j < q_tiles, qscale, 1.0).astype(F32)
        outs = []
        for g in range(z.shape[1] // V7X_LANES):
            zg = z[:, g * V7X_LANES:(g + 1) * V7X_LANES]
            rg = zg * c + pltpu.roll(zg, V7X_LANES - 16, 1) * sa + pltpu.roll(zg, 16, 1) * sb
            outs.append(rg * sc)
        o_ref[...] = jnp.concatenate(outs, axis=1).astype(o_ref.dtype)

    @pl.when(j >= rot_tiles)
    def _():
        o_ref[...] = z.astype(o_ref.dtype)


def _qkv_proj(h, w, cos_t, sa_t, sb_t, *, seq, qk_cols, qscale):
    m, d = h.shape
    n = w.shape[1]
    tm = _tile(seq, 1024, 16)
    tn = _tile(math.gcd(qk_cols, n), 512, V7X_LANES)
    tpb = seq // tm
    return pl.pallas_call(
        functools.partial(_qkv_kernel, rot_tiles=2 * qk_cols // tn, q_tiles=qk_cols // tn, qscale=qscale),
        out_shape=jax.ShapeDtypeStruct((m, n), BF16),
        grid=(m // tm, n // tn),
        in_specs=[pl.BlockSpec((tm, d), lambda i, j: (i, 0)),
                  pl.BlockSpec((d, tn), lambda i, j: (0, j)),
                  pl.BlockSpec((tm, V7X_LANES), lambda i, j: (i % tpb, 0)),
                  pl.BlockSpec((tm, V7X_LANES), lambda i, j: (i % tpb, 0)),
                  pl.BlockSpec((tm, V7X_LANES), lambda i, j: (i % tpb, 0))],
        out_specs=pl.BlockSpec((tm, tn), lambda i, j: (i, j)),
        compiler_params=pltpu.CompilerParams(
            dimension_semantics=("arbitrary", "arbitrary"),
            vmem_limit_bytes=_vmem_limit(tm * d * 2 + d * tn * 2 + tm * tn * 2 + 3 * tm * V7X_LANES * 4,
                                         5 * tm * tn * 4)),
        name="qkv_proj_rotary",
    )(h, w, cos_t, sa_t, sb_t)


def _plain_proj_kernel(h_ref, w_ref, o_ref):
    o_ref[...] = _dot(h_ref[...], w_ref[...]).astype(o_ref.dtype)


def _plain_proj(h, w, out_dtype):
    m, d = h.shape
    n = w.shape[1]
    tm = _tile(m, 1024, 16)
    tn = _tile(n, 512, V7X_LANES)
    return pl.pallas_call(
        _plain_proj_kernel,
        out_shape=jax.ShapeDtypeStruct((m, n), out_dtype),
        grid=(m // tm, n // tn),
        in_specs=[pl.BlockSpec((tm, d), lambda i, j: (i, 0)),
                  pl.BlockSpec((d, tn), lambda i, j: (0, j))],
        out_specs=pl.BlockSpec((tm, tn), lambda i, j: (i, j)),
        compiler_params=pltpu.CompilerParams(
            dimension_semantics=("arbitrary", "arbitrary"),
            vmem_limit_bytes=_vmem_limit(tm * d * 2 + d * tn * 2 + tm * tn * 4, 2 * tm * tn * 4)),
        name="rwkv_proj",
    )(h, w)


def _attn_kernel(q_ref, k_ref, v_ref, lq1_ref, lk1_ref, lq2_ref, lk2_ref, sg_ref, o_ref,
                 m_ref, l_ref, acc_ref, *, tq, hd, lambda_init):
    qi = pl.program_id(2)
    q = q_ref[...]
    m_ref[...] = jnp.full(m_ref.shape, -jnp.inf, F32)
    l_ref[...] = jnp.zeros(l_ref.shape, F32)
    acc_ref[...] = jnp.zeros(acc_ref.shape, F32)

    def tile(start, diagonal):
        kt = k_ref[pl.ds(start, tq), :]
        vt = v_ref[pl.ds(start, tq), :]
        for c in range(2):
            s = _dot_nt(q[:, c * hd:(c + 1) * hd], kt[:, c * hd:(c + 1) * hd])
            if diagonal:
                sh = ATTN_CHUNK.bit_length() - 1
                rq = lax.shift_right_logical(lax.broadcasted_iota(jnp.int32, s.shape, 0), sh)
                ck = lax.shift_right_logical(lax.broadcasted_iota(jnp.int32, s.shape, 1), sh)
                s = jnp.where(ck <= rq, s, -jnp.inf)
            m_old = m_ref[c]
            m_new = jnp.maximum(m_old, jnp.max(s, axis=-1, keepdims=True))
            p = jnp.exp(s - m_new)
            alpha = jnp.exp(m_old - m_new)
            l_ref[c] = alpha * l_ref[c] + jnp.sum(p, axis=-1, keepdims=True)
            acc_ref[c] = alpha * acc_ref[c] + _dot(p.astype(BF16), vt)
            m_ref[c] = m_new

    def body(j, carry):
        tile(pl.multiple_of(j * tq, tq), False)
        return carry

    lax.fori_loop(0, qi, body, 0)
    tile(pl.multiple_of(qi * tq, tq), True)

    lam = (jnp.exp(jnp.sum(lq1_ref[...] * lk1_ref[...], axis=-1, keepdims=True))
           - jnp.exp(jnp.sum(lq2_ref[...] * lk2_ref[...], axis=-1, keepdims=True)) + lambda_init)
    o = acc_ref[0] / l_ref[0] - lam * (acc_ref[1] / l_ref[1])
    o_ref[...] = (_rms(o, sg_ref[...]) * (1.0 - lambda_init)).astype(o_ref.dtype)


def _diff_attention(zqkv, lq1, lk1, lq2, lk2, subln_g, *, batch, seq, heads, hd, out_cols, lambda_init):
    m = zqkv.shape[0]
    vd = 2 * hd
    tq = _tile(seq, 512, ATTN_CHUNK)
    nq = seq // tq
    vec = pl.BlockSpec((1, hd), lambda b, h, i: (0, 0))
    return pl.pallas_call(
        functools.partial(_attn_kernel, tq=tq, hd=hd, lambda_init=lambda_init),
        out_shape=jax.ShapeDtypeStruct((m, out_cols), BF16),
        grid=(batch, heads, nq),
        in_specs=[pl.BlockSpec((tq, vd), lambda b, h, i: (b * nq + i, h)),
                  pl.BlockSpec((seq, vd), lambda b, h, i: (b, heads + h)),
                  pl.BlockSpec((seq, vd), lambda b, h, i: (b, 2 * heads + h)),
                  vec, vec, vec, vec,
                  pl.BlockSpec((1, vd), lambda b, h, i: (0, 0))],
        out_specs=pl.BlockSpec((tq, vd), lambda b, h, i: (b * nq + i, h)),
        scratch_shapes=[pltpu.VMEM((2, tq, 1), F32), pltpu.VMEM((2, tq, 1), F32),
                        pltpu.VMEM((2, tq, vd), F32)],
        compiler_params=pltpu.CompilerParams(
            dimension_semantics=("arbitrary", "arbitrary", "arbitrary"),
            vmem_limit_bytes=_vmem_limit(2 * tq * vd * 2 + 2 * seq * vd * 2,
                                         2 * tq * vd * 4 + 4 * tq * V7X_LANES * 4 + 8 * tq * tq * 4)),
        name="diff_attention",
    )(zqkv, zqkv, zqkv, lq1, lk1, lq2, lk2, subln_g)


def _split3_bf16(x):
    h1 = x.astype(BF16)
    r1 = x - h1.astype(F32)
    h2 = r1.astype(BF16)
    h3 = (r1 - h2.astype(F32)).astype(BF16)
    return h1, h2, h3


def _rwkv_kernel(z_ref, mu_ref, w0_ref, w2_ref, a0_ref, a2_ref, g2_ref, kk_ref, ka_ref, rk_ref,
                 lnw_ref, lnb_ref, alias_ref, o_ref,
                 prev_ref, st_ref, zm_ref, lw_ref, as_ref, gt_ref, *, rw, hn, pairs_per_iter):
    del alias_ref
    c = pl.program_id(1)
    lt = z_ref.shape[0]
    ln = V7X_LANES

    @pl.when(c == 0)
    def _():
        prev_ref[...] = jnp.zeros(prev_ref.shape, F32)
        st_ref[...] = jnp.zeros(st_ref.shape, F32)

    z = z_ref[...]
    row = lax.broadcasted_iota(jnp.int32, z.shape, 0)
    zprev = jnp.where(row == 0, prev_ref[0:1, :], pltpu.roll(z, 1, 0))
    prev_ref[0:1, :] = z[lt - 1:lt, :]
    zm = z + (zprev - z) * mu_ref[...]
    zm_ref[...] = zm
    dl = w2_ref.shape[0]
    al = a2_ref.shape[0]
    wl = zm[:, 3 * rw:3 * rw + dl]
    aa = zm[:, 3 * rw + dl:3 * rw + dl + al]
    gl = zm[:, 3 * rw + dl + al:]
    wpre = w0_ref[...] + _dot(jnp.tanh(wl).astype(BF16), w2_ref[...])
    softplus = jnp.maximum(-wpre, 0.0) + jnp.log(1.0 + jnp.exp(-jnp.abs(wpre)))
    lw_ref[...] = -jnp.exp(-softplus - 0.5)
    as_ref[...] = jax.nn.sigmoid(a0_ref[...] + _dot(aa.astype(BF16), a2_ref[...]))
    gt_ref[...] = _dot(jax.nn.sigmoid(gl).astype(BF16), g2_ref[...])

    lane = lax.broadcasted_iota(jnp.int32, (lt, ln), 1)
    first = lane < hn
    r2 = lax.broadcasted_iota(jnp.int32, (2 * hn, 2 * hn), 0)
    c2 = lax.broadcasted_iota(jnp.int32, (2 * hn, 2 * hn), 1)
    hsh = hn.bit_length() - 1
    same = lax.shift_right_logical(r2, hsh) == lax.shift_right_logical(c2, hsh)
    strict = same & (r2 > c2)
    incl = same & (r2 >= c2)
    eye = (r2 == c2).astype(F32)
    tri = (lax.broadcasted_iota(jnp.int32, (lt, lt), 0)
           >= lax.broadcasted_iota(jnp.int32, (lt, lt), 1)).astype(BF16)
    zeros = jnp.zeros((2 * hn, ln), F32)

    def headsum(x):
        s0 = jnp.sum(jnp.where(first, x, 0.0), axis=-1, keepdims=True)
        s1 = jnp.sum(jnp.where(first, 0.0, x), axis=-1, keepdims=True)
        return jnp.where(first, s0, s1)

    def bd(x):
        return jnp.concatenate([jnp.where(first, x, 0.0), jnp.where(first, 0.0, x)], axis=0)

    def fold(x):
        return x[:hn] + x[hn:]

    def pair(off):
        sl = pl.ds(off, ln)
        r = zm_ref[:, pl.ds(off, ln)]
        k = zm_ref[:, pl.ds(rw + off, ln)]
        v = zm_ref[:, pl.ds(2 * rw + off, ln)]
        asg = as_ref[:, sl]
        kk = k * kk_ref[:, sl]
        kk = kk / jnp.maximum(jnp.sqrt(headsum(kk * kk)), 1e-12)
        kn = k * (1.0 + (asg - 1.0) * ka_ref[:, sl])
        an = -kk
        bn = kk * asg

        lw = lw_ref[:, sl]
        h1, h2, h3 = _split3_bf16(lw)
        cum = _dot(tri, h1) + _dot(tri, h2) + _dot(tri, h3)
        tot = cum[lt - 1:lt, :]
        e_prev = jnp.exp(cum - lw)
        e_cum = jnp.exp(cum)
        e_inv = jnp.exp(-cum)
        e_end = jnp.exp(tot - cum)
        at = bd(an * e_prev)
        rt = bd(r * e_cum)
        vb = bd(v)

        lhs = jnp.concatenate([at, rt], axis=0).astype(BF16)
        rhs = jnp.concatenate([bd(bn * e_inv), bd(kn * e_inv)], axis=0).astype(BF16)
        full = _dot_nt(lhs, rhs)
        n2 = 2 * hn
        a_ab = jnp.where(strict, full[:n2, :n2], 0.0)
        a_ak = jnp.where(strict, full[:n2, n2:], 0.0)
        a_rb = jnp.where(incl, full[n2:, :n2], 0.0)
        a_rk = jnp.where(incl, full[n2:, n2:], 0.0)

        akv = _dot(a_ak.astype(BF16), vb.astype(BF16))
        zq = jnp.concatenate([at, akv], axis=1)
        apow = a_ab
        steps = max(1, (lt - 1).bit_length())
        for it in range(steps):
            if it + 1 < steps:
                aw = _dot(apow.astype(BF16), jnp.concatenate([zq, apow], axis=1).astype(BF16))
                zq = zq + aw[:, :2 * ln]
                apow = aw[:, 2 * ln:]
            else:
                zq = zq + _dot(apow.astype(BF16), zq.astype(BF16))

        rhs2 = jnp.concatenate([zq, jnp.concatenate([zeros, vb], axis=1)], axis=0).astype(BF16)
        ry = _dot(jnp.concatenate([a_rb, a_rk], axis=1).astype(BF16), rhs2)
        lhs3 = jnp.concatenate([bd(bn * e_end), bd(kn * e_end)], axis=0).astype(BF16)
        gh = _dot_tn(lhs3, rhs2)
        rp = fold(rt + ry[:, :ln])
        y0 = fold(ry[:, ln:])
        gt = fold(gh[:, :ln] + eye * jnp.exp(tot))
        ht = fold(gh[:, ln:])

        st = st_ref[:, sl]
        out = _dot(jnp.concatenate([rp, gt], axis=0).astype(BF16), bd(st).astype(BF16))
        y = out[:hn] + y0
        st_ref[:, sl] = out[hn:] + ht

        mean = headsum(y) * (1.0 / hn)
        yc = y - mean
        var = headsum(yc * yc) * (1.0 / hn)
        yn = yc * lax.rsqrt(var + RW_LN_EPS) * lnw_ref[:, sl] + lnb_ref[:, sl]
        bonus = headsum(r * kn * rk_ref[:, sl]) * v
        o_ref[:, sl] = ((yn + bonus) * gt_ref[:, sl]).astype(o_ref.dtype)

    npairs = rw // ln

    def body(i, carry):
        for u in range(pairs_per_iter):
            pair(pl.multiple_of((i * pairs_per_iter + u) * ln, ln))
        return carry

    lax.fori_loop(0, npairs // pairs_per_iter, body, 0)


def _rwkv(zr, omix, mu, w0, w2, a0, a2, g2, k_k, k_a, r_k, ln_w, ln_b, *, batch, seq, rw, hn):
    m, zc = zr.shape
    assert hn * 2 == V7X_LANES and RW_CHUNK == hn and seq % RW_CHUNK == 0
    lt = RW_CHUNK
    nc = seq // lt
    npairs = rw // V7X_LANES
    ppi = 2 if npairs % 2 == 0 else 1
    ocol = omix.shape[1] // rw - 1
    assert (ocol + 1) * rw == omix.shape[1]
    row = lambda n: pl.BlockSpec((1, n), lambda b, c: (0, 0))
    full = lambda a: pl.BlockSpec(a.shape, lambda b, c: (0, 0))
    return pl.pallas_call(
        functools.partial(_rwkv_kernel, rw=rw, hn=hn, pairs_per_iter=ppi),
        out_shape=jax.ShapeDtypeStruct(omix.shape, omix.dtype),
        grid=(batch, nc),
        in_specs=[pl.BlockSpec((lt, zc), lambda b, c: (b * nc + c, 0)),
                  row(zc), row(rw), full(w2), row(rw), full(a2), full(g2),
                  row(rw), row(rw), row(rw), row(rw), row(rw),
                  pl.BlockSpec(memory_space=pl.ANY)],
        out_specs=pl.BlockSpec((lt, rw), lambda b, c: (b * nc + c, ocol)),
        scratch_shapes=[pltpu.VMEM((8, zc), F32), pltpu.VMEM((hn, rw), F32),
                        pltpu.VMEM((lt, zc), F32), pltpu.VMEM((lt, rw), F32),
                        pltpu.VMEM((lt, rw), F32), pltpu.VMEM((lt, rw), F32)],
        input_output_aliases={12: 0},
        compiler_params=pltpu.CompilerParams(
            dimension_semantics=("arbitrary", "arbitrary"),
            vmem_limit_bytes=_vmem_limit(lt * zc * 4 + (w2.size + a2.size + g2.size) * 2 + lt * rw * 2,
                                         (8 + lt) * zc * 4 + 4 * lt * rw * 4 + (8 << 20))),
        name="rwkv7_chunked",
    )(zr, mu, w0, w2, a0, a2, g2, k_k, k_a, r_k, ln_w, ln_b, omix)


def _pad_to(a, axis, n):
    pad = n - a.shape[axis]
    if pad == 0:
        return a
    widths = [(0, 0)] * a.ndim
    widths[axis] = (0, pad)
    return jnp.pad(a, widths)


def _ffn(h, x, w_gate, w_up, w_down, gpost, gnext):
    fp = _round_up(w_gate.shape[1], 512)
    wg = _pad_to(w_gate.astype(BF16), 1, fp)
    wu = _pad_to(w_up.astype(BF16), 1, fp)
    wd = _pad_to(w_down.astype(BF16), 0, fp)
    act = _gateup(h, wg, wu)
    return _down(act, wd, x, gpost, gnext, 0.5)


def kernel(x, p, ffn1_pre_g, ffn1_w_gate, ffn1_w_up, ffn1_w_down, ffn1_post_g, mix_pre_g, w_in, diff_lambda_q1, diff_lambda_k1, diff_lambda_q2, diff_lambda_k2, diff_subln_g, rwkv_mu, rwkv_w0, rwkv_w2, rwkv_a0, rwkv_a2, rwkv_g2, rwkv_k_k, rwkv_k_a, rwkv_r_k, rwkv_ln_w, rwkv_ln_b, w_out, mix_post_g, ffn2_pre_g, ffn2_w_gate, ffn2_w_up, ffn2_w_down, ffn2_post_g, ple_pre_g, ple_w_gate, ple_w_proj, ple_post_g):
    batch, seq, d = x.shape
    depth = p.shape[0]
    m = batch * seq
    hd = diff_subln_g.shape[-1] // 2
    rw = rwkv_w2.shape[-1]
    da = w_out.shape[1] - rw
    heads = da // (2 * hd)
    qk = heads * 2 * hd
    rheads, hn = rwkv_r_k.shape[1], rwkv_r_k.shape[2]
    dl, al, gl = rwkv_w2.shape[1], rwkv_a2.shape[1], rwkv_g2.shape[1]
    dlp, alp, glp = (_round_up(n, V7X_LANES) for n in (dl, al, gl))
    rot = hd // 4
    assert rheads * hn == rw and da == rw and w_in.shape[2] == 2 * qk + da + 3 * rw + dl + al + gl

    inv = ROPE_THETA ** (-jnp.arange(0, rot, 2, dtype=F32) / rot)
    ang = jnp.arange(seq).astype(F32)[:, None] * inv[None, :]
    half = rot // 2
    ones = jnp.ones((seq, hd - rot), F32)
    zeros_h = jnp.zeros((seq, half), F32)
    zeros_r = jnp.zeros((seq, hd - rot), F32)
    cos_t = jnp.concatenate([jnp.cos(ang), jnp.cos(ang), ones], axis=1)
    sa_t = jnp.concatenate([-jnp.sin(ang), zeros_h, zeros_r], axis=1)
    sb_t = jnp.concatenate([zeros_h, jnp.sin(ang), zeros_r], axis=1)
    assert half == 16 and hd == V7X_LANES

    row = lambda a: a.reshape(1, -1).astype(F32)
    xf = x.reshape(m, d)
    for i in range(depth):
        lambda_init = 0.8 - 0.6 * math.exp(-0.3 * i)

        h = _prenorm(xf, row(ffn1_pre_g[i]))
        xf, h = _ffn(h, xf, ffn1_w_gate[i], ffn1_w_up[i], ffn1_w_down[i],
                     row(ffn1_post_g[i]), row(mix_pre_g[i]))

        wi = w_in[i].astype(BF16)
        w_qkv = wi[:, :2 * qk + da]
        c0 = 2 * qk + da + 3 * rw
        w_r = jnp.concatenate([wi[:, 2 * qk + da:c0],
                               _pad_to(wi[:, c0:c0 + dl], 1, dlp),
                               _pad_to(wi[:, c0 + dl:c0 + dl + al], 1, alp),
                               _pad_to(wi[:, c0 + dl + al:], 1, glp)], axis=1)
        mu = rwkv_mu[i]
        mu_p = jnp.concatenate([mu[:3 * rw], _pad_to(mu[3 * rw:3 * rw + dl], 0, dlp),
                                _pad_to(mu[3 * rw + dl:3 * rw + dl + al], 0, alp),
                                _pad_to(mu[3 * rw + dl + al:], 0, glp)])
        zqkv = _qkv_proj(h, w_qkv, cos_t, sa_t, sb_t, seq=seq, qk_cols=qk, qscale=hd ** -0.5)
        zr = _plain_proj(h, w_r, F32)

        omix = _diff_attention(zqkv, row(diff_lambda_q1[i]), row(diff_lambda_k1[i]),
                               row(diff_lambda_q2[i]), row(diff_lambda_k2[i]), row(diff_subln_g[i]),
                               batch=batch, seq=seq, heads=heads, hd=hd, out_cols=da + rw,
                               lambda_init=lambda_init)
        omix = _rwkv(zr, omix, row(mu_p), row(rwkv_w0[i]),
                     _pad_to(rwkv_w2[i].astype(BF16), 0, dlp), row(rwkv_a0[i]),
                     _pad_to(rwkv_a2[i].astype(BF16), 0, alp), _pad_to(rwkv_g2[i].astype(BF16), 0, glp),
                     row(rwkv_k_k[i]), row(rwkv_k_a[i]), row(rwkv_r_k[i]),
                     row(rwkv_ln_w[i]), row(rwkv_ln_b[i]), batch=batch, seq=seq, rw=rw, hn=hn)

        xf, h = _down(omix, w_out[i].astype(BF16), xf, row(mix_post_g[i]), row(ffn2_pre_g[i]), 1.0)
        xf, h = _ffn(h, xf, ffn2_w_gate[i], ffn2_w_up[i], ffn2_w_down[i],
                     row(ffn2_post_g[i]), row(ple_pre_g[i]))
        xf = _ple(h, ple_w_gate[i].astype(BF16), xf, p[i].reshape(m, -1).astype(BF16),
                  ple_w_proj[i].astype(BF16), row(ple_post_g[i]))
    return xf.reshape(batch, seq, d)
```

```python
import functools
import math

import jax
import jax.numpy as jnp
from jax import lax
from jax.experimental import pallas as pl
from jax.experimental.pallas import tpu as pltpu

F32 = jnp.float32
BF16 = jnp.bfloat16

RMS_EPS = 1e-6
RW_LN_EPS = 64e-5
ATTN_CHUNK = 64
ROPE_THETA = 500000.0
RW_CHUNK = 64

V7X_LANES = 128
V7X_VMEM_CAP_BYTES = 58 * 1024 * 1024


def _tile(n, pref, align):
    t = (min(n, pref) // align) * align
    while t >= align:
        if n % t == 0:
            return t
        t -= align
    return n


def _round_up(n, m):
    return (n + m - 1) // m * m


def _vmem_limit(pipelined_bytes, resident_bytes):
    need = 2 * pipelined_bytes + resident_bytes + (4 << 20)
    return int(min(V7X_VMEM_CAP_BYTES, max(need, 16 << 20)))


def _rms(x, g, eps=RMS_EPS):
    return x * lax.rsqrt(jnp.mean(x * x, axis=-1, keepdims=True) + eps) * g


def _dot(a, b):
    return jnp.dot(a, b, preferred_element_type=F32)


def _dot_nt(a, b):
    return lax.dot_general(a, b, (((1,), (1,)), ((), ())), preferred_element_type=F32)


def _dot_tn(a, b):
    return lax.dot_general(a, b, (((0,), (0,)), ((), ())), preferred_element_type=F32)


def _prenorm_kernel(x_ref, g_ref, h_ref):
    h_ref[...] = _rms(x_ref[...], g_ref[...]).astype(h_ref.dtype)


def _prenorm(x, g):
    m, d = x.shape
    tm = _tile(m, 256, 8)
    return pl.pallas_call(
        _prenorm_kernel,
        out_shape=jax.ShapeDtypeStruct((m, d), BF16),
        grid=(m // tm,),
        in_specs=[pl.BlockSpec((tm, d), lambda i: (i, 0)),
                  pl.BlockSpec((1, d), lambda i: (0, 0))],
        out_specs=pl.BlockSpec((tm, d), lambda i: (i, 0)),
        compiler_params=pltpu.CompilerParams(
            dimension_semantics=("arbitrary",),
            vmem_limit_bytes=_vmem_limit(tm * d * 6, 3 * tm * d * 4)),
        name="prenorm",
    )(x, g)


def _gateup_kernel(h_ref, wg_ref, wu_ref, o_ref):
    h = h_ref[...]
    g = _dot(h, wg_ref[...])
    u = _dot(h, wu_ref[...])
    o_ref[...] = (g * jax.nn.sigmoid(g) * u).astype(o_ref.dtype)


def _gateup(h, wg, wu):
    m, d = h.shape
    f = wg.shape[1]
    tm = _tile(m, 1024, 16)
    tn = _tile(f, 512, V7X_LANES)
    return pl.pallas_call(
        _gateup_kernel,
        out_shape=jax.ShapeDtypeStruct((m, f), BF16),
        grid=(m // tm, f // tn),
        in_specs=[pl.BlockSpec((tm, d), lambda i, j: (i, 0)),
                  pl.BlockSpec((d, tn), lambda i, j: (0, j)),
                  pl.BlockSpec((d, tn), lambda i, j: (0, j))],
        out_specs=pl.BlockSpec((tm, tn), lambda i, j: (i, j)),
        compiler_params=pltpu.CompilerParams(
            dimension_semantics=("arbitrary", "arbitrary"),
            vmem_limit_bytes=_vmem_limit(tm * d * 2 + 2 * d * tn * 2 + tm * tn * 2, 4 * tm * tn * 4)),
        name="swiglu_up",
    )(h, wg, wu)


def _accumulate(a_ref, w_ref, o_ref):
    k = pl.program_id(1)
    d = _dot(a_ref[...], w_ref[...])

    @pl.when(k == 0)
    def _():
        o_ref[...] = d

    @pl.when(k > 0)
    def _():
        o_ref[...] += d


def _down_kernel(a_ref, w_ref, x_ref, gpost_ref, gnext_ref, o_ref, h_ref, *, nk, scale, rows):
    _accumulate(a_ref, w_ref, o_ref)

    @pl.when(pl.program_id(1) == nk - 1)
    def _():
        def body(s, carry):
            sl = pl.ds(pl.multiple_of(s * rows, rows), rows)
            xn = x_ref[sl, :] + scale * _rms(o_ref[sl, :], gpost_ref[...])
            o_ref[sl, :] = xn
            h_ref[sl, :] = _rms(xn, gnext_ref[...]).astype(h_ref.dtype)
            return carry
        lax.fori_loop(0, o_ref.shape[0] // rows, body, 0)


def _down(a, w, x, gpost, gnext, scale):
    m, kdim = a.shape
    d = w.shape[1]
    tm = _tile(m, 512, 16)
    tk = _tile(kdim, 512, V7X_LANES)
    rows = _tile(tm, 64, 8)
    nk = kdim // tk
    return pl.pallas_call(
        functools.partial(_down_kernel, nk=nk, scale=scale, rows=rows),
        out_shape=(jax.ShapeDtypeStruct((m, d), F32), jax.ShapeDtypeStruct((m, d), BF16)),
        grid=(m // tm, nk),
        in_specs=[pl.BlockSpec((tm, tk), lambda i, k: (i, k)),
                  pl.BlockSpec((tk, d), lambda i, k: (k, 0)),
                  pl.BlockSpec((tm, d), lambda i, k: (i, 0)),
                  pl.BlockSpec((1, d), lambda i, k: (0, 0)),
                  pl.BlockSpec((1, d), lambda i, k: (0, 0))],
        out_specs=(pl.BlockSpec((tm, d), lambda i, k: (i, 0)),
                   pl.BlockSpec((tm, d), lambda i, k: (i, 0))),
        compiler_params=pltpu.CompilerParams(
            dimension_semantics=("arbitrary", "arbitrary"),
            vmem_limit_bytes=_vmem_limit(tm * tk * 2 + tk * d * 2 + tm * d * 10, 6 * rows * d * 4)),
        name="down_norm_residual",
    )(a, w, x, gpost, gnext)


def _ple_kernel(a_ref, w_ref, x_ref, p_ref, wp_ref, gpost_ref, o_ref, *, nk, rows):
    _accumulate(a_ref, w_ref, o_ref)

    @pl.when(pl.program_id(1) == nk - 1)
    def _():
        def body(s, carry):
            sl = pl.ds(pl.multiple_of(s * rows, rows), rows)
            gate = jax.nn.sigmoid(o_ref[sl, :])
            proj = _dot(p_ref[sl, :], wp_ref[...])
            o_ref[sl, :] = x_ref[sl, :] + _rms(proj * gate, gpost_ref[...])
            return carry
        lax.fori_loop(0, o_ref.shape[0] // rows, body, 0)


def _ple(a, w, x, p, wp, gpost):
    m, kdim = a.shape
    d = w.shape[1]
    pd = p.shape[1]
    tm = _tile(m, 512, 16)
    tk = _tile(kdim, 512, V7X_LANES)
    rows = _tile(tm, 64, 8)
    nk = kdim // tk
    return pl.pallas_call(
        functools.partial(_ple_kernel, nk=nk, rows=rows),
        out_shape=jax.ShapeDtypeStruct((m, d), F32),
        grid=(m // tm, nk),
        in_specs=[pl.BlockSpec((tm, tk), lambda i, k: (i, k)),
                  pl.BlockSpec((tk, d), lambda i, k: (k, 0)),
                  pl.BlockSpec((tm, d), lambda i, k: (i, 0)),
                  pl.BlockSpec((tm, pd), lambda i, k: (i, 0)),
                  pl.BlockSpec((pd, d), lambda i, k: (0, 0)),
                  pl.BlockSpec((1, d), lambda i, k: (0, 0))],
        out_specs=pl.BlockSpec((tm, d), lambda i, k: (i, 0)),
        compiler_params=pltpu.CompilerParams(
            dimension_semantics=("arbitrary", "arbitrary"),
            vmem_limit_bytes=_vmem_limit(tm * tk * 2 + tk * d * 2 + tm * d * 8 + tm * pd * 2 + pd * d * 2,
                                         6 * rows * d * 4)),
        name="ple_gate_norm_residual",
    )(a, w, x, p, wp, gpost)


def _qkv_kernel(h_ref, w_ref, cos_ref, sa_ref, sb_ref, o_ref, *, rot_tiles, q_tiles, qscale):
    j = pl.program_id(1)
    z = _dot(h_ref[...], w_ref[...])

    @pl.when(j < rot_tiles)
    def _():
        c, sa, sb = cos_ref[...], sa_ref[...], sb_ref[...]
        sc = jnp.where(j < q_tiles, qscale, 1.0).astype(F32)
        outs = []
        for g in range(z.shape[1] // V7X_LANES):
            zg = z[:, g * V7X_LANES:(g + 1) * V7X_LANES]
            rg = zg * c + pltpu.roll(zg, V7X_LANES - 16, 1) * sa + pltpu.roll(zg, 16, 1) * sb
            outs.append(rg * sc)
        o_ref[...] = jnp.concatenate(outs, axis=1).astype(o_ref.dtype)

    @pl.when(j >= rot_tiles)
    def _():
        o_ref[...] = z.astype(o_ref.dtype)


def _qkv_proj(h, w, cos_t, sa_t, sb_t, *, seq, qk_cols, qscale):
    m, d = h.shape
    n = w.shape[1]
    tm = _tile(seq, 1024, 16)
    tn = _tile(math.gcd(qk_cols, n), 512, V7X_LANES)
    tpb = seq // tm
    return pl.pallas_call(
        functools.partial(_qkv_kernel, rot_tiles=2 * qk_cols // tn, q_tiles=qk_cols // tn, qscale=qscale),
        out_shape=jax.ShapeDtypeStruct((m, n), BF16),
        grid=(m // tm, n // tn),
        in_specs=[pl.BlockSpec((tm, d), lambda i, j: (i, 0)),
                  pl.BlockSpec((d, tn), lambda i, j: (0, j)),
                  pl.BlockSpec((tm, V7X_LANES), lambda i, j: (i % tpb, 0)),
                  pl.BlockSpec((tm, V7X_LANES), lambda i, j: (i % tpb, 0)),
                  pl.BlockSpec((tm, V7X_LANES), lambda i, j: (i % tpb, 0))],
        out_specs=pl.BlockSpec((tm, tn), lambda i, j: (i, j)),
        compiler_params=pltpu.CompilerParams(
            dimension_semantics=("arbitrary", "arbitrary"),
            vmem_limit_bytes=_vmem_limit(tm * d * 2 + d * tn * 2 + tm * tn * 2 + 3 * tm * V7X_LANES * 4,
                                         5 * tm * tn * 4)),
        name="qkv_proj_rotary",
    )(h, w, cos_t, sa_t, sb_t)


def _plain_proj_kernel(h_ref, w_ref, o_ref):
    o_ref[...] = _dot(h_ref[...], w_ref[...]).astype(o_ref.dtype)


def _plain_proj(h, w, out_dtype):
    m, d = h.shape
    n = w.shape[1]
    tm = _tile(m, 1024, 16)
    tn = _tile(n, 512, V7X_LANES)
    return pl.pallas_call(
        _plain_proj_kernel,
        out_shape=jax.ShapeDtypeStruct((m, n), out_dtype),
        grid=(m // tm, n // tn),
        in_specs=[pl.BlockSpec((tm, d), lambda i, j: (i, 0)),
                  pl.BlockSpec((d, tn), lambda i, j: (0, j))],
        out_specs=pl.BlockSpec((tm, tn), lambda i, j: (i, j)),
        compiler_params=pltpu.CompilerParams(
            dimension_semantics=("arbitrary", "arbitrary"),
            vmem_limit_bytes=_vmem_limit(tm * d * 2 + d * tn * 2 + tm * tn * 4, 2 * tm * tn * 4)),
        name="rwkv_proj",
    )(h, w)


def _attn_kernel(q_ref, k_ref, v_ref, lq1_ref, lk1_ref, lq2_ref, lk2_ref, sg_ref, o_ref,
                 m_ref, l_ref, acc_ref, *, tq, hd, lambda_init):
    qi = pl.program_id(2)
    q = q_ref[...]
    m_ref[...] = jnp.full(m_ref.shape, -jnp.inf, F32)
    l_ref[...] = jnp.zeros(l_ref.shape, F32)
    acc_ref[...] = jnp.zeros(acc_ref.shape, F32)

    def tile(start, diagonal):
        kt = k_ref[pl.ds(start, tq), :]
        vt = v_ref[pl.ds(start, tq), :]
        for c in range(2):
            s = _dot_nt(q[:, c * hd:(c + 1) * hd], kt[:, c * hd:(c + 1) * hd])
            if diagonal:
                sh = ATTN_CHUNK.bit_length() - 1
                rq = lax.shift_right_logical(lax.broadcasted_iota(jnp.int32, s.shape, 0), sh)
                ck = lax.shift_right_logical(lax.broadcasted_iota(jnp.int32, s.shape, 1), sh)
                s = jnp.where(ck <= rq, s, -jnp.inf)
            m_old = m_ref[c]
            m_new = jnp.maximum(m_old, jnp.max(s, axis=-1, keepdims=True))
            p = jnp.exp(s - m_new)
            alpha = jnp.exp(m_old - m_new)
            l_ref[c] = alpha * l_ref[c] + jnp.sum(p, axis=-1, keepdims=True)
            acc_ref[c] = alpha * acc_ref[c] + _dot(p.astype(BF16), vt)
            m_ref[c] = m_new

    def body(j, carry):
        tile(pl.multiple_of(j * tq, tq), False)
        return carry

    lax.fori_loop(0, qi, body, 0)
    tile(pl.multiple_of(qi * tq, tq), True)

    lam = (jnp.exp(jnp.sum(lq1_ref[...] * lk1_ref[...], axis=-1, keepdims=True))
           - jnp.exp(jnp.sum(lq2_ref[...] * lk2_ref[...], axis=-1, keepdims=True)) + lambda_init)
    o = acc_ref[0] / l_ref[0] - lam * (acc_ref[1] / l_ref[1])
    o_ref[...] = (_rms(o, sg_ref[...]) * (1.0 - lambda_init)).astype(o_ref.dtype)


def _diff_attention(zqkv, lq1, lk1, lq2, lk2, subln_g, *, batch, seq, heads, hd, out_cols, lambda_init):
    m = zqkv.shape[0]
    vd = 2 * hd
    tq = _tile(seq, 512, ATTN_CHUNK)
    nq = seq // tq
    vec = pl.BlockSpec((1, hd), lambda b, h, i: (0, 0))
    return pl.pallas_call(
        functools.partial(_attn_kernel, tq=tq, hd=hd, lambda_init=lambda_init),
        out_shape=jax.ShapeDtypeStruct((m, out_cols), BF16),
        grid=(batch, heads, nq),
        in_specs=[pl.BlockSpec((tq, vd), lambda b, h, i: (b * nq + i, h)),
                  pl.BlockSpec((seq, vd), lambda b, h, i: (b, heads + h)),
                  pl.BlockSpec((seq, vd), lambda b, h, i: (b, 2 * heads + h)),
                  vec, vec, vec, vec,
                  pl.BlockSpec((1, vd), lambda b, h, i: (0, 0))],
        out_specs=pl.BlockSpec((tq, vd), lambda b, h, i: (b * nq + i, h)),
        scratch_shapes=[pltpu.VMEM((2, tq, 1), F32), pltpu.VMEM((2, tq, 1), F32),
                        pltpu.VMEM((2, tq, vd), F32)],
        compiler_params=pltpu.CompilerParams(
            dimension_semantics=("arbitrary", "arbitrary", "arbitrary"),
            vmem_limit_bytes=_vmem_limit(2 * tq * vd * 2 + 2 * seq * vd * 2,
                                         2 * tq * vd * 4 + 4 * tq * V7X_LANES * 4 + 8 * tq * tq * 4)),
        name="diff_attention",
    )(zqkv, zqkv, zqkv, lq1, lk1, lq2, lk2, subln_g)


def _split3_bf16(x):
    h1 = x.astype(BF16)
    r1 = x - h1.astype(F32)
    h2 = r1.astype(BF16)
    h3 = (r1 - h2.astype(F32)).astype(BF16)
    return h1, h2, h3


def _rwkv_kernel(z_ref, mu_ref, w0_ref, w2_ref, a0_ref, a2_ref, g2_ref, kk_ref, ka_ref, rk_ref,
                 lnw_ref, lnb_ref, alias_ref, o_ref,
                 prev_ref, st_ref, zm_ref, lw_ref, as_ref, gt_ref, *, rw, hn, pairs_per_iter):
    del alias_ref
    c = pl.program_id(1)
    lt = z_ref.shape[0]
    ln = V7X_LANES

    @pl.when(c == 0)
    def _():
        prev_ref[...] = jnp.zeros(prev_ref.shape, F32)
        st_ref[...] = jnp.zeros(st_ref.shape, F32)

    z = z_ref[...]
    row = lax.broadcasted_iota(jnp.int32, z.shape, 0)
    zprev = jnp.where(row == 0, prev_ref[0:1, :], pltpu.roll(z, 1, 0))
    prev_ref[0:1, :] = z[lt - 1:lt, :]
    zm = z + (zprev - z) * mu_ref[...]
    zm_ref[...] = zm
    dl = w2_ref.shape[0]
    al = a2_ref.shape[0]
    wl = zm[:, 3 * rw:3 * rw + dl]
    aa = zm[:, 3 * rw + dl:3 * rw + dl + al]
    gl = zm[:, 3 * rw + dl + al:]
    wpre = w0_ref[...] + _dot(jnp.tanh(wl).astype(BF16), w2_ref[...])
    softplus = jnp.maximum(-wpre, 0.0) + jnp.log(1.0 + jnp.exp(-jnp.abs(wpre)))
    lw_ref[...] = -jnp.exp(-softplus - 0.5)
    as_ref[...] = jax.nn.sigmoid(a0_ref[...] + _dot(aa.astype(BF16), a2_ref[...]))
    gt_ref[...] = _dot(jax.nn.sigmoid(gl).astype(BF16), g2_ref[...])

    lane = lax.broadcasted_iota(jnp.int32, (lt, ln), 1)
    first = lane < hn
    r2 = lax.broadcasted_iota(jnp.int32, (2 * hn, 2 * hn), 0)
    c2 = lax.broadcasted_iota(jnp.int32, (2 * hn, 2 * hn), 1)
    hsh = hn.bit_length() - 1
    same = lax.shift_right_logical(r2, hsh) == lax.shift_right_logical(c2, hsh)
    strict = same & (r2 > c2)
    incl = same & (r2 >= c2)
    eye = (r2 == c2).astype(F32)
    tri = (lax.broadcasted_iota(jnp.int32, (lt, lt), 0)
           >= lax.broadcasted_iota(jnp.int32, (lt, lt), 1)).astype(BF16)
    zeros = jnp.zeros((2 * hn, ln), F32)

    def headsum(x):
        s0 = jnp.sum(jnp.where(first, x, 0.0), axis=-1, keepdims=True)
        s1 = jnp.sum(jnp.where(first, 0.0, x), axis=-1, keepdims=True)
        return jnp.where(first, s0, s1)

    def bd(x):
        return jnp.concatenate([jnp.where(first, x, 0.0), jnp.where(first, 0.0, x)], axis=0)

    def fold(x):
        return x[:hn] + x[hn:]

    def pairs(offs):
        n2 = 2 * hn
        sls = [pl.ds(off, ln) for off in offs]
        rs = [zm_ref[:, pl.ds(off, ln)] for off in offs]
        ks = [zm_ref[:, pl.ds(rw + off, ln)] for off in offs]
        vs = [zm_ref[:, pl.ds(2 * rw + off, ln)] for off in offs]
        asgs = [as_ref[:, sl] for sl in sls]
        kks = [k * kk_ref[:, sl] for k, sl in zip(ks, sls)]
        kks = [kk / jnp.maximum(jnp.sqrt(headsum(kk * kk)), 1e-12) for kk in kks]
        kns = [k * (1.0 + (asg - 1.0) * ka_ref[:, sl]) for k, asg, sl in zip(ks, asgs, sls)]
        bns = [kk * asg for kk, asg in zip(kks, asgs)]

        lws = [lw_ref[:, sl] for sl in sls]
        splits = [_split3_bf16(lw) for lw in lws]
        cums = [_dot(tri, h1) + _dot(tri, h2) + _dot(tri, h3) for h1, h2, h3 in splits]
        tots = [cum[lt - 1:lt, :] for cum in cums]
        e_invs = [jnp.exp(-cum) for cum in cums]
        ats = [bd(-kk * jnp.exp(cum - lw)) for kk, cum, lw in zip(kks, cums, lws)]
        rts = [bd(r * jnp.exp(cum)) for r, cum in zip(rs, cums)]
        vbs = [bd(v) for v in vs]

        fulls = [_dot_nt(jnp.concatenate([at, rt], axis=0).astype(BF16),
                         jnp.concatenate([bd(bn * e), bd(kn * e)], axis=0).astype(BF16))
                 for at, rt, bn, kn, e in zip(ats, rts, bns, kns, e_invs)]
        a_abs = [jnp.where(strict, f[:n2, :n2], 0.0) for f in fulls]
        a_aks = [jnp.where(strict, f[:n2, n2:], 0.0) for f in fulls]
        a_rs = [jnp.concatenate([jnp.where(incl, f[n2:, :n2], 0.0), jnp.where(incl, f[n2:, n2:], 0.0)],
                                axis=1).astype(BF16) for f in fulls]

        akvs = [_dot(a_ak.astype(BF16), vb.astype(BF16)) for a_ak, vb in zip(a_aks, vbs)]
        zqs = [jnp.concatenate([at, akv], axis=1) for at, akv in zip(ats, akvs)]
        apows = a_abs
        steps = max(1, (lt - 1).bit_length())
        for it in range(steps):
            if it + 1 < steps:
                aws = [_dot(ap.astype(BF16), jnp.concatenate([zq, ap], axis=1).astype(BF16))
                       for ap, zq in zip(apows, zqs)]
                zqs = [zq + aw[:, :2 * ln] for zq, aw in zip(zqs, aws)]
                apows = [aw[:, 2 * ln:] for aw in aws]
            else:
                zqs = [zq + _dot(ap.astype(BF16), zq.astype(BF16)) for ap, zq in zip(apows, zqs)]

        rhs2s = [jnp.concatenate([zq, jnp.concatenate([zeros, vb], axis=1)], axis=0).astype(BF16)
                 for zq, vb in zip(zqs, vbs)]
        rys = [_dot(a_r, rhs2) for a_r, rhs2 in zip(a_rs, rhs2s)]
        e_ends = [jnp.exp(tot - cum) for tot, cum in zip(tots, cums)]
        ghs = [_dot_tn(jnp.concatenate([bd(bn * e), bd(kn * e)], axis=0).astype(BF16), rhs2)
               for bn, kn, e, rhs2 in zip(bns, kns, e_ends, rhs2s)]
        lhs4s = [jnp.concatenate([fold(rt + ry[:, :ln]), fold(gh[:, :ln] + eye * jnp.exp(tot))],
                                 axis=0).astype(BF16)
                 for rt, ry, gh, tot in zip(rts, rys, ghs, tots)]
        outs = [_dot(lhs4, bd(st_ref[:, sl]).astype(BF16)) for lhs4, sl in zip(lhs4s, sls)]
        for sl, out, ry, gh, r, kn, v in zip(sls, outs, rys, ghs, rs, kns, vs):
            y = out[:hn] + fold(ry[:, ln:])
            st_ref[:, sl] = out[hn:] + fold(gh[:, ln:])
            mean = headsum(y) * (1.0 / hn)
            yc = y - mean
            var = headsum(yc * yc) * (1.0 / hn)
            yn = yc * lax.rsqrt(var + RW_LN_EPS) * lnw_ref[:, sl] + lnb_ref[:, sl]
            bonus = headsum(r * kn * rk_ref[:, sl]) * v
            o_ref[:, sl] = ((yn + bonus) * gt_ref[:, sl]).astype(o_ref.dtype)

    npairs = rw // ln

    def body(i, carry):
        pairs([pl.multiple_of((i * pairs_per_iter + u) * ln, ln) for u in range(pairs_per_iter)])
        return carry

    lax.fori_loop(0, npairs // pairs_per_iter, body, 0)


def _rwkv(zr, omix, mu, w0, w2, a0, a2, g2, k_k, k_a, r_k, ln_w, ln_b, *, batch, seq, rw, hn):
    m, zc = zr.shape
    assert hn * 2 == V7X_LANES and RW_CHUNK == hn and seq % RW_CHUNK == 0
    lt = RW_CHUNK
    nc = seq // lt
    npairs = rw // V7X_LANES
    ppi = _tile(npairs, 8, 1)
    ocol = omix.shape[1] // rw - 1
    assert (ocol + 1) * rw == omix.shape[1]
    row = lambda n: pl.BlockSpec((1, n), lambda b, c: (0, 0))
    full = lambda a: pl.BlockSpec(a.shape, lambda b, c: (0, 0))
    return pl.pallas_call(
        functools.partial(_rwkv_kernel, rw=rw, hn=hn, pairs_per_iter=ppi),
        out_shape=jax.ShapeDtypeStruct(omix.shape, omix.dtype),
        grid=(batch, nc),
        in_specs=[pl.BlockSpec((lt, zc), lambda b, c: (b * nc + c, 0)),
                  row(zc), row(rw), full(w2), row(rw), full(a2), full(g2),
                  row(rw), row(rw), row(rw), row(rw), row(rw),
                  pl.BlockSpec(memory_space=pl.ANY)],
        out_specs=pl.BlockSpec((lt, rw), lambda b, c: (b * nc + c, ocol)),
        scratch_shapes=[pltpu.VMEM((8, zc), F32), pltpu.VMEM((hn, rw), F32),
                        pltpu.VMEM((lt, zc), F32), pltpu.VMEM((lt, rw), F32),
                        pltpu.VMEM((lt, rw), F32), pltpu.VMEM((lt, rw), F32)],
        input_output_aliases={12: 0},
        compiler_params=pltpu.CompilerParams(
            dimension_semantics=("arbitrary", "arbitrary"),
            vmem_limit_bytes=_vmem_limit(lt * zc * 4 + (w2.size + a2.size + g2.size) * 2 + lt * rw * 2,
                                         (8 + lt) * zc * 4 + 4 * lt * rw * 4 + (8 << 20))),
        name="rwkv7_chunked",
    )(zr, mu, w0, w2, a0, a2, g2, k_k, k_a, r_k, ln_w, ln_b, omix)


def _pad_to(a, axis, n):
    pad = n - a.shape[axis]
    if pad == 0:
        return a
    widths = [(0, 0)] * a.ndim
    widths[axis] = (0, pad)
    return jnp.pad(a, widths)


def _ffn(h, x, w_gate, w_up, w_down, gpost, gnext):
    fp = _round_up(w_gate.shape[1], 512)
    wg = _pad_to(w_gate.astype(BF16), 1, fp)
    wu = _pad_to(w_up.astype(BF16), 1, fp)
    wd = _pad_to(w_down.astype(BF16), 0, fp)
    act = _gateup(h, wg, wu)
    return _down(act, wd, x, gpost, gnext, 0.5)


def kernel(x, p, ffn1_pre_g, ffn1_w_gate, ffn1_w_up, ffn1_w_down, ffn1_post_g, mix_pre_g, w_in, diff_lambda_q1, diff_lambda_k1, diff_lambda_q2, diff_lambda_k2, diff_subln_g, rwkv_mu, rwkv_w0, rwkv_w2, rwkv_a0, rwkv_a2, rwkv_g2, rwkv_k_k, rwkv_k_a, rwkv_r_k, rwkv_ln_w, rwkv_ln_b, w_out, mix_post_g, ffn2_pre_g, ffn2_w_gate, ffn2_w_up, ffn2_w_down, ffn2_post_g, ple_pre_g, ple_w_gate, ple_w_proj, ple_post_g):
    batch, seq, d = x.shape
    depth = p.shape[0]
    m = batch * seq
    hd = diff_subln_g.shape[-1] // 2
    rw = rwkv_w2.shape[-1]
    da = w_out.shape[1] - rw
    heads = da // (2 * hd)
    qk = heads * 2 * hd
    rheads, hn = rwkv_r_k.shape[1], rwkv_r_k.shape[2]
    dl, al, gl = rwkv_w2.shape[1], rwkv_a2.shape[1], rwkv_g2.shape[1]
    dlp, alp, glp = (_round_up(n, V7X_LANES) for n in (dl, al, gl))
    rot = hd // 4
    assert rheads * hn == rw and da == rw and w_in.shape[2] == 2 * qk + da + 3 * rw + dl + al + gl

    inv = ROPE_THETA ** (-jnp.arange(0, rot, 2, dtype=F32) / rot)
    ang = jnp.arange(seq).astype(F32)[:, None] * inv[None, :]
    half = rot // 2
    ones = jnp.ones((seq, hd - rot), F32)
    zeros_h = jnp.zeros((seq, half), F32)
    zeros_r = jnp.zeros((seq, hd - rot), F32)
    cos_t = jnp.concatenate([jnp.cos(ang), jnp.cos(ang), ones], axis=1)
    sa_t = jnp.concatenate([-jnp.sin(ang), zeros_h, zeros_r], axis=1)
    sb_t = jnp.concatenate([zeros_h, jnp.sin(ang), zeros_r], axis=1)
    assert half == 16 and hd == V7X_LANES

    row = lambda a: a.reshape(1, -1).astype(F32)
    xf = x.reshape(m, d)
    for i in range(depth):
        lambda_init = 0.8 - 0.6 * math.exp(-0.3 * i)

        h = _prenorm(xf, row(ffn1_pre_g[i]))
        xf, h = _ffn(h, xf, ffn1_w_gate[i], ffn1_w_up[i], ffn1_w_down[i],
                     row(ffn1_post_g[i]), row(mix_pre_g[i]))

        wi = w_in[i].astype(BF16)
        w_qkv = wi[:, :2 * qk + da]
        c0 = 2 * qk + da + 3 * rw
        w_r = jnp.concatenate([wi[:, 2 * qk + da:c0],
                               _pad_to(wi[:, c0:c0 + dl], 1, dlp),
                               _pad_to(wi[:, c0 + dl:c0 + dl + al], 1, alp),
                               _pad_to(wi[:, c0 + dl + al:], 1, glp)], axis=1)
        mu = rwkv_mu[i]
        mu_p = jnp.concatenate([mu[:3 * rw], _pad_to(mu[3 * rw:3 * rw + dl], 0, dlp),
                                _pad_to(mu[3 * rw + dl:3 * rw + dl + al], 0, alp),
                                _pad_to(mu[3 * rw + dl + al:], 0, glp)])
        zqkv = _qkv_proj(h, w_qkv, cos_t, sa_t, sb_t, seq=seq, qk_cols=qk, qscale=hd ** -0.5)
        zr = _plain_proj(h, w_r, F32)

        omix = _diff_attention(zqkv, row(diff_lambda_q1[i]), row(diff_lambda_k1[i]),
                               row(diff_lambda_q2[i]), row(diff_lambda_k2[i]), row(diff_subln_g[i]),
                               batch=batch, seq=seq, heads=heads, hd=hd, out_cols=da + rw,
                               lambda_init=lambda_init)
        omix = _rwkv(zr, omix, row(mu_p), row(rwkv_w0[i]),
                     _pad_to(rwkv_w2[i].astype(BF16), 0, dlp), row(rwkv_a0[i]),
                     _pad_to(rwkv_a2[i].astype(BF16), 0, alp), _pad_to(rwkv_g2[i].astype(BF16), 0, glp),
                     row(rwkv_k_k[i]), row(rwkv_k_a[i]), row(rwkv_r_k[i]),
                     row(rwkv_ln_w[i]), row(rwkv_ln_b[i]), batch=batch, seq=seq, rw=rw, hn=hn)

        xf, h = _down(omix, w_out[i].astype(BF16), xf, row(mix_post_g[i]), row(ffn2_pre_g[i]), 1.0)
        xf, h = _ffn(h, xf, ffn2_w_gate[i], ffn2_w_up[i], ffn2_w_down[i],
                     row(ffn2_post_g[i]), row(ple_pre_g[i]))
        xf = _ple(h, ple_w_gate[i].astype(BF16), xf, p[i].reshape(m, -1).astype(BF16),
                  ple_w_proj[i].astype(BF16), row(ple_post_g[i]))
    return xf.reshape(batch, seq, d)
```

```python
import functools
import math

import jax
import jax.numpy as jnp
from jax import lax
from jax.experimental import pallas as pl
from jax.experimental.pallas import tpu as pltpu

F32 = jnp.float32
BF16 = jnp.bfloat16

RMS_EPS = 1e-6
RW_LN_EPS = 64e-5
ATTN_CHUNK = 64
ROPE_THETA = 500000.0
RW_CHUNK = 64

V7X_LANES = 128
V7X_VMEM_CAP_BYTES = 58 * 1024 * 1024


def _tile(n, pref, align):
    t = (min(n, pref) // align) * align
    while t >= align:
        if n % t == 0:
            return t
        t -= align
    return n


def _round_up(n, m):
    return (n + m - 1) // m * m


def _vmem_limit(pipelined_bytes, resident_bytes):
    need = 2 * pipelined_bytes + resident_bytes + (4 << 20)
    return int(min(V7X_VMEM_CAP_BYTES, max(need, 16 << 20)))


def _rms(x, g, eps=RMS_EPS):
    return x * lax.rsqrt(jnp.mean(x * x, axis=-1, keepdims=True) + eps) * g


def _dot(a, b):
    return jnp.dot(a, b, preferred_element_type=F32)


def _dot_nt(a, b):
    return lax.dot_general(a, b, (((1,), (1,)), ((), ())), preferred_element_type=F32)


def _dot_tn(a, b):
    return lax.dot_general(a, b, (((0,), (0,)), ((), ())), preferred_element_type=F32)


def _prenorm_kernel(x_ref, g_ref, h_ref):
    h_ref[...] = _rms(x_ref[...], g_ref[...]).astype(h_ref.dtype)


def _prenorm(x, g):
    m, d = x.shape
    tm = _tile(m, 256, 8)
    return pl.pallas_call(
        _prenorm_kernel,
        out_shape=jax.ShapeDtypeStruct((m, d), BF16),
        grid=(m // tm,),
        in_specs=[pl.BlockSpec((tm, d), lambda i: (i, 0)),
                  pl.BlockSpec((1, d), lambda i: (0, 0))],
        out_specs=pl.BlockSpec((tm, d), lambda i: (i, 0)),
        compiler_params=pltpu.CompilerParams(
            dimension_semantics=("arbitrary",),
            vmem_limit_bytes=_vmem_limit(tm * d * 6, 3 * tm * d * 4)),
        name="prenorm",
    )(x, g)


def _gateup_kernel(h_ref, wg_ref, wu_ref, o_ref):
    h = h_ref[...]
    g = _dot(h, wg_ref[...])
    u = _dot(h, wu_ref[...])
    o_ref[...] = (g * jax.nn.sigmoid(g) * u).astype(o_ref.dtype)


def _gateup(h, wg, wu):
    m, d = h.shape
    f = wg.shape[1]
    tm = _tile(m, 1024, 16)
    tn = _tile(f, 512, V7X_LANES)
    return pl.pallas_call(
        _gateup_kernel,
        out_shape=jax.ShapeDtypeStruct((m, f), BF16),
        grid=(m // tm, f // tn),
        in_specs=[pl.BlockSpec((tm, d), lambda i, j: (i, 0)),
                  pl.BlockSpec((d, tn), lambda i, j: (0, j)),
                  pl.BlockSpec((d, tn), lambda i, j: (0, j))],
        out_specs=pl.BlockSpec((tm, tn), lambda i, j: (i, j)),
        compiler_params=pltpu.CompilerParams(
            dimension_semantics=("arbitrary", "arbitrary"),
            vmem_limit_bytes=_vmem_limit(tm * d * 2 + 2 * d * tn * 2 + tm * tn * 2, 4 * tm * tn * 4)),
        name="swiglu_up",
    )(h, wg, wu)


def _accumulate(a_ref, w_ref, o_ref):
    k = pl.program_id(1)

    @pl.when(k == 0)
    def _():
        o_ref[...] = _dot(a_ref[...], w_ref[...])

    @pl.when(k > 0)
    def _():
        o_ref[...] += _dot(a_ref[...], w_ref[...])


def _down_kernel(a_ref, w_ref, x_ref, gpost_ref, gnext_ref, o_ref, h_ref, *, nk, scale, rows):
    _accumulate(a_ref, w_ref, o_ref)

    @pl.when(pl.program_id(1) == nk - 1)
    def _():
        def body(s, carry):
            sl = pl.ds(pl.multiple_of(s * rows, rows), rows)
            xn = x_ref[sl, :] + scale * _rms(o_ref[sl, :], gpost_ref[...])
            o_ref[sl, :] = xn
            h_ref[sl, :] = _rms(xn, gnext_ref[...]).astype(h_ref.dtype)
            return carry
        lax.fori_loop(0, o_ref.shape[0] // rows, body, 0)


def _down(a, w, x, gpost, gnext, scale):
    m, kdim = a.shape
    d = w.shape[1]
    tm = _tile(m, 512, 16)
    tk = _tile(kdim, 512, V7X_LANES)
    rows = _tile(tm, 64, 8)
    nk = kdim // tk
    return pl.pallas_call(
        functools.partial(_down_kernel, nk=nk, scale=scale, rows=rows),
        out_shape=(jax.ShapeDtypeStruct((m, d), F32), jax.ShapeDtypeStruct((m, d), BF16)),
        grid=(m // tm, nk),
        in_specs=[pl.BlockSpec((tm, tk), lambda i, k: (i, k)),
                  pl.BlockSpec((tk, d), lambda i, k: (k, 0)),
                  pl.BlockSpec((tm, d), lambda i, k: (i, 0)),
                  pl.BlockSpec((1, d), lambda i, k: (0, 0)),
                  pl.BlockSpec((1, d), lambda i, k: (0, 0))],
        out_specs=(pl.BlockSpec((tm, d), lambda i, k: (i, 0)),
                   pl.BlockSpec((tm, d), lambda i, k: (i, 0))),
        compiler_params=pltpu.CompilerParams(
            dimension_semantics=("arbitrary", "arbitrary"),
            vmem_limit_bytes=_vmem_limit(tm * tk * 2 + tk * d * 2 + tm * d * 10, 6 * rows * d * 4)),
        name="down_norm_residual",
    )(a, w, x, gpost, gnext)


def _ple_kernel(a_ref, w_ref, x_ref, p_ref, wp_ref, gpost_ref, o_ref, *, nk, rows):
    _accumulate(a_ref, w_ref, o_ref)

    @pl.when(pl.program_id(1) == nk - 1)
    def _():
        def body(s, carry):
            sl = pl.ds(pl.multiple_of(s * rows, rows), rows)
            gate = jax.nn.sigmoid(o_ref[sl, :])
            proj = _dot(p_ref[sl, :], wp_ref[...])
            o_ref[sl, :] = x_ref[sl, :] + _rms(proj * gate, gpost_ref[...])
            return carry
        lax.fori_loop(0, o_ref.shape[0] // rows, body, 0)


def _ple(a, w, x, p, wp, gpost):
    m, kdim = a.shape
    d = w.shape[1]
    pd = p.shape[1]
    tm = _tile(m, 512, 16)
    tk = _tile(kdim, 512, V7X_LANES)
    rows = _tile(tm, 64, 8)
    nk = kdim // tk
    return pl.pallas_call(
        functools.partial(_ple_kernel, nk=nk, rows=rows),
        out_shape=jax.ShapeDtypeStruct((m, d), F32),
        grid=(m // tm, nk),
        in_specs=[pl.BlockSpec((tm, tk), lambda i, k: (i, k)),
                  pl.BlockSpec((tk, d), lambda i, k: (k, 0)),
                  pl.BlockSpec((tm, d), lambda i, k: (i, 0)),
                  pl.BlockSpec((tm, pd), lambda i, k: (i, 0)),
                  pl.BlockSpec((pd, d), lambda i, k: (0, 0)),
                  pl.BlockSpec((1, d), lambda i, k: (0, 0))],
        out_specs=pl.BlockSpec((tm, d), lambda i, k: (i, 0)),
        compiler_params=pltpu.CompilerParams(
            dimension_semantics=("arbitrary", "arbitrary"),
            vmem_limit_bytes=_vmem_limit(tm * tk * 2 + tk * d * 2 + tm * d * 8 + tm * pd * 2 + pd * d * 2,
                                         6 * rows * d * 4)),
        name="ple_gate_norm_residual",
    )(a, w, x, p, wp, gpost)


def _qk_kernel(h_ref, w_ref, cos_ref, sa_ref, sb_ref, o_ref, *, q_tiles, qscale):
    j = pl.program_id(1)
    z = _dot(h_ref[...], w_ref[...])
    c, sa, sb = cos_ref[...], sa_ref[...], sb_ref[...]
    sc = jnp.where(j < q_tiles, qscale, 1.0).astype(F32)
    outs = []
    for g in range(z.shape[1] // V7X_LANES):
        zg = z[:, g * V7X_LANES:(g + 1) * V7X_LANES]
        rg = zg * c + pltpu.roll(zg, V7X_LANES - 16, 1) * sa + pltpu.roll(zg, 16, 1) * sb
        outs.append(rg * sc)
    o_ref[...] = jnp.concatenate(outs, axis=1).astype(o_ref.dtype)


def _qk_proj(h, w, cos_t, sa_t, sb_t, *, seq, qk_cols, qscale):
    m, d = h.shape
    n = w.shape[1]
    tm = _tile(seq, 1024, 16)
    tn = _tile(qk_cols, 512, V7X_LANES)
    tpb = seq // tm
    return pl.pallas_call(
        functools.partial(_qk_kernel, q_tiles=qk_cols // tn, qscale=qscale),
        out_shape=jax.ShapeDtypeStruct((m, n), BF16),
        grid=(m // tm, n // tn),
        in_specs=[pl.BlockSpec((tm, d), lambda i, j: (i, 0)),
                  pl.BlockSpec((d, tn), lambda i, j: (0, j)),
                  pl.BlockSpec((tm, V7X_LANES), lambda i, j: (i % tpb, 0)),
                  pl.BlockSpec((tm, V7X_LANES), lambda i, j: (i % tpb, 0)),
                  pl.BlockSpec((tm, V7X_LANES), lambda i, j: (i % tpb, 0))],
        out_specs=pl.BlockSpec((tm, tn), lambda i, j: (i, j)),
        compiler_params=pltpu.CompilerParams(
            dimension_semantics=("arbitrary", "arbitrary"),
            vmem_limit_bytes=_vmem_limit(tm * d * 2 + d * tn * 2 + tm * tn * 2 + 3 * tm * V7X_LANES * 4,
                                         5 * tm * tn * 4)),
        name="qk_proj_rotary",
    )(h, w, cos_t, sa_t, sb_t)


def _vt_proj_kernel(wt_ref, h_ref, o_ref):
    o_ref[...] = _dot_nt(wt_ref[...], h_ref[...]).astype(o_ref.dtype)


def _vt_proj(h, wt):
    m, d = h.shape
    n = wt.shape[0]
    tm = _tile(m, 1024, V7X_LANES)
    tn = _tile(n, 512, 16)
    return pl.pallas_call(
        _vt_proj_kernel,
        out_shape=jax.ShapeDtypeStruct((n, m), BF16),
        grid=(m // tm, n // tn),
        in_specs=[pl.BlockSpec((tn, d), lambda i, j: (j, 0)),
                  pl.BlockSpec((tm, d), lambda i, j: (i, 0))],
        out_specs=pl.BlockSpec((tn, tm), lambda i, j: (j, i)),
        compiler_params=pltpu.CompilerParams(
            dimension_semantics=("arbitrary", "arbitrary"),
            vmem_limit_bytes=_vmem_limit(tm * d * 2 + d * tn * 2 + tm * tn * 2, 2 * tm * tn * 4)),
        name="v_proj_transposed",
    )(wt, h)


def _plain_proj_kernel(h_ref, w_ref, o_ref):
    o_ref[...] = _dot(h_ref[...], w_ref[...]).astype(o_ref.dtype)


def _plain_proj(h, w, out_dtype):
    m, d = h.shape
    n = w.shape[1]
    tm = _tile(m, 1024, 16)
    tn = _tile(n, 512, V7X_LANES)
    return pl.pallas_call(
        _plain_proj_kernel,
        out_shape=jax.ShapeDtypeStruct((m, n), out_dtype),
        grid=(m // tm, n // tn),
        in_specs=[pl.BlockSpec((tm, d), lambda i, j: (i, 0)),
                  pl.BlockSpec((d, tn), lambda i, j: (0, j))],
        out_specs=pl.BlockSpec((tm, tn), lambda i, j: (i, j)),
        compiler_params=pltpu.CompilerParams(
            dimension_semantics=("arbitrary", "arbitrary"),
            vmem_limit_bytes=_vmem_limit(tm * d * 2 + d * tn * 2 + tm * tn * 4, 2 * tm * tn * 4)),
        name="rwkv_proj",
    )(h, w)


def _attn_kernel(q_ref, k_ref, vt_ref, lq1_ref, lk1_ref, lq2_ref, lk2_ref, sg_ref, o_ref,
                 m_ref, l_ref, acc_ref, *, tq, hd, lambda_init):
    qi = pl.program_id(2)
    q = q_ref[...]
    m_ref[...] = jnp.full(m_ref.shape, -jnp.inf, F32)
    l_ref[...] = jnp.zeros(l_ref.shape, F32)
    acc_ref[...] = jnp.zeros(acc_ref.shape, F32)

    def tile(start, diagonal):
        kt = k_ref[pl.ds(start, tq), :]
        vt = vt_ref[:, pl.ds(start, tq)]
        ss = [_dot_nt(kt[:, c * hd:(c + 1) * hd], q[:, c * hd:(c + 1) * hd]) for c in range(2)]
        if diagonal:
            sh = ATTN_CHUNK.bit_length() - 1
            ck = lax.shift_right_logical(lax.broadcasted_iota(jnp.int32, ss[0].shape, 0), sh)
            rq = lax.shift_right_logical(lax.broadcasted_iota(jnp.int32, ss[0].shape, 1), sh)
            ss = [jnp.where(ck <= rq, s, -jnp.inf) for s in ss]
        m_olds = [m_ref[c] for c in range(2)]
        m_news = [jnp.maximum(m_old, jnp.max(s, axis=0, keepdims=True)) for m_old, s in zip(m_olds, ss)]
        ps = [jnp.exp2(s - m_new) for s, m_new in zip(ss, m_news)]
        alphas = [jnp.exp2(m_old - m_new) for m_old, m_new in zip(m_olds, m_news)]
        pvs = [_dot(vt, p.astype(BF16)) for p in ps]
        for c in range(2):
            l_ref[c] = alphas[c] * l_ref[c] + jnp.sum(ps[c], axis=0, keepdims=True)
            acc_ref[c] = alphas[c] * acc_ref[c] + pvs[c]
            m_ref[c] = m_news[c]

    def body(j, carry):
        tile(pl.multiple_of(j * tq, tq), False)
        return carry

    lax.fori_loop(0, qi, body, 0)
    tile(pl.multiple_of(qi * tq, tq), True)

    lam = (jnp.exp(jnp.sum(lq1_ref[...] * lk1_ref[...], axis=-1, keepdims=True))
           - jnp.exp(jnp.sum(lq2_ref[...] * lk2_ref[...], axis=-1, keepdims=True)) + lambda_init)
    o = acc_ref[0] / l_ref[0] - lam * (acc_ref[1] / l_ref[1])
    o = o * lax.rsqrt(jnp.mean(o * o, axis=0, keepdims=True) + RMS_EPS) * sg_ref[...] * (1.0 - lambda_init)
    o_ref[...] = o.T.astype(o_ref.dtype)


def _diff_attention(zqk, vt, lq1, lk1, lq2, lk2, subln_g, *, batch, seq, heads, hd, out_cols, lambda_init):
    m = zqk.shape[0]
    vd = 2 * hd
    tq = _tile(seq, 512, V7X_LANES)
    nq = seq // tq
    vec = pl.BlockSpec((1, hd), lambda b, h, i: (0, 0))
    return pl.pallas_call(
        functools.partial(_attn_kernel, tq=tq, hd=hd, lambda_init=lambda_init),
        out_shape=jax.ShapeDtypeStruct((m, out_cols), BF16),
        grid=(batch, heads, nq),
        in_specs=[pl.BlockSpec((tq, vd), lambda b, h, i: (b * nq + i, h)),
                  pl.BlockSpec((seq, vd), lambda b, h, i: (b, heads + h)),
                  pl.BlockSpec((vd, seq), lambda b, h, i: (h, b)),
                  vec, vec, vec, vec,
                  pl.BlockSpec((vd, 1), lambda b, h, i: (0, 0))],
        out_specs=pl.BlockSpec((tq, vd), lambda b, h, i: (b * nq + i, h)),
        scratch_shapes=[pltpu.VMEM((2, 1, tq), F32), pltpu.VMEM((2, 1, tq), F32),
                        pltpu.VMEM((2, vd, tq), F32)],
        compiler_params=pltpu.CompilerParams(
            dimension_semantics=("arbitrary", "arbitrary", "arbitrary"),
            vmem_limit_bytes=_vmem_limit(2 * tq * vd * 2 + 2 * seq * vd * 2,
                                         2 * tq * vd * 4 + 10 * tq * tq * 4)),
        name="diff_attention",
    )(zqk, zqk, vt, lq1, lk1, lq2, lk2, subln_g)


def _split3_bf16(x):
    h1 = x.astype(BF16)
    r1 = x - h1.astype(F32)
    h2 = r1.astype(BF16)
    h3 = (r1 - h2.astype(F32)).astype(BF16)
    return h1, h2, h3


def _rwkv_kernel(z_ref, mu_ref, w0_ref, w2_ref, a0_ref, a2_ref, g2_ref, kk_ref, ka_ref, rk_ref,
                 lnw_ref, lnb_ref, alias_ref, o_ref,
                 prev_ref, st_ref, zm_ref, lw_ref, as_ref, gt_ref, *, rw, hn, pairs_per_iter):
    del alias_ref
    c = pl.program_id(1)
    lt = z_ref.shape[0]
    ln = V7X_LANES

    @pl.when(c == 0)
    def _():
        prev_ref[...] = jnp.zeros(prev_ref.shape, F32)
        st_ref[...] = jnp.zeros(st_ref.shape, F32)

    z = z_ref[...]
    row = lax.broadcasted_iota(jnp.int32, z.shape, 0)
    zprev = jnp.where(row == 0, prev_ref[0:1, :], pltpu.roll(z, 1, 0))
    prev_ref[0:1, :] = z[lt - 1:lt, :]
    zm = z + (zprev - z) * mu_ref[...]
    zm_ref[...] = zm
    dl = w2_ref.shape[0]
    al = a2_ref.shape[0]
    wl = zm[:, 3 * rw:3 * rw + dl]
    aa = zm[:, 3 * rw + dl:3 * rw + dl + al]
    gl = zm[:, 3 * rw + dl + al:]
    wpre = w0_ref[...] + _dot(jnp.tanh(wl).astype(BF16), w2_ref[...])
    softplus = jnp.maximum(-wpre, 0.0) + jnp.log(1.0 + jnp.exp(-jnp.abs(wpre)))
    lw_ref[...] = -jnp.exp(-softplus - 0.5)
    as_ref[...] = jax.nn.sigmoid(a0_ref[...] + _dot(aa.astype(BF16), a2_ref[...]))
    gt_ref[...] = _dot(jax.nn.sigmoid(gl).astype(BF16), g2_ref[...])

    lane = lax.broadcasted_iota(jnp.int32, (lt, ln), 1)
    first = lane < hn
    r2 = lax.broadcasted_iota(jnp.int32, (2 * hn, 2 * hn), 0)
    c2 = lax.broadcasted_iota(jnp.int32, (2 * hn, 2 * hn), 1)
    hsh = hn.bit_length() - 1
    same = lax.shift_right_logical(r2, hsh) == lax.shift_right_logical(c2, hsh)
    strict = same & (r2 > c2)
    incl = same & (r2 >= c2)
    eye = (r2 == c2).astype(F32)
    tri = (lax.broadcasted_iota(jnp.int32, (lt, lt), 0)
           >= lax.broadcasted_iota(jnp.int32, (lt, lt), 1)).astype(BF16)
    zeros = jnp.zeros((2 * hn, ln), F32)

    def headsum(x):
        s0 = jnp.sum(jnp.where(first, x, 0.0), axis=-1, keepdims=True)
        s1 = jnp.sum(jnp.where(first, 0.0, x), axis=-1, keepdims=True)
        return jnp.where(first, s0, s1)

    def bd(x):
        return jnp.concatenate([jnp.where(first, x, 0.0), jnp.where(first, 0.0, x)], axis=0)

    def fold(x):
        return x[:hn] + x[hn:]

    def pairs(offs):
        n2 = 2 * hn
        sls = [pl.ds(off, ln) for off in offs]
        rs = [zm_ref[:, pl.ds(off, ln)] for off in offs]
        ks = [zm_ref[:, pl.ds(rw + off, ln)] for off in offs]
        vs = [zm_ref[:, pl.ds(2 * rw + off, ln)] for off in offs]
        asgs = [as_ref[:, sl] for sl in sls]
        kks = [k * kk_ref[:, sl] for k, sl in zip(ks, sls)]
        kks = [kk / jnp.maximum(jnp.sqrt(headsum(kk * kk)), 1e-12) for kk in kks]
        kns = [k * (1.0 + (asg - 1.0) * ka_ref[:, sl]) for k, asg, sl in zip(ks, asgs, sls)]
        bns = [kk * asg for kk, asg in zip(kks, asgs)]

        lws = [lw_ref[:, sl] for sl in sls]
        splits = [_split3_bf16(lw) for lw in lws]
        cums = [_dot(tri, h1) + _dot(tri, h2) + _dot(tri, h3) for h1, h2, h3 in splits]
        tots = [cum[lt - 1:lt, :] for cum in cums]
        e_invs = [jnp.exp(-cum) for cum in cums]
        ats = [bd(-kk * jnp.exp(cum - lw)) for kk, cum, lw in zip(kks, cums, lws)]
        rts = [bd(r * jnp.exp(cum)) for r, cum in zip(rs, cums)]
        vbs = [bd(v) for v in vs]

        fulls = [_dot_nt(jnp.concatenate([at, rt], axis=0).astype(BF16),
                         jnp.concatenate([bd(bn * e), bd(kn * e)], axis=0).astype(BF16))
                 for at, rt, bn, kn, e in zip(ats, rts, bns, kns, e_invs)]
        a_abs = [jnp.where(strict, f[:n2, :n2], 0.0) for f in fulls]
        a_aks = [jnp.where(strict, f[:n2, n2:], 0.0) for f in fulls]
        a_rs = [jnp.concatenate([jnp.where(incl, f[n2:, :n2], 0.0), jnp.where(incl, f[n2:, n2:], 0.0)],
                                axis=1).astype(BF16) for f in fulls]

        akvs = [_dot(a_ak.astype(BF16), vb.astype(BF16)) for a_ak, vb in zip(a_aks, vbs)]
        zqs = [jnp.concatenate([at, akv], axis=1) for at, akv in zip(ats, akvs)]
        apows = a_abs
        steps = max(1, (lt - 1).bit_length())
        for it in range(steps):
            if it + 1 < steps:
                aws = [_dot(ap.astype(BF16), jnp.concatenate([zq, ap], axis=1).astype(BF16))
                       for ap, zq in zip(apows, zqs)]
                zqs = [zq + aw[:, :2 * ln] for zq, aw in zip(zqs, aws)]
                apows = [aw[:, 2 * ln:] for aw in aws]
            else:
                zqs = [zq + _dot(ap.astype(BF16), zq.astype(BF16)) for ap, zq in zip(apows, zqs)]

        rhs2s = [jnp.concatenate([zq, jnp.concatenate([zeros, vb], axis=1)], axis=0).astype(BF16)
                 for zq, vb in zip(zqs, vbs)]
        rys = [_dot(a_r, rhs2) for a_r, rhs2 in zip(a_rs, rhs2s)]
        e_ends = [jnp.exp(tot - cum) for tot, cum in zip(tots, cums)]
        ghs = [_dot_tn(jnp.concatenate([bd(bn * e), bd(kn * e)], axis=0).astype(BF16), rhs2)
               for bn, kn, e, rhs2 in zip(bns, kns, e_ends, rhs2s)]
        lhs4s = [jnp.concatenate([fold(rt + ry[:, :ln]), fold(gh[:, :ln] + eye * jnp.exp(tot))],
                                 axis=0).astype(BF16)
                 for rt, ry, gh, tot in zip(rts, rys, ghs, tots)]
        outs = [_dot(lhs4, bd(st_ref[:, sl]).astype(BF16)) for lhs4, sl in zip(lhs4s, sls)]
        for sl, out, ry, gh, r, kn, v in zip(sls, outs, rys, ghs, rs, kns, vs):
            y = out[:hn] + fold(ry[:, ln:])
            st_ref[:, sl] = out[hn:] + fold(gh[:, ln:])
            mean = headsum(y) * (1.0 / hn)
            yc = y - mean
            var = headsum(yc * yc) * (1.0 / hn)
            yn = yc * lax.rsqrt(var + RW_LN_EPS) * lnw_ref[:, sl] + lnb_ref[:, sl]
            bonus = headsum(r * kn * rk_ref[:, sl]) * v
            o_ref[:, sl] = ((yn + bonus) * gt_ref[:, sl]).astype(o_ref.dtype)

    npairs = rw // ln

    def body(i, carry):
        pairs([pl.multiple_of((i * pairs_per_iter + u) * ln, ln) for u in range(pairs_per_iter)])
        return carry

    lax.fori_loop(0, npairs // pairs_per_iter, body, 0)


def _rwkv(zr, omix, mu, w0, w2, a0, a2, g2, k_k, k_a, r_k, ln_w, ln_b, *, batch, seq, rw, hn):
    m, zc = zr.shape
    assert hn * 2 == V7X_LANES and RW_CHUNK == hn and seq % RW_CHUNK == 0
    lt = RW_CHUNK
    nc = seq // lt
    npairs = rw // V7X_LANES
    ppi = _tile(npairs, 8, 1)
    ocol = omix.shape[1] // rw - 1
    assert (ocol + 1) * rw == omix.shape[1]
    row = lambda n: pl.BlockSpec((1, n), lambda b, c: (0, 0))
    full = lambda a: pl.BlockSpec(a.shape, lambda b, c: (0, 0))
    return pl.pallas_call(
        functools.partial(_rwkv_kernel, rw=rw, hn=hn, pairs_per_iter=ppi),
        out_shape=jax.ShapeDtypeStruct(omix.shape, omix.dtype),
        grid=(batch, nc),
        in_specs=[pl.BlockSpec((lt, zc), lambda b, c: (b * nc + c, 0)),
                  row(zc), row(rw), full(w2), row(rw), full(a2), full(g2),
                  row(rw), row(rw), row(rw), row(rw), row(rw),
                  pl.BlockSpec(memory_space=pl.ANY)],
        out_specs=pl.BlockSpec((lt, rw), lambda b, c: (b * nc + c, ocol)),
        scratch_shapes=[pltpu.VMEM((8, zc), F32), pltpu.VMEM((hn, rw), F32),
                        pltpu.VMEM((lt, zc), F32), pltpu.VMEM((lt, rw), F32),
                        pltpu.VMEM((lt, rw), F32), pltpu.VMEM((lt, rw), F32)],
        input_output_aliases={12: 0},
        compiler_params=pltpu.CompilerParams(
            dimension_semantics=("arbitrary", "arbitrary"),
            vmem_limit_bytes=_vmem_limit(lt * zc * 4 + (w2.size + a2.size + g2.size) * 2 + lt * rw * 2,
                                         (8 + lt) * zc * 4 + 4 * lt * rw * 4 + (8 << 20))),
        name="rwkv7_chunked",
    )(zr, mu, w0, w2, a0, a2, g2, k_k, k_a, r_k, ln_w, ln_b, omix)


def _pad_to(a, axis, n):
    pad = n - a.shape[axis]
    if pad == 0:
        return a
    widths = [(0, 0)] * a.ndim
    widths[axis] = (0, pad)
    return jnp.pad(a, widths)


def _ffn(h, x, w_gate, w_up, w_down, gpost, gnext):
    fp = _round_up(w_gate.shape[1], 512)
    wg = _pad_to(w_gate.astype(BF16), 1, fp)
    wu = _pad_to(w_up.astype(BF16), 1, fp)
    wd = _pad_to(w_down.astype(BF16), 0, fp)
    act = _gateup(h, wg, wu)
    return _down(act, wd, x, gpost, gnext, 0.5)


def kernel(x, p, ffn1_pre_g, ffn1_w_gate, ffn1_w_up, ffn1_w_down, ffn1_post_g, mix_pre_g, w_in, diff_lambda_q1, diff_lambda_k1, diff_lambda_q2, diff_lambda_k2, diff_subln_g, rwkv_mu, rwkv_w0, rwkv_w2, rwkv_a0, rwkv_a2, rwkv_g2, rwkv_k_k, rwkv_k_a, rwkv_r_k, rwkv_ln_w, rwkv_ln_b, w_out, mix_post_g, ffn2_pre_g, ffn2_w_gate, ffn2_w_up, ffn2_w_down, ffn2_post_g, ple_pre_g, ple_w_gate, ple_w_proj, ple_post_g):
    batch, seq, d = x.shape
    depth = p.shape[0]
    m = batch * seq
    hd = diff_subln_g.shape[-1] // 2
    rw = rwkv_w2.shape[-1]
    da = w_out.shape[1] - rw
    heads = da // (2 * hd)
    qk = heads * 2 * hd
    rheads, hn = rwkv_r_k.shape[1], rwkv_r_k.shape[2]
    dl, al, gl = rwkv_w2.shape[1], rwkv_a2.shape[1], rwkv_g2.shape[1]
    dlp, alp, glp = (_round_up(n, V7X_LANES) for n in (dl, al, gl))
    rot = hd // 4
    assert rheads * hn == rw and da == rw and w_in.shape[2] == 2 * qk + da + 3 * rw + dl + al + gl

    inv = ROPE_THETA ** (-jnp.arange(0, rot, 2, dtype=F32) / rot)
    ang = jnp.arange(seq).astype(F32)[:, None] * inv[None, :]
    half = rot // 2
    ones = jnp.ones((seq, hd - rot), F32)
    zeros_h = jnp.zeros((seq, half), F32)
    zeros_r = jnp.zeros((seq, hd - rot), F32)
    cos_t = jnp.concatenate([jnp.cos(ang), jnp.cos(ang), ones], axis=1)
    sa_t = jnp.concatenate([-jnp.sin(ang), zeros_h, zeros_r], axis=1)
    sb_t = jnp.concatenate([zeros_h, jnp.sin(ang), zeros_r], axis=1)
    assert half == 16 and hd == V7X_LANES

    row = lambda a: a.reshape(1, -1).astype(F32)
    xf = x.reshape(m, d)
    for i in range(depth):
        lambda_init = 0.8 - 0.6 * math.exp(-0.3 * i)

        h = _prenorm(xf, row(ffn1_pre_g[i]))
        xf, h = _ffn(h, xf, ffn1_w_gate[i], ffn1_w_up[i], ffn1_w_down[i],
                     row(ffn1_post_g[i]), row(mix_pre_g[i]))

        wi = w_in[i].astype(BF16)
        c0 = 2 * qk + da + 3 * rw
        w_r = jnp.concatenate([wi[:, 2 * qk + da:c0],
                               _pad_to(wi[:, c0:c0 + dl], 1, dlp),
                               _pad_to(wi[:, c0 + dl:c0 + dl + al], 1, alp),
                               _pad_to(wi[:, c0 + dl + al:], 1, glp)], axis=1)
        mu = rwkv_mu[i]
        mu_p = jnp.concatenate([mu[:3 * rw], _pad_to(mu[3 * rw:3 * rw + dl], 0, dlp),
                                _pad_to(mu[3 * rw + dl:3 * rw + dl + al], 0, alp),
                                _pad_to(mu[3 * rw + dl + al:], 0, glp)])
        zqk = _qk_proj(h, wi[:, :2 * qk], cos_t, sa_t, sb_t, seq=seq, qk_cols=qk,
                       qscale=hd ** -0.5 * math.log2(math.e))
        vt = _vt_proj(h, wi[:, 2 * qk:2 * qk + da].T)
        zr = _plain_proj(h, w_r, F32)

        omix = _diff_attention(zqk, vt, row(diff_lambda_q1[i]), row(diff_lambda_k1[i]),
                               row(diff_lambda_q2[i]), row(diff_lambda_k2[i]),
                               diff_subln_g[i].reshape(-1, 1).astype(F32),
                               batch=batch, seq=seq, heads=heads, hd=hd, out_cols=da + rw,
                               lambda_init=lambda_init)
        omix = _rwkv(zr, omix, row(mu_p), row(rwkv_w0[i]),
                     _pad_to(rwkv_w2[i].astype(BF16), 0, dlp), row(rwkv_a0[i]),
                     _pad_to(rwkv_a2[i].astype(BF16), 0, alp), _pad_to(rwkv_g2[i].astype(BF16), 0, glp),
                     row(rwkv_k_k[i]), row(rwkv_k_a[i]), row(rwkv_r_k[i]),
                     row(rwkv_ln_w[i]), row(rwkv_ln_b[i]), batch=batch, seq=seq, rw=rw, hn=hn)

        xf, h = _down(omix, w_out[i].astype(BF16), xf, row(mix_post_g[i]), row(ffn2_pre_g[i]), 1.0)
        xf, h = _ffn(h, xf, ffn2_w_gate[i], ffn2_w_up[i], ffn2_w_down[i],
                     row(ffn2_post_g[i]), row(ple_pre_g[i]))
        xf = _ple(h, ple_w_gate[i].astype(BF16), xf, p[i].reshape(m, -1).astype(BF16),
                  ple_w_proj[i].astype(BF16), row(ple_post_g[i]))
    return xf.reshape(batch, seq, d)
```

```python
import functools
import math

import jax
import jax.numpy as jnp
from jax import lax
from jax.experimental import pallas as pl
from jax.experimental.pallas import tpu as pltpu

F32 = jnp.float32
BF16 = jnp.bfloat16

RMS_EPS = 1e-6
RW_LN_EPS = 64e-5
ATTN_CHUNK = 64
ROPE_THETA = 500000.0
RW_CHUNK = 64

V7X_LANES = 128
V7X_VMEM_CAP_BYTES = 58 * 1024 * 1024


def _tile(n, pref, align):
    t = (min(n, pref) // align) * align
    while t >= align:
        if n % t == 0:
            return t
        t -= align
    return n


def _round_up(n, m):
    return (n + m - 1) // m * m


def _vmem_limit(pipelined_bytes, resident_bytes):
    need = 2 * pipelined_bytes + resident_bytes + (4 << 20)
    return int(min(V7X_VMEM_CAP_BYTES, max(need, 16 << 20)))


def _rms(x, g, eps=RMS_EPS):
    return x * lax.rsqrt(jnp.mean(x * x, axis=-1, keepdims=True) + eps) * g


def _dot(a, b):
    return jnp.dot(a, b, preferred_element_type=F32)


def _dot_nt(a, b):
    return lax.dot_general(a, b, (((1,), (1,)), ((), ())), preferred_element_type=F32)


def _dot_tn(a, b):
    return lax.dot_general(a, b, (((0,), (0,)), ((), ())), preferred_element_type=F32)


def _prenorm_kernel(x_ref, g_ref, h_ref):
    h_ref[...] = _rms(x_ref[...], g_ref[...]).astype(h_ref.dtype)


def _prenorm(x, g):
    m, d = x.shape
    tm = _tile(m, 256, 8)
    return pl.pallas_call(
        _prenorm_kernel,
        out_shape=jax.ShapeDtypeStruct((m, d), BF16),
        grid=(m // tm,),
        in_specs=[pl.BlockSpec((tm, d), lambda i: (i, 0)),
                  pl.BlockSpec((1, d), lambda i: (0, 0))],
        out_specs=pl.BlockSpec((tm, d), lambda i: (i, 0)),
        compiler_params=pltpu.CompilerParams(
            dimension_semantics=("arbitrary",),
            vmem_limit_bytes=_vmem_limit(tm * d * 6, 3 * tm * d * 4)),
        name="prenorm",
    )(x, g)


def _gateup_kernel(h_ref, wg_ref, wu_ref, o_ref):
    h = h_ref[...]
    g = _dot(h, wg_ref[...])
    u = _dot(h, wu_ref[...])
    o_ref[...] = (g * jax.nn.sigmoid(g) * u).astype(o_ref.dtype)


def _gateup(h, wg, wu):
    m, d = h.shape
    f = wg.shape[1]
    tm = _tile(m, 1024, 16)
    tn = _tile(f, 512, V7X_LANES)
    return pl.pallas_call(
        _gateup_kernel,
        out_shape=jax.ShapeDtypeStruct((m, f), BF16),
        grid=(m // tm, f // tn),
        in_specs=[pl.BlockSpec((tm, d), lambda i, j: (i, 0)),
                  pl.BlockSpec((d, tn), lambda i, j: (0, j)),
                  pl.BlockSpec((d, tn), lambda i, j: (0, j))],
        out_specs=pl.BlockSpec((tm, tn), lambda i, j: (i, j)),
        compiler_params=pltpu.CompilerParams(
            dimension_semantics=("arbitrary", "arbitrary"),
            vmem_limit_bytes=_vmem_limit(tm * d * 2 + 2 * d * tn * 2 + tm * tn * 2, 4 * tm * tn * 4)),
        name="swiglu_up",
    )(h, wg, wu)


def _accumulate(a_ref, w_ref, o_ref):
    k = pl.program_id(1)

    @pl.when(k == 0)
    def _():
        o_ref[...] = _dot(a_ref[...], w_ref[...])

    @pl.when(k > 0)
    def _():
        o_ref[...] += _dot(a_ref[...], w_ref[...])


def _down_kernel(a_ref, w_ref, x_ref, gpost_ref, gnext_ref, o_ref, h_ref, *, nk, scale, rows):
    _accumulate(a_ref, w_ref, o_ref)

    @pl.when(pl.program_id(1) == nk - 1)
    def _():
        def body(s, carry):
            sl = pl.ds(pl.multiple_of(s * rows, rows), rows)
            xn = x_ref[sl, :] + scale * _rms(o_ref[sl, :], gpost_ref[...])
            o_ref[sl, :] = xn
            h_ref[sl, :] = _rms(xn, gnext_ref[...]).astype(h_ref.dtype)
            return carry
        lax.fori_loop(0, o_ref.shape[0] // rows, body, 0)


def _down(a, w, x, gpost, gnext, scale):
    m, kdim = a.shape
    d = w.shape[1]
    tm = _tile(m, 512, 16)
    tk = _tile(kdim, 1024, V7X_LANES)
    rows = _tile(tm, 64, 8)
    nk = kdim // tk
    return pl.pallas_call(
        functools.partial(_down_kernel, nk=nk, scale=scale, rows=rows),
        out_shape=(jax.ShapeDtypeStruct((m, d), F32), jax.ShapeDtypeStruct((m, d), BF16)),
        grid=(m // tm, nk),
        in_specs=[pl.BlockSpec((tm, tk), lambda i, k: (i, k)),
                  pl.BlockSpec((tk, d), lambda i, k: (k, 0)),
                  pl.BlockSpec((tm, d), lambda i, k: (i, 0), pipeline_mode=pl.Buffered(1)),
                  pl.BlockSpec((1, d), lambda i, k: (0, 0)),
                  pl.BlockSpec((1, d), lambda i, k: (0, 0))],
        out_specs=(pl.BlockSpec((tm, d), lambda i, k: (i, 0)),
                   pl.BlockSpec((tm, d), lambda i, k: (i, 0))),
        compiler_params=pltpu.CompilerParams(
            dimension_semantics=("arbitrary", "arbitrary"),
            vmem_limit_bytes=_vmem_limit(tm * tk * 2 + tk * d * 2 + tm * d * 6, tm * d * 4 + 6 * rows * d * 4)),
        name="down_norm_residual",
    )(a, w, x, gpost, gnext)


def _ple_kernel(a_ref, w_ref, x_ref, p_ref, wp_ref, gpost_ref, o_ref, *, nk, rows):
    _accumulate(a_ref, w_ref, o_ref)

    @pl.when(pl.program_id(1) == nk - 1)
    def _():
        def body(s, carry):
            sl = pl.ds(pl.multiple_of(s * rows, rows), rows)
            gate = jax.nn.sigmoid(o_ref[sl, :])
            proj = _dot(p_ref[sl, :], wp_ref[...])
            o_ref[sl, :] = x_ref[sl, :] + _rms(proj * gate, gpost_ref[...])
            return carry
        lax.fori_loop(0, o_ref.shape[0] // rows, body, 0)


def _ple(a, w, x, p, wp, gpost):
    m, kdim = a.shape
    d = w.shape[1]
    pd = p.shape[1]
    tm = _tile(m, 512, 16)
    tk = _tile(kdim, 1024, V7X_LANES)
    rows = _tile(tm, 64, 8)
    nk = kdim // tk
    return pl.pallas_call(
        functools.partial(_ple_kernel, nk=nk, rows=rows),
        out_shape=jax.ShapeDtypeStruct((m, d), F32),
        grid=(m // tm, nk),
        in_specs=[pl.BlockSpec((tm, tk), lambda i, k: (i, k)),
                  pl.BlockSpec((tk, d), lambda i, k: (k, 0)),
                  pl.BlockSpec((tm, d), lambda i, k: (i, 0), pipeline_mode=pl.Buffered(1)),
                  pl.BlockSpec((tm, pd), lambda i, k: (i, 0)),
                  pl.BlockSpec((pd, d), lambda i, k: (0, 0)),
                  pl.BlockSpec((1, d), lambda i, k: (0, 0))],
        out_specs=pl.BlockSpec((tm, d), lambda i, k: (i, 0)),
        compiler_params=pltpu.CompilerParams(
            dimension_semantics=("arbitrary", "arbitrary"),
            vmem_limit_bytes=_vmem_limit(tm * tk * 2 + tk * d * 2 + tm * d * 4 + tm * pd * 2 + pd * d * 2,
                                         tm * d * 4 + 6 * rows * d * 4)),
        name="ple_gate_norm_residual",
    )(a, w, x, p, wp, gpost)


def _qk_kernel(h_ref, w_ref, cos_ref, sa_ref, sb_ref, o_ref, *, q_tiles, qscale):
    j = pl.program_id(1)
    z = _dot(h_ref[...], w_ref[...])
    c, sa, sb = cos_ref[...], sa_ref[...], sb_ref[...]
    sc = jnp.where(j < q_tiles, qscale, 1.0).astype(F32)
    outs = []
    for g in range(z.shape[1] // V7X_LANES):
        zg = z[:, g * V7X_LANES:(g + 1) * V7X_LANES]
        rg = zg * c + pltpu.roll(zg, V7X_LANES - 16, 1) * sa + pltpu.roll(zg, 16, 1) * sb
        outs.append(rg * sc)
    o_ref[...] = jnp.concatenate(outs, axis=1).astype(o_ref.dtype)


def _qk_proj(h, w, cos_t, sa_t, sb_t, *, seq, qk_cols, qscale):
    m, d = h.shape
    n = w.shape[1]
    tm = _tile(seq, 1024, 16)
    tn = _tile(qk_cols, 512, V7X_LANES)
    tpb = seq // tm
    return pl.pallas_call(
        functools.partial(_qk_kernel, q_tiles=qk_cols // tn, qscale=qscale),
        out_shape=jax.ShapeDtypeStruct((m, n), BF16),
        grid=(m // tm, n // tn),
        in_specs=[pl.BlockSpec((tm, d), lambda i, j: (i, 0)),
                  pl.BlockSpec((d, tn), lambda i, j: (0, j)),
                  pl.BlockSpec((tm, V7X_LANES), lambda i, j: (i % tpb, 0)),
                  pl.BlockSpec((tm, V7X_LANES), lambda i, j: (i % tpb, 0)),
                  pl.BlockSpec((tm, V7X_LANES), lambda i, j: (i % tpb, 0))],
        out_specs=pl.BlockSpec((tm, tn), lambda i, j: (i, j)),
        compiler_params=pltpu.CompilerParams(
            dimension_semantics=("arbitrary", "arbitrary"),
            vmem_limit_bytes=_vmem_limit(tm * d * 2 + d * tn * 2 + tm * tn * 2 + 3 * tm * V7X_LANES * 4,
                                         5 * tm * tn * 4)),
        name="qk_proj_rotary",
    )(h, w, cos_t, sa_t, sb_t)


def _vt_proj_kernel(wt_ref, h_ref, o_ref):
    o_ref[...] = _dot_nt(wt_ref[...], h_ref[...]).astype(o_ref.dtype)


def _vt_proj(h, wt):
    m, d = h.shape
    n = wt.shape[0]
    tm = _tile(m, 1024, V7X_LANES)
    tn = _tile(n, 512, 16)
    return pl.pallas_call(
        _vt_proj_kernel,
        out_shape=jax.ShapeDtypeStruct((n, m), BF16),
        grid=(m // tm, n // tn),
        in_specs=[pl.BlockSpec((tn, d), lambda i, j: (j, 0)),
                  pl.BlockSpec((tm, d), lambda i, j: (i, 0))],
        out_specs=pl.BlockSpec((tn, tm), lambda i, j: (j, i)),
        compiler_params=pltpu.CompilerParams(
            dimension_semantics=("arbitrary", "arbitrary"),
            vmem_limit_bytes=_vmem_limit(tm * d * 2 + d * tn * 2 + tm * tn * 2, 2 * tm * tn * 4)),
        name="v_proj_transposed",
    )(wt, h)


def _plain_proj_kernel(h_ref, w_ref, o_ref):
    o_ref[...] = _dot(h_ref[...], w_ref[...]).astype(o_ref.dtype)


def _plain_proj(h, w, out_dtype):
    m, d = h.shape
    n = w.shape[1]
    tm = _tile(m, 1024, 16)
    tn = _tile(n, 512, V7X_LANES)
    return pl.pallas_call(
        _plain_proj_kernel,
        out_shape=jax.ShapeDtypeStruct((m, n), out_dtype),
        grid=(m // tm, n // tn),
        in_specs=[pl.BlockSpec((tm, d), lambda i, j: (i, 0)),
                  pl.BlockSpec((d, tn), lambda i, j: (0, j))],
        out_specs=pl.BlockSpec((tm, tn), lambda i, j: (i, j)),
        compiler_params=pltpu.CompilerParams(
            dimension_semantics=("arbitrary", "arbitrary"),
            vmem_limit_bytes=_vmem_limit(tm * d * 2 + d * tn * 2 + tm * tn * 4, 2 * tm * tn * 4)),
        name="rwkv_proj",
    )(h, w)


def _attn_kernel(q_ref, k_ref, vt_ref, lq1_ref, lk1_ref, lq2_ref, lk2_ref, sg_ref, o_ref,
                 m_ref, l_ref, acc_ref, *, tq, hd, lambda_init):
    qi = pl.program_id(2)
    q = q_ref[...]
    m_ref[...] = jnp.full(m_ref.shape, -jnp.inf, F32)
    l_ref[...] = jnp.zeros(l_ref.shape, F32)
    acc_ref[...] = jnp.zeros(acc_ref.shape, F32)

    def tile(start, diagonal):
        kt = k_ref[pl.ds(start, tq), :]
        vt = vt_ref[:, pl.ds(start, tq)]
        ss = [_dot_nt(kt[:, c * hd:(c + 1) * hd], q[:, c * hd:(c + 1) * hd]) for c in range(2)]
        if diagonal:
            sh = ATTN_CHUNK.bit_length() - 1
            ck = lax.shift_right_logical(lax.broadcasted_iota(jnp.int32, ss[0].shape, 0), sh)
            rq = lax.shift_right_logical(lax.broadcasted_iota(jnp.int32, ss[0].shape, 1), sh)
            ss = [jnp.where(ck <= rq, s, -jnp.inf) for s in ss]
        m_olds = [m_ref[c] for c in range(2)]
        m_news = [jnp.maximum(m_old, jnp.max(s, axis=0, keepdims=True)) for m_old, s in zip(m_olds, ss)]
        ps = [jnp.exp2(s - m_new) for s, m_new in zip(ss, m_news)]
        alphas = [jnp.exp2(m_old - m_new) for m_old, m_new in zip(m_olds, m_news)]
        pvs = [_dot(vt, p.astype(BF16)) for p in ps]
        for c in range(2):
            l_ref[c] = alphas[c] * l_ref[c] + jnp.sum(ps[c], axis=0, keepdims=True)
            acc_ref[c] = alphas[c] * acc_ref[c] + pvs[c]
            m_ref[c] = m_news[c]

    def body(j, carry):
        tile(pl.multiple_of(j * tq, tq), False)
        return carry

    lax.fori_loop(0, qi, body, 0)
    tile(pl.multiple_of(qi * tq, tq), True)

    lam = (jnp.exp(jnp.sum(lq1_ref[...] * lk1_ref[...], axis=-1, keepdims=True))
           - jnp.exp(jnp.sum(lq2_ref[...] * lk2_ref[...], axis=-1, keepdims=True)) + lambda_init)
    o = acc_ref[0] / l_ref[0] - lam * (acc_ref[1] / l_ref[1])
    o = o * lax.rsqrt(jnp.mean(o * o, axis=0, keepdims=True) + RMS_EPS) * sg_ref[...] * (1.0 - lambda_init)
    o_ref[...] = o.T.astype(o_ref.dtype)


def _diff_attention(zqk, vt, lq1, lk1, lq2, lk2, subln_g, *, batch, seq, heads, hd, out_cols, lambda_init):
    m = zqk.shape[0]
    vd = 2 * hd
    tq = _tile(seq, 512, V7X_LANES)
    nq = seq // tq
    vec = pl.BlockSpec((1, hd), lambda b, h, i: (0, 0))
    return pl.pallas_call(
        functools.partial(_attn_kernel, tq=tq, hd=hd, lambda_init=lambda_init),
        out_shape=jax.ShapeDtypeStruct((m, out_cols), BF16),
        grid=(batch, heads, nq),
        in_specs=[pl.BlockSpec((tq, vd), lambda b, h, i: (b * nq + i, h)),
                  pl.BlockSpec((seq, vd), lambda b, h, i: (b, heads + h)),
                  pl.BlockSpec((vd, seq), lambda b, h, i: (h, b)),
                  vec, vec, vec, vec,
                  pl.BlockSpec((vd, 1), lambda b, h, i: (0, 0))],
        out_specs=pl.BlockSpec((tq, vd), lambda b, h, i: (b * nq + i, h)),
        scratch_shapes=[pltpu.VMEM((2, 1, tq), F32), pltpu.VMEM((2, 1, tq), F32),
                        pltpu.VMEM((2, vd, tq), F32)],
        compiler_params=pltpu.CompilerParams(
            dimension_semantics=("arbitrary", "arbitrary", "arbitrary"),
            vmem_limit_bytes=_vmem_limit(2 * tq * vd * 2 + 2 * seq * vd * 2,
                                         2 * tq * vd * 4 + 10 * tq * tq * 4)),
        name="diff_attention",
    )(zqk, zqk, vt, lq1, lk1, lq2, lk2, subln_g)


def _split3_bf16(x):
    h1 = x.astype(BF16)
    r1 = x - h1.astype(F32)
    h2 = r1.astype(BF16)
    h3 = (r1 - h2.astype(F32)).astype(BF16)
    return h1, h2, h3


def _rwkv_kernel(z_ref, mu_ref, w0_ref, w2_ref, a0_ref, a2_ref, g2_ref, kk_ref, ka_ref, rk_ref,
                 lnw_ref, lnb_ref, alias_ref, o_ref,
                 prev_ref, st_ref, zm_ref, lw_ref, as_ref, gt_ref, *, rw, hn, pairs_per_iter):
    del alias_ref
    c = pl.program_id(1)
    lt = z_ref.shape[0]
    ln = V7X_LANES

    @pl.when(c == 0)
    def _():
        prev_ref[...] = jnp.zeros(prev_ref.shape, F32)
        st_ref[...] = jnp.zeros(st_ref.shape, F32)

    z = z_ref[...]
    row = lax.broadcasted_iota(jnp.int32, z.shape, 0)
    zprev = jnp.where(row == 0, prev_ref[0:1, :], pltpu.roll(z, 1, 0))
    prev_ref[0:1, :] = z[lt - 1:lt, :]
    zm = z + (zprev - z) * mu_ref[...]
    zm_ref[...] = zm
    dl = w2_ref.shape[0]
    al = a2_ref.shape[0]
    wl = zm[:, 3 * rw:3 * rw + dl]
    aa = zm[:, 3 * rw + dl:3 * rw + dl + al]
    gl = zm[:, 3 * rw + dl + al:]
    wpre = w0_ref[...] + _dot(jnp.tanh(wl).astype(BF16), w2_ref[...])
    softplus = jnp.maximum(-wpre, 0.0) + jnp.log(1.0 + jnp.exp(-jnp.abs(wpre)))
    lw_ref[...] = -jnp.exp(-softplus - 0.5)
    as_ref[...] = jax.nn.sigmoid(a0_ref[...] + _dot(aa.astype(BF16), a2_ref[...]))
    gt_ref[...] = _dot(jax.nn.sigmoid(gl).astype(BF16), g2_ref[...])

    lane = lax.broadcasted_iota(jnp.int32, (lt, ln), 1)
    first = lane < hn
    r2 = lax.broadcasted_iota(jnp.int32, (2 * hn, 2 * hn), 0)
    c2 = lax.broadcasted_iota(jnp.int32, (2 * hn, 2 * hn), 1)
    hsh = hn.bit_length() - 1
    same = lax.shift_right_logical(r2, hsh) == lax.shift_right_logical(c2, hsh)
    strict = same & (r2 > c2)
    incl = same & (r2 >= c2)
    eye = (r2 == c2).astype(F32)
    tri = (lax.broadcasted_iota(jnp.int32, (lt, lt), 0)
           >= lax.broadcasted_iota(jnp.int32, (lt, lt), 1)).astype(BF16)
    zeros = jnp.zeros((2 * hn, ln), F32)

    def headsum(x):
        s0 = jnp.sum(jnp.where(first, x, 0.0), axis=-1, keepdims=True)
        s1 = jnp.sum(jnp.where(first, 0.0, x), axis=-1, keepdims=True)
        return jnp.where(first, s0, s1)

    def bd(x):
        return jnp.concatenate([jnp.where(first, x, 0.0), jnp.where(first, 0.0, x)], axis=0)

    def fold(x):
        return x[:hn] + x[hn:]

    def pairs(offs):
        n2 = 2 * hn
        sls = [pl.ds(off, ln) for off in offs]
        rs = [zm_ref[:, pl.ds(off, ln)] for off in offs]
        ks = [zm_ref[:, pl.ds(rw + off, ln)] for off in offs]
        vs = [zm_ref[:, pl.ds(2 * rw + off, ln)] for off in offs]
        asgs = [as_ref[:, sl] for sl in sls]
        kks = [k * kk_ref[:, sl] for k, sl in zip(ks, sls)]
        kks = [kk / jnp.maximum(jnp.sqrt(headsum(kk * kk)), 1e-12) for kk in kks]
        kns = [k * (1.0 + (asg - 1.0) * ka_ref[:, sl]) for k, asg, sl in zip(ks, asgs, sls)]
        bns = [kk * asg for kk, asg in zip(kks, asgs)]

        lws = [lw_ref[:, sl] for sl in sls]
        splits = [_split3_bf16(lw) for lw in lws]
        cums = [_dot(tri, h1) + _dot(tri, h2) + _dot(tri, h3) for h1, h2, h3 in splits]
        tots = [cum[lt - 1:lt, :] for cum in cums]
        e_invs = [jnp.exp(-cum) for cum in cums]
        ats = [bd(-kk * jnp.exp(cum - lw)) for kk, cum, lw in zip(kks, cums, lws)]
        rts = [bd(r * jnp.exp(cum)) for r, cum in zip(rs, cums)]
        vbs = [bd(v) for v in vs]

        fulls = [_dot_nt(jnp.concatenate([at, rt], axis=0).astype(BF16),
                         jnp.concatenate([bd(bn * e), bd(kn * e)], axis=0).astype(BF16))
                 for at, rt, bn, kn, e in zip(ats, rts, bns, kns, e_invs)]
        a_abs = [jnp.where(strict, f[:n2, :n2], 0.0) for f in fulls]
        a_aks = [jnp.where(strict, f[:n2, n2:], 0.0) for f in fulls]
        a_rs = [jnp.concatenate([jnp.where(incl, f[n2:, :n2], 0.0), jnp.where(incl, f[n2:, n2:], 0.0)],
                                axis=1).astype(BF16) for f in fulls]

        akvs = [_dot(a_ak.astype(BF16), vb.astype(BF16)) for a_ak, vb in zip(a_aks, vbs)]
        zqs = [jnp.concatenate([at, akv], axis=1) for at, akv in zip(ats, akvs)]
        apows = a_abs
        steps = max(1, (lt - 1).bit_length())
        for it in range(steps):
            if it + 1 < steps:
                aws = [_dot(ap.astype(BF16), jnp.concatenate([zq, ap], axis=1).astype(BF16))
                       for ap, zq in zip(apows, zqs)]
                zqs = [zq + aw[:, :2 * ln] for zq, aw in zip(zqs, aws)]
                apows = [aw[:, 2 * ln:] for aw in aws]
            else:
                zqs = [zq + _dot(ap.astype(BF16), zq.astype(BF16)) for ap, zq in zip(apows, zqs)]

        rhs2s = [jnp.concatenate([zq, jnp.concatenate([zeros, vb], axis=1)], axis=0).astype(BF16)
                 for zq, vb in zip(zqs, vbs)]
        rys = [_dot(a_r, rhs2) for a_r, rhs2 in zip(a_rs, rhs2s)]
        e_ends = [jnp.exp(tot - cum) for tot, cum in zip(tots, cums)]
        ghs = [_dot_tn(jnp.concatenate([bd(bn * e), bd(kn * e)], axis=0).astype(BF16), rhs2)
               for bn, kn, e, rhs2 in zip(bns, kns, e_ends, rhs2s)]
        lhs4s = [jnp.concatenate([fold(rt + ry[:, :ln]), fold(gh[:, :ln] + eye * jnp.exp(tot))],
                                 axis=0).astype(BF16)
                 for rt, ry, gh, tot in zip(rts, rys, ghs, tots)]
        outs = [_dot(lhs4, bd(st_ref[:, sl]).astype(BF16)) for lhs4, sl in zip(lhs4s, sls)]
        for sl, out, ry, gh, r, kn, v in zip(sls, outs, rys, ghs, rs, kns, vs):
            y = out[:hn] + fold(ry[:, ln:])
            st_ref[:, sl] = out[hn:] + fold(gh[:, ln:])
            mean = headsum(y) * (1.0 / hn)
            yc = y - mean
            var = headsum(yc * yc) * (1.0 / hn)
            yn = yc * lax.rsqrt(var + RW_LN_EPS) * lnw_ref[:, sl] + lnb_ref[:, sl]
            bonus = headsum(r * kn * rk_ref[:, sl]) * v
            o_ref[:, sl] = ((yn + bonus) * gt_ref[:, sl]).astype(o_ref.dtype)

    npairs = rw // ln

    def body(i, carry):
        pairs([pl.multiple_of((i * pairs_per_iter + u) * ln, ln) for u in range(pairs_per_iter)])
        return carry

    lax.fori_loop(0, npairs // pairs_per_iter, body, 0)


def _rwkv(zr, omix, mu, w0, w2, a0, a2, g2, k_k, k_a, r_k, ln_w, ln_b, *, batch, seq, rw, hn):
    m, zc = zr.shape
    assert hn * 2 == V7X_LANES and RW_CHUNK == hn and seq % RW_CHUNK == 0
    lt = RW_CHUNK
    nc = seq // lt
    npairs = rw // V7X_LANES
    ppi = _tile(npairs, 8, 1)
    ocol = omix.shape[1] // rw - 1
    assert (ocol + 1) * rw == omix.shape[1]
    row = lambda n: pl.BlockSpec((1, n), lambda b, c: (0, 0))
    full = lambda a: pl.BlockSpec(a.shape, lambda b, c: (0, 0))
    return pl.pallas_call(
        functools.partial(_rwkv_kernel, rw=rw, hn=hn, pairs_per_iter=ppi),
        out_shape=jax.ShapeDtypeStruct(omix.shape, omix.dtype),
        grid=(batch, nc),
        in_specs=[pl.BlockSpec((lt, zc), lambda b, c: (b * nc + c, 0)),
                  row(zc), row(rw), full(w2), row(rw), full(a2), full(g2),
                  row(rw), row(rw), row(rw), row(rw), row(rw),
                  pl.BlockSpec(memory_space=pl.ANY)],
        out_specs=pl.BlockSpec((lt, rw), lambda b, c: (b * nc + c, ocol)),
        scratch_shapes=[pltpu.VMEM((8, zc), F32), pltpu.VMEM((hn, rw), F32),
                        pltpu.VMEM((lt, zc), F32), pltpu.VMEM((lt, rw), F32),
                        pltpu.VMEM((lt, rw), F32), pltpu.VMEM((lt, rw), F32)],
        input_output_aliases={12: 0},
        compiler_params=pltpu.CompilerParams(
            dimension_semantics=("arbitrary", "arbitrary"),
            vmem_limit_bytes=_vmem_limit(lt * zc * 4 + (w2.size + a2.size + g2.size) * 2 + lt * rw * 2,
                                         (8 + lt) * zc * 4 + 4 * lt * rw * 4 + (8 << 20))),
        name="rwkv7_chunked",
    )(zr, mu, w0, w2, a0, a2, g2, k_k, k_a, r_k, ln_w, ln_b, omix)


def _pad_to(a, axis, n):
    pad = n - a.shape[axis]
    if pad == 0:
        return a
    widths = [(0, 0)] * a.ndim
    widths[axis] = (0, pad)
    return jnp.pad(a, widths)


def _ffn(h, x, w_gate, w_up, w_down, gpost, gnext):
    fp = _round_up(w_gate.shape[1], 512)
    wg = _pad_to(w_gate, 1, fp).astype(BF16)
    wu = _pad_to(w_up, 1, fp).astype(BF16)
    wd = _pad_to(w_down, 0, fp).astype(BF16)
    act = _gateup(h, wg, wu)
    return _down(act, wd, x, gpost, gnext, 0.5)


def kernel(x, p, ffn1_pre_g, ffn1_w_gate, ffn1_w_up, ffn1_w_down, ffn1_post_g, mix_pre_g, w_in, diff_lambda_q1, diff_lambda_k1, diff_lambda_q2, diff_lambda_k2, diff_subln_g, rwkv_mu, rwkv_w0, rwkv_w2, rwkv_a0, rwkv_a2, rwkv_g2, rwkv_k_k, rwkv_k_a, rwkv_r_k, rwkv_ln_w, rwkv_ln_b, w_out, mix_post_g, ffn2_pre_g, ffn2_w_gate, ffn2_w_up, ffn2_w_down, ffn2_post_g, ple_pre_g, ple_w_gate, ple_w_proj, ple_post_g):
    batch, seq, d = x.shape
    depth = p.shape[0]
    m = batch * seq
    hd = diff_subln_g.shape[-1] // 2
    rw = rwkv_w2.shape[-1]
    da = w_out.shape[1] - rw
    heads = da // (2 * hd)
    qk = heads * 2 * hd
    rheads, hn = rwkv_r_k.shape[1], rwkv_r_k.shape[2]
    dl, al, gl = rwkv_w2.shape[1], rwkv_a2.shape[1], rwkv_g2.shape[1]
    dlp, alp, glp = (_round_up(n, V7X_LANES) for n in (dl, al, gl))
    rot = hd // 4
    assert rheads * hn == rw and da == rw and w_in.shape[2] == 2 * qk + da + 3 * rw + dl + al + gl

    inv = ROPE_THETA ** (-jnp.arange(0, rot, 2, dtype=F32) / rot)
    ang = jnp.arange(seq).astype(F32)[:, None] * inv[None, :]
    half = rot // 2
    ones = jnp.ones((seq, hd - rot), F32)
    zeros_h = jnp.zeros((seq, half), F32)
    zeros_r = jnp.zeros((seq, hd - rot), F32)
    cos_t = jnp.concatenate([jnp.cos(ang), jnp.cos(ang), ones], axis=1)
    sa_t = jnp.concatenate([-jnp.sin(ang), zeros_h, zeros_r], axis=1)
    sb_t = jnp.concatenate([zeros_h, jnp.sin(ang), zeros_r], axis=1)
    assert half == 16 and hd == V7X_LANES

    row = lambda a: a.reshape(1, -1).astype(F32)
    xf = x.reshape(m, d)
    for i in range(depth):
        lambda_init = 0.8 - 0.6 * math.exp(-0.3 * i)

        h = _prenorm(xf, row(ffn1_pre_g[i]))
        xf, h = _ffn(h, xf, ffn1_w_gate[i], ffn1_w_up[i], ffn1_w_down[i],
                     row(ffn1_post_g[i]), row(mix_pre_g[i]))

        wi = w_in[i]
        c0 = 2 * qk + da + 3 * rw
        w_r = jnp.concatenate([wi[:, 2 * qk + da:c0],
                               _pad_to(wi[:, c0:c0 + dl], 1, dlp),
                               _pad_to(wi[:, c0 + dl:c0 + dl + al], 1, alp),
                               _pad_to(wi[:, c0 + dl + al:], 1, glp)], axis=1).astype(BF16)
        mu = rwkv_mu[i]
        mu_p = jnp.concatenate([mu[:3 * rw], _pad_to(mu[3 * rw:3 * rw + dl], 0, dlp),
                                _pad_to(mu[3 * rw + dl:3 * rw + dl + al], 0, alp),
                                _pad_to(mu[3 * rw + dl + al:], 0, glp)])
        zqk = _qk_proj(h, wi[:, :2 * qk].astype(BF16), cos_t, sa_t, sb_t, seq=seq, qk_cols=qk,
                       qscale=hd ** -0.5 * math.log2(math.e))
        vt = _vt_proj(h, wi[:, 2 * qk:2 * qk + da].T.astype(BF16))
        zr = _plain_proj(h, w_r, F32)

        omix = _diff_attention(zqk, vt, row(diff_lambda_q1[i]), row(diff_lambda_k1[i]),
                               row(diff_lambda_q2[i]), row(diff_lambda_k2[i]),
                               diff_subln_g[i].reshape(-1, 1).astype(F32),
                               batch=batch, seq=seq, heads=heads, hd=hd, out_cols=da + rw,
                               lambda_init=lambda_init)
        omix = _rwkv(zr, omix, row(mu_p), row(rwkv_w0[i]),
                     _pad_to(rwkv_w2[i].astype(BF16), 0, dlp), row(rwkv_a0[i]),
                     _pad_to(rwkv_a2[i].astype(BF16), 0, alp), _pad_to(rwkv_g2[i].astype(BF16), 0, glp),
                     row(rwkv_k_k[i]), row(rwkv_k_a[i]), row(rwkv_r_k[i]),
                     row(rwkv_ln_w[i]), row(rwkv_ln_b[i]), batch=batch, seq=seq, rw=rw, hn=hn)

        xf, h = _down(omix, w_out[i].astype(BF16), xf, row(mix_post_g[i]), row(ffn2_pre_g[i]), 1.0)
        xf, h = _ffn(h, xf, ffn2_w_gate[i], ffn2_w_up[i], ffn2_w_down[i],
                     row(ffn2_post_g[i]), row(ple_pre_g[i]))
        xf = _ple(h, ple_w_gate[i].astype(BF16), xf, p[i].reshape(m, -1).astype(BF16),
                  ple_w_proj[i].astype(BF16), row(ple_post_g[i]))
    return xf.reshape(batch, seq, d)
```

```python
import functools
import math

import jax
import jax.numpy as jnp
from jax import lax
from jax.experimental import pallas as pl
from jax.experimental.pallas import tpu as pltpu

F32 = jnp.float32
BF16 = jnp.bfloat16

RMS_EPS = 1e-6
RW_LN_EPS = 64e-5
ATTN_CHUNK = 64
ROPE_THETA = 500000.0
RW_CHUNK = 64

V7X_LANES = 128
V7X_VMEM_CAP_BYTES = 58 * 1024 * 1024


def _tile(n, pref, align):
    t = (min(n, pref) // align) * align
    while t >= align:
        if n % t == 0:
            return t
        t -= align
    return n


def _round_up(n, m):
    return (n + m - 1) // m * m


def _vmem_limit(pipelined_bytes, resident_bytes):
    need = 2 * pipelined_bytes + resident_bytes + (4 << 20)
    return int(min(V7X_VMEM_CAP_BYTES, max(need, 16 << 20)))


def _rms(x, g, eps=RMS_EPS):
    return x * lax.rsqrt(jnp.mean(x * x, axis=-1, keepdims=True) + eps) * g


def _dot(a, b):
    return jnp.dot(a, b, preferred_element_type=F32)


def _dot_nt(a, b):
    return lax.dot_general(a, b, (((1,), (1,)), ((), ())), preferred_element_type=F32)


def _dot_tn(a, b):
    return lax.dot_general(a, b, (((0,), (0,)), ((), ())), preferred_element_type=F32)


def _prenorm_kernel(x_ref, g_ref, h_ref):
    h_ref[...] = _rms(x_ref[...], g_ref[...]).astype(h_ref.dtype)


def _prenorm(x, g):
    m, d = x.shape
    tm = _tile(m, 256, 8)
    return pl.pallas_call(
        _prenorm_kernel,
        out_shape=jax.ShapeDtypeStruct((m, d), BF16),
        grid=(m // tm,),
        in_specs=[pl.BlockSpec((tm, d), lambda i: (i, 0)),
                  pl.BlockSpec((1, d), lambda i: (0, 0))],
        out_specs=pl.BlockSpec((tm, d), lambda i: (i, 0)),
        compiler_params=pltpu.CompilerParams(
            dimension_semantics=("arbitrary",),
            vmem_limit_bytes=_vmem_limit(tm * d * 6, 3 * tm * d * 4)),
        name="prenorm",
    )(x, g)


def _gateup_kernel(h_ref, wg_ref, wu_ref, o_ref):
    h = h_ref[...]
    g = _dot(h, wg_ref[...].astype(BF16))
    u = _dot(h, wu_ref[...].astype(BF16))
    o_ref[...] = (g * jax.nn.sigmoid(g) * u).astype(o_ref.dtype)


def _gateup(h, wg, wu):
    m, d = h.shape
    f = wg.shape[1]
    tm = _tile(m, 1024, 16)
    tn = _tile(f, 256, V7X_LANES)
    return pl.pallas_call(
        _gateup_kernel,
        out_shape=jax.ShapeDtypeStruct((m, f), BF16),
        grid=(m // tm, f // tn),
        in_specs=[pl.BlockSpec((tm, d), lambda i, j: (i, 0)),
                  pl.BlockSpec((d, tn), lambda i, j: (0, j)),
                  pl.BlockSpec((d, tn), lambda i, j: (0, j))],
        out_specs=pl.BlockSpec((tm, tn), lambda i, j: (i, j)),
        compiler_params=pltpu.CompilerParams(
            dimension_semantics=("arbitrary", "arbitrary"),
            vmem_limit_bytes=_vmem_limit(tm * d * 2 + 2 * d * tn * 4 + tm * tn * 2,
                                         2 * d * tn * 2 + 4 * tm * tn * 4)),
        name="swiglu_up",
    )(h, wg, wu)


def _accumulate(a_ref, w_ref, o_ref, nk, last):
    k = pl.program_id(1)
    tk = a_ref.shape[1]

    @pl.when(k == 0)
    def _():
        o_ref[...] = _dot(a_ref[...], w_ref[...])

    @pl.when((k > 0) & (k < nk - (last < tk)))
    def _():
        o_ref[...] += _dot(a_ref[...], w_ref[...])

    if last < tk:
        @pl.when(k == nk - 1)
        def _():
            o_ref[...] += _dot(a_ref[:, :last], w_ref[:last, :])


def _down_kernel(a_ref, w_ref, x_ref, gpost_ref, gnext_ref, o_ref, h_ref, *, nk, last, scale, rows):
    _accumulate(a_ref, w_ref, o_ref, nk, last)

    @pl.when(pl.program_id(1) == nk - 1)
    def _():
        def body(s, carry):
            sl = pl.ds(pl.multiple_of(s * rows, rows), rows)
            xn = x_ref[sl, :] + scale * _rms(o_ref[sl, :], gpost_ref[...])
            o_ref[sl, :] = xn
            h_ref[sl, :] = _rms(xn, gnext_ref[...]).astype(h_ref.dtype)
            return carry
        lax.fori_loop(0, o_ref.shape[0] // rows, body, 0)


def _down(a, w, x, gpost, gnext, scale):
    m, kdim = a.shape
    d = w.shape[1]
    tm = _tile(m, 512, 16)
    tk = min(512, kdim)
    rows = _tile(tm, 64, 8)
    nk = pl.cdiv(kdim, tk)
    last = kdim - (nk - 1) * tk
    assert nk >= 2 and last % V7X_LANES == 0
    return pl.pallas_call(
        functools.partial(_down_kernel, nk=nk, last=last, scale=scale, rows=rows),
        out_shape=(jax.ShapeDtypeStruct((m, d), F32), jax.ShapeDtypeStruct((m, d), BF16)),
        grid=(m // tm, nk),
        in_specs=[pl.BlockSpec((tm, tk), lambda i, k: (i, k)),
                  pl.BlockSpec((tk, d), lambda i, k: (k, 0)),
                  pl.BlockSpec((tm, d), lambda i, k: (i, 0)),
                  pl.BlockSpec((1, d), lambda i, k: (0, 0)),
                  pl.BlockSpec((1, d), lambda i, k: (0, 0))],
        out_specs=(pl.BlockSpec((tm, d), lambda i, k: (i, 0)),
                   pl.BlockSpec((tm, d), lambda i, k: (i, 0))),
        compiler_params=pltpu.CompilerParams(
            dimension_semantics=("arbitrary", "arbitrary"),
            vmem_limit_bytes=_vmem_limit(tm * tk * 2 + tk * d * 2 + tm * d * 10, 6 * rows * d * 4)),
        name="down_norm_residual",
    )(a, w, x, gpost, gnext)


def _ple_kernel(a_ref, w_ref, x_ref, p_ref, wp_ref, gpost_ref, o_ref, *, nk, last, rows):
    _accumulate(a_ref, w_ref, o_ref, nk, last)

    @pl.when(pl.program_id(1) == nk - 1)
    def _():
        def body(s, carry):
            sl = pl.ds(pl.multiple_of(s * rows, rows), rows)
            gate = jax.nn.sigmoid(o_ref[sl, :])
            proj = _dot(p_ref[sl, :], wp_ref[...])
            o_ref[sl, :] = x_ref[sl, :] + _rms(proj * gate, gpost_ref[...])
            return carry
        lax.fori_loop(0, o_ref.shape[0] // rows, body, 0)


def _ple(a, w, x, p, wp, gpost):
    m, kdim = a.shape
    d = w.shape[1]
    pd = p.shape[1]
    tm = _tile(m, 512, 16)
    tk = min(512, kdim)
    rows = _tile(tm, 64, 8)
    nk = pl.cdiv(kdim, tk)
    last = kdim - (nk - 1) * tk
    assert nk >= 2 and last % V7X_LANES == 0
    return pl.pallas_call(
        functools.partial(_ple_kernel, nk=nk, last=last, rows=rows),
        out_shape=jax.ShapeDtypeStruct((m, d), F32),
        grid=(m // tm, nk),
        in_specs=[pl.BlockSpec((tm, tk), lambda i, k: (i, k)),
                  pl.BlockSpec((tk, d), lambda i, k: (k, 0)),
                  pl.BlockSpec((tm, d), lambda i, k: (i, 0)),
                  pl.BlockSpec((tm, pd), lambda i, k: (i, 0)),
                  pl.BlockSpec((pd, d), lambda i, k: (0, 0)),
                  pl.BlockSpec((1, d), lambda i, k: (0, 0))],
        out_specs=pl.BlockSpec((tm, d), lambda i, k: (i, 0)),
        compiler_params=pltpu.CompilerParams(
            dimension_semantics=("arbitrary", "arbitrary"),
            vmem_limit_bytes=_vmem_limit(tm * tk * 2 + tk * d * 2 + tm * d * 8 + tm * pd * 2 + pd * d * 2,
                                         6 * rows * d * 4)),
        name="ple_gate_norm_residual",
    )(a, w, x, p, wp, gpost)


def _qk_kernel(h_ref, w_ref, cos_ref, sa_ref, sb_ref, o_ref, *, q_tiles, qscale):
    j = pl.program_id(1)
    z = _dot(h_ref[...], w_ref[...])
    c, sa, sb = cos_ref[...], sa_ref[...], sb_ref[...]
    sc = jnp.where(j < q_tiles, qscale, 1.0).astype(F32)
    outs = []
    for g in range(z.shape[1] // V7X_LANES):
        zg = z[:, g * V7X_LANES:(g + 1) * V7X_LANES]
        rg = zg * c + pltpu.roll(zg, V7X_LANES - 16, 1) * sa + pltpu.roll(zg, 16, 1) * sb
        outs.append(rg * sc)
    o_ref[...] = jnp.concatenate(outs, axis=1).astype(o_ref.dtype)


def _qk_proj(h, w, cos_t, sa_t, sb_t, *, seq, qk_cols, qscale):
    m, d = h.shape
    n = w.shape[1]
    tm = _tile(seq, 1024, 16)
    tn = _tile(qk_cols, 512, V7X_LANES)
    tpb = seq // tm
    return pl.pallas_call(
        functools.partial(_qk_kernel, q_tiles=qk_cols // tn, qscale=qscale),
        out_shape=jax.ShapeDtypeStruct((m, n), BF16),
        grid=(m // tm, n // tn),
        in_specs=[pl.BlockSpec((tm, d), lambda i, j: (i, 0)),
                  pl.BlockSpec((d, tn), lambda i, j: (0, j)),
                  pl.BlockSpec((tm, V7X_LANES), lambda i, j: (i % tpb, 0)),
                  pl.BlockSpec((tm, V7X_LANES), lambda i, j: (i % tpb, 0)),
                  pl.BlockSpec((tm, V7X_LANES), lambda i, j: (i % tpb, 0))],
        out_specs=pl.BlockSpec((tm, tn), lambda i, j: (i, j)),
        compiler_params=pltpu.CompilerParams(
            dimension_semantics=("arbitrary", "arbitrary"),
            vmem_limit_bytes=_vmem_limit(tm * d * 2 + d * tn * 2 + tm * tn * 2 + 3 * tm * V7X_LANES * 4,
                                         5 * tm * tn * 4)),
        name="qk_proj_rotary",
    )(h, w, cos_t, sa_t, sb_t)


def _vt_proj_kernel(wt_ref, h_ref, o_ref):
    o_ref[...] = _dot_nt(wt_ref[...], h_ref[...]).astype(o_ref.dtype)


def _vt_proj(h, wt):
    m, d = h.shape
    n = wt.shape[0]
    tm = _tile(m, 1024, V7X_LANES)
    tn = _tile(n, 512, 16)
    return pl.pallas_call(
        _vt_proj_kernel,
        out_shape=jax.ShapeDtypeStruct((n, m), BF16),
        grid=(m // tm, n // tn),
        in_specs=[pl.BlockSpec((tn, d), lambda i, j: (j, 0)),
                  pl.BlockSpec((tm, d), lambda i, j: (i, 0))],
        out_specs=pl.BlockSpec((tn, tm), lambda i, j: (j, i)),
        compiler_params=pltpu.CompilerParams(
            dimension_semantics=("arbitrary", "arbitrary"),
            vmem_limit_bytes=_vmem_limit(tm * d * 2 + d * tn * 2 + tm * tn * 2, 2 * tm * tn * 4)),
        name="v_proj_transposed",
    )(wt, h)


def _plain_proj_kernel(h_ref, w_ref, o_ref):
    o_ref[...] = _dot(h_ref[...], w_ref[...]).astype(o_ref.dtype)


def _plain_proj(h, w, out_dtype):
    m, d = h.shape
    n = w.shape[1]
    tm = _tile(m, 1024, 16)
    tn = _tile(n, 512, V7X_LANES)
    return pl.pallas_call(
        _plain_proj_kernel,
        out_shape=jax.ShapeDtypeStruct((m, n), out_dtype),
        grid=(m // tm, n // tn),
        in_specs=[pl.BlockSpec((tm, d), lambda i, j: (i, 0)),
                  pl.BlockSpec((d, tn), lambda i, j: (0, j))],
        out_specs=pl.BlockSpec((tm, tn), lambda i, j: (i, j)),
        compiler_params=pltpu.CompilerParams(
            dimension_semantics=("arbitrary", "arbitrary"),
            vmem_limit_bytes=_vmem_limit(tm * d * 2 + d * tn * 2 + tm * tn * 4, 2 * tm * tn * 4)),
        name="rwkv_proj",
    )(h, w)


def _attn_kernel(q_ref, k_ref, vt_ref, lq1_ref, lk1_ref, lq2_ref, lk2_ref, sg_ref, o_ref,
                 m_ref, l_ref, acc_ref, *, tq, hd, lambda_init):
    qi = pl.program_id(2)
    q = q_ref[...]
    m_ref[...] = jnp.full(m_ref.shape, -jnp.inf, F32)
    l_ref[...] = jnp.zeros(l_ref.shape, F32)
    acc_ref[...] = jnp.zeros(acc_ref.shape, F32)

    def tile(start, diagonal):
        kt = k_ref[pl.ds(start, tq), :]
        vt = vt_ref[:, pl.ds(start, tq)]
        ss = [_dot_nt(kt[:, c * hd:(c + 1) * hd], q[:, c * hd:(c + 1) * hd]) for c in range(2)]
        if diagonal:
            sh = ATTN_CHUNK.bit_length() - 1
            ck = lax.shift_right_logical(lax.broadcasted_iota(jnp.int32, ss[0].shape, 0), sh)
            rq = lax.shift_right_logical(lax.broadcasted_iota(jnp.int32, ss[0].shape, 1), sh)
            ss = [jnp.where(ck <= rq, s, -jnp.inf) for s in ss]
        m_olds = [m_ref[c] for c in range(2)]
        m_news = [jnp.maximum(m_old, jnp.max(s, axis=0, keepdims=True)) for m_old, s in zip(m_olds, ss)]
        ps = [jnp.exp2(s - m_new) for s, m_new in zip(ss, m_news)]
        alphas = [jnp.exp2(m_old - m_new) for m_old, m_new in zip(m_olds, m_news)]
        pvs = [_dot(vt, p.astype(BF16)) for p in ps]
        for c in range(2):
            l_ref[c] = alphas[c] * l_ref[c] + jnp.sum(ps[c], axis=0, keepdims=True)
            acc_ref[c] = alphas[c] * acc_ref[c] + pvs[c]
            m_ref[c] = m_news[c]

    def body(j, carry):
        tile(pl.multiple_of(j * tq, tq), False)
        return carry

    lax.fori_loop(0, qi, body, 0)
    tile(pl.multiple_of(qi * tq, tq), True)

    lam = (jnp.exp(jnp.sum(lq1_ref[...] * lk1_ref[...], axis=-1, keepdims=True))
           - jnp.exp(jnp.sum(lq2_ref[...] * lk2_ref[...], axis=-1, keepdims=True)) + lambda_init)
    o = acc_ref[0] / l_ref[0] - lam * (acc_ref[1] / l_ref[1])
    o = o * lax.rsqrt(jnp.mean(o * o, axis=0, keepdims=True) + RMS_EPS) * sg_ref[...] * (1.0 - lambda_init)
    o_ref[...] = o.T.astype(o_ref.dtype)


def _diff_attention(zqk, vt, lq1, lk1, lq2, lk2, subln_g, *, batch, seq, heads, hd, out_cols, lambda_init):
    m = zqk.shape[0]
    vd = 2 * hd
    tq = _tile(seq, 512, V7X_LANES)
    nq = seq // tq
    vec = pl.BlockSpec((1, hd), lambda b, h, i: (0, 0))
    return pl.pallas_call(
        functools.partial(_attn_kernel, tq=tq, hd=hd, lambda_init=lambda_init),
        out_shape=jax.ShapeDtypeStruct((m, out_cols), BF16),
        grid=(batch, heads, nq),
        in_specs=[pl.BlockSpec((tq, vd), lambda b, h, i: (b * nq + i, h)),
                  pl.BlockSpec((seq, vd), lambda b, h, i: (b, heads + h)),
                  pl.BlockSpec((vd, seq), lambda b, h, i: (h, b)),
                  vec, vec, vec, vec,
                  pl.BlockSpec((vd, 1), lambda b, h, i: (0, 0))],
        out_specs=pl.BlockSpec((tq, vd), lambda b, h, i: (b * nq + i, h)),
        scratch_shapes=[pltpu.VMEM((2, 1, tq), F32), pltpu.VMEM((2, 1, tq), F32),
                        pltpu.VMEM((2, vd, tq), F32)],
        compiler_params=pltpu.CompilerParams(
            dimension_semantics=("arbitrary", "arbitrary", "arbitrary"),
            vmem_limit_bytes=_vmem_limit(2 * tq * vd * 2 + 2 * seq * vd * 2,
                                         2 * tq * vd * 4 + 10 * tq * tq * 4)),
        name="diff_attention",
    )(zqk, zqk, vt, lq1, lk1, lq2, lk2, subln_g)


def _split3_bf16(x):
    h1 = x.astype(BF16)
    r1 = x - h1.astype(F32)
    h2 = r1.astype(BF16)
    h3 = (r1 - h2.astype(F32)).astype(BF16)
    return h1, h2, h3


def _rwkv_kernel(z_ref, mu_ref, w0_ref, w2_ref, a0_ref, a2_ref, g2_ref, kk_ref, ka_ref, rk_ref,
                 lnw_ref, lnb_ref, alias_ref, o_ref,
                 prev_ref, st_ref, zm_ref, lw_ref, as_ref, gt_ref, *, rw, hn, pairs_per_iter):
    del alias_ref
    c = pl.program_id(1)
    lt = z_ref.shape[0]
    ln = V7X_LANES

    @pl.when(c == 0)
    def _():
        prev_ref[...] = jnp.zeros(prev_ref.shape, F32)
        st_ref[...] = jnp.zeros(st_ref.shape, F32)

    z = z_ref[...]
    row = lax.broadcasted_iota(jnp.int32, z.shape, 0)
    zprev = jnp.where(row == 0, prev_ref[0:1, :], pltpu.roll(z, 1, 0))
    prev_ref[0:1, :] = z[lt - 1:lt, :]
    zm = z + (zprev - z) * mu_ref[...]
    zm_ref[...] = zm
    dl = w2_ref.shape[0]
    al = a2_ref.shape[0]
    wl = zm[:, 3 * rw:3 * rw + dl]
    aa = zm[:, 3 * rw + dl:3 * rw + dl + al]
    gl = zm[:, 3 * rw + dl + al:]
    wpre = w0_ref[...] + _dot(jnp.tanh(wl).astype(BF16), w2_ref[...])
    softplus = jnp.maximum(-wpre, 0.0) + jnp.log(1.0 + jnp.exp(-jnp.abs(wpre)))
    lw_ref[...] = -jnp.exp(-softplus - 0.5)
    as_ref[...] = jax.nn.sigmoid(a0_ref[...] + _dot(aa.astype(BF16), a2_ref[...]))
    gt_ref[...] = _dot(jax.nn.sigmoid(gl).astype(BF16), g2_ref[...])

    lane = lax.broadcasted_iota(jnp.int32, (lt, ln), 1)
    first = lane < hn
    r2 = lax.broadcasted_iota(jnp.int32, (2 * hn, 2 * hn), 0)
    c2 = lax.broadcasted_iota(jnp.int32, (2 * hn, 2 * hn), 1)
    hsh = hn.bit_length() - 1
    same = lax.shift_right_logical(r2, hsh) == lax.shift_right_logical(c2, hsh)
    strict = same & (r2 > c2)
    incl = same & (r2 >= c2)
    eye = (r2 == c2).astype(F32)
    tri = (lax.broadcasted_iota(jnp.int32, (lt, lt), 0)
           >= lax.broadcasted_iota(jnp.int32, (lt, lt), 1)).astype(BF16)
    zeros = jnp.zeros((2 * hn, ln), F32)

    def headsum(x):
        s0 = jnp.sum(jnp.where(first, x, 0.0), axis=-1, keepdims=True)
        s1 = jnp.sum(jnp.where(first, 0.0, x), axis=-1, keepdims=True)
        return jnp.where(first, s0, s1)

    def bd(x):
        return jnp.concatenate([jnp.where(first, x, 0.0), jnp.where(first, 0.0, x)], axis=0)

    def fold(x):
        return x[:hn] + x[hn:]

    def pairs(offs):
        n2 = 2 * hn
        sls = [pl.ds(off, ln) for off in offs]
        rs = [zm_ref[:, pl.ds(off, ln)] for off in offs]
        ks = [zm_ref[:, pl.ds(rw + off, ln)] for off in offs]
        vs = [zm_ref[:, pl.ds(2 * rw + off, ln)] for off in offs]
        asgs = [as_ref[:, sl] for sl in sls]
        kks = [k * kk_ref[:, sl] for k, sl in zip(ks, sls)]
        kks = [kk / jnp.maximum(jnp.sqrt(headsum(kk * kk)), 1e-12) for kk in kks]
        kns = [k * (1.0 + (asg - 1.0) * ka_ref[:, sl]) for k, asg, sl in zip(ks, asgs, sls)]
        bns = [kk * asg for kk, asg in zip(kks, asgs)]

        lws = [lw_ref[:, sl] for sl in sls]
        splits = [_split3_bf16(lw) for lw in lws]
        cums = [_dot(tri, h1) + _dot(tri, h2) + _dot(tri, h3) for h1, h2, h3 in splits]
        tots = [cum[lt - 1:lt, :] for cum in cums]
        e_invs = [jnp.exp(-cum) for cum in cums]
        ats = [bd(-kk * jnp.exp(cum - lw)) for kk, cum, lw in zip(kks, cums, lws)]
        rts = [bd(r * jnp.exp(cum)) for r, cum in zip(rs, cums)]
        vbs = [bd(v) for v in vs]

        fulls = [_dot_nt(jnp.concatenate([at, rt], axis=0).astype(BF16),
                         jnp.concatenate([bd(bn * e), bd(kn * e)], axis=0).astype(BF16))
                 for at, rt, bn, kn, e in zip(ats, rts, bns, kns, e_invs)]
        a_abs = [jnp.where(strict, f[:n2, :n2], 0.0) for f in fulls]
        a_aks = [jnp.where(strict, f[:n2, n2:], 0.0) for f in fulls]
        a_rs = [jnp.concatenate([jnp.where(incl, f[n2:, :n2], 0.0), jnp.where(incl, f[n2:, n2:], 0.0)],
                                axis=1).astype(BF16) for f in fulls]

        akvs = [_dot(a_ak.astype(BF16), vb.astype(BF16)) for a_ak, vb in zip(a_aks, vbs)]
        zqs = [jnp.concatenate([at, akv], axis=1) for at, akv in zip(ats, akvs)]
        apows = a_abs
        steps = max(1, (lt - 1).bit_length())
        for it in range(steps):
            if it + 1 < steps:
                aws = [_dot(ap.astype(BF16), jnp.concatenate([zq, ap], axis=1).astype(BF16))
                       for ap, zq in zip(apows, zqs)]
                zqs = [zq + aw[:, :2 * ln] for zq, aw in zip(zqs, aws)]
                apows = [aw[:, 2 * ln:] for aw in aws]
            else:
                zqs = [zq + _dot(ap.astype(BF16), zq.astype(BF16)) for ap, zq in zip(apows, zqs)]

        rhs2s = [jnp.concatenate([zq, jnp.concatenate([zeros, vb], axis=1)], axis=0).astype(BF16)
                 for zq, vb in zip(zqs, vbs)]
        rys = [_dot(a_r, rhs2) for a_r, rhs2 in zip(a_rs, rhs2s)]
        e_ends = [jnp.exp(tot - cum) for tot, cum in zip(tots, cums)]
        ghs = [_dot_tn(jnp.concatenate([bd(bn * e), bd(kn * e)], axis=0).astype(BF16), rhs2)
               for bn, kn, e, rhs2 in zip(bns, kns, e_ends, rhs2s)]
        lhs4s = [jnp.concatenate([fold(rt + ry[:, :ln]), fold(gh[:, :ln] + eye * jnp.exp(tot))],
                                 axis=0).astype(BF16)
                 for rt, ry, gh, tot in zip(rts, rys, ghs, tots)]
        outs = [_dot(lhs4, bd(st_ref[:, sl]).astype(BF16)) for lhs4, sl in zip(lhs4s, sls)]
        for sl, out, ry, gh, r, kn, v in zip(sls, outs, rys, ghs, rs, kns, vs):
            y = out[:hn] + fold(ry[:, ln:])
            st_ref[:, sl] = out[hn:] + fold(gh[:, ln:])
            mean = headsum(y) * (1.0 / hn)
            yc = y - mean
            var = headsum(yc * yc) * (1.0 / hn)
            yn = yc * lax.rsqrt(var + RW_LN_EPS) * lnw_ref[:, sl] + lnb_ref[:, sl]
            bonus = headsum(r * kn * rk_ref[:, sl]) * v
            o_ref[:, sl] = ((yn + bonus) * gt_ref[:, sl]).astype(o_ref.dtype)

    npairs = rw // ln

    def body(i, carry):
        pairs([pl.multiple_of((i * pairs_per_iter + u) * ln, ln) for u in range(pairs_per_iter)])
        return carry

    lax.fori_loop(0, npairs // pairs_per_iter, body, 0)


def _rwkv(zr, omix, mu, w0, w2, a0, a2, g2, k_k, k_a, r_k, ln_w, ln_b, *, batch, seq, rw, hn):
    m, zc = zr.shape
    assert hn * 2 == V7X_LANES and RW_CHUNK == hn and seq % RW_CHUNK == 0
    lt = RW_CHUNK
    nc = seq // lt
    npairs = rw // V7X_LANES
    ppi = _tile(npairs, 8, 1)
    ocol = omix.shape[1] // rw - 1
    assert (ocol + 1) * rw == omix.shape[1]
    row = lambda n: pl.BlockSpec((1, n), lambda b, c: (0, 0))
    full = lambda a: pl.BlockSpec(a.shape, lambda b, c: (0, 0))
    return pl.pallas_call(
        functools.partial(_rwkv_kernel, rw=rw, hn=hn, pairs_per_iter=ppi),
        out_shape=jax.ShapeDtypeStruct(omix.shape, omix.dtype),
        grid=(batch, nc),
        in_specs=[pl.BlockSpec((lt, zc), lambda b, c: (b * nc + c, 0)),
                  row(zc), row(rw), full(w2), row(rw), full(a2), full(g2),
                  row(rw), row(rw), row(rw), row(rw), row(rw),
                  pl.BlockSpec(memory_space=pl.ANY)],
        out_specs=pl.BlockSpec((lt, rw), lambda b, c: (b * nc + c, ocol)),
        scratch_shapes=[pltpu.VMEM((8, zc), F32), pltpu.VMEM((hn, rw), F32),
                        pltpu.VMEM((lt, zc), F32), pltpu.VMEM((lt, rw), F32),
                        pltpu.VMEM((lt, rw), F32), pltpu.VMEM((lt, rw), F32)],
        input_output_aliases={12: 0},
        compiler_params=pltpu.CompilerParams(
            dimension_semantics=("arbitrary", "arbitrary"),
            vmem_limit_bytes=_vmem_limit(lt * zc * 4 + (w2.size + a2.size + g2.size) * 2 + lt * rw * 2,
                                         (8 + lt) * zc * 4 + 4 * lt * rw * 4 + (8 << 20))),
        name="rwkv7_chunked",
    )(zr, mu, w0, w2, a0, a2, g2, k_k, k_a, r_k, ln_w, ln_b, omix)


def _pad_to(a, axis, n):
    pad = n - a.shape[axis]
    if pad == 0:
        return a
    widths = [(0, 0)] * a.ndim
    widths[axis] = (0, pad)
    return jnp.pad(a, widths)


def _ffn(h, x, w_gate, w_up, w_down, gpost, gnext):
    act = _gateup(h, w_gate, w_up)
    return _down(act, w_down.astype(BF16), x, gpost, gnext, 0.5)


def kernel(x, p, ffn1_pre_g, ffn1_w_gate, ffn1_w_up, ffn1_w_down, ffn1_post_g, mix_pre_g, w_in, diff_lambda_q1, diff_lambda_k1, diff_lambda_q2, diff_lambda_k2, diff_subln_g, rwkv_mu, rwkv_w0, rwkv_w2, rwkv_a0, rwkv_a2, rwkv_g2, rwkv_k_k, rwkv_k_a, rwkv_r_k, rwkv_ln_w, rwkv_ln_b, w_out, mix_post_g, ffn2_pre_g, ffn2_w_gate, ffn2_w_up, ffn2_w_down, ffn2_post_g, ple_pre_g, ple_w_gate, ple_w_proj, ple_post_g):
    batch, seq, d = x.shape
    depth = p.shape[0]
    m = batch * seq
    hd = diff_subln_g.shape[-1] // 2
    rw = rwkv_w2.shape[-1]
    da = w_out.shape[1] - rw
    heads = da // (2 * hd)
    qk = heads * 2 * hd
    rheads, hn = rwkv_r_k.shape[1], rwkv_r_k.shape[2]
    dl, al, gl = rwkv_w2.shape[1], rwkv_a2.shape[1], rwkv_g2.shape[1]
    dlp, alp, glp = (_round_up(n, V7X_LANES) for n in (dl, al, gl))
    rot = hd // 4
    assert rheads * hn == rw and da == rw and w_in.shape[2] == 2 * qk + da + 3 * rw + dl + al + gl

    inv = ROPE_THETA ** (-jnp.arange(0, rot, 2, dtype=F32) / rot)
    ang = jnp.arange(seq).astype(F32)[:, None] * inv[None, :]
    half = rot // 2
    ones = jnp.ones((seq, hd - rot), F32)
    zeros_h = jnp.zeros((seq, half), F32)
    zeros_r = jnp.zeros((seq, hd - rot), F32)
    cos_t = jnp.concatenate([jnp.cos(ang), jnp.cos(ang), ones], axis=1)
    sa_t = jnp.concatenate([-jnp.sin(ang), zeros_h, zeros_r], axis=1)
    sb_t = jnp.concatenate([zeros_h, jnp.sin(ang), zeros_r], axis=1)
    assert half == 16 and hd == V7X_LANES

    row = lambda a: a.reshape(1, -1).astype(F32)
    xf = x.reshape(m, d)
    for i in range(depth):
        lambda_init = 0.8 - 0.6 * math.exp(-0.3 * i)

        h = _prenorm(xf, row(ffn1_pre_g[i]))
        xf, h = _ffn(h, xf, ffn1_w_gate[i], ffn1_w_up[i], ffn1_w_down[i],
                     row(ffn1_post_g[i]), row(mix_pre_g[i]))

        wi = w_in[i]
        c0 = 2 * qk + da + 3 * rw
        w_r = jnp.concatenate([wi[:, 2 * qk + da:c0],
                               _pad_to(wi[:, c0:c0 + dl], 1, dlp),
                               _pad_to(wi[:, c0 + dl:c0 + dl + al], 1, alp),
                               _pad_to(wi[:, c0 + dl + al:], 1, glp)], axis=1).astype(BF16)
        mu = rwkv_mu[i]
        mu_p = jnp.concatenate([mu[:3 * rw], _pad_to(mu[3 * rw:3 * rw + dl], 0, dlp),
                                _pad_to(mu[3 * rw + dl:3 * rw + dl + al], 0, alp),
                                _pad_to(mu[3 * rw + dl + al:], 0, glp)])
        zqk = _qk_proj(h, wi[:, :2 * qk].astype(BF16), cos_t, sa_t, sb_t, seq=seq, qk_cols=qk,
                       qscale=hd ** -0.5 * math.log2(math.e))
        vt = _vt_proj(h, wi[:, 2 * qk:2 * qk + da].T.astype(BF16))
        zr = _plain_proj(h, w_r, F32)

        omix = _diff_attention(zqk, vt, row(diff_lambda_q1[i]), row(diff_lambda_k1[i]),
                               row(diff_lambda_q2[i]), row(diff_lambda_k2[i]),
                               diff_subln_g[i].reshape(-1, 1).astype(F32),
                               batch=batch, seq=seq, heads=heads, hd=hd, out_cols=da + rw,
                               lambda_init=lambda_init)
        omix = _rwkv(zr, omix, row(mu_p), row(rwkv_w0[i]),
                     _pad_to(rwkv_w2[i].astype(BF16), 0, dlp), row(rwkv_a0[i]),
                     _pad_to(rwkv_a2[i].astype(BF16), 0, alp), _pad_to(rwkv_g2[i].astype(BF16), 0, glp),
                     row(rwkv_k_k[i]), row(rwkv_k_a[i]), row(rwkv_r_k[i]),
                     row(rwkv_ln_w[i]), row(rwkv_ln_b[i]), batch=batch, seq=seq, rw=rw, hn=hn)

        xf, h = _down(omix, w_out[i].astype(BF16), xf, row(mix_post_g[i]), row(ffn2_pre_g[i]), 1.0)
        xf, h = _ffn(h, xf, ffn2_w_gate[i], ffn2_w_up[i], ffn2_w_down[i],
                     row(ffn2_post_g[i]), row(ple_pre_g[i]))
        xf = _ple(h, ple_w_gate[i].astype(BF16), xf, p[i].reshape(m, -1).astype(BF16),
                  ple_w_proj[i].astype(BF16), row(ple_post_g[i]))
    return xf.reshape(batch, seq, d)
```

```python
import functools
import math

import jax
import jax.numpy as jnp
from jax import lax
from jax.experimental import pallas as pl
from jax.experimental.pallas import tpu as pltpu

F32 = jnp.float32
BF16 = jnp.bfloat16

RMS_EPS = 1e-6
RW_LN_EPS = 64e-5
ATTN_CHUNK = 64
ROPE_THETA = 500000.0
RW_CHUNK = 64

V7X_LANES = 128
V7X_VMEM_CAP_BYTES = 58 * 1024 * 1024


def _tile(n, pref, align):
    t = (min(n, pref) // align) * align
    while t >= align:
        if n % t == 0:
            return t
        t -= align
    return n


def _round_up(n, m):
    return (n + m - 1) // m * m


def _vmem_limit(pipelined_bytes, resident_bytes):
    need = 2 * pipelined_bytes + resident_bytes + (4 << 20)
    return int(min(V7X_VMEM_CAP_BYTES, max(need, 16 << 20)))


def _rms(x, g, eps=RMS_EPS):
    return x * lax.rsqrt(jnp.mean(x * x, axis=-1, keepdims=True) + eps) * g


def _dot(a, b):
    return jnp.dot(a, b, preferred_element_type=F32)


def _dot_nt(a, b):
    return lax.dot_general(a, b, (((1,), (1,)), ((), ())), preferred_element_type=F32)


def _dot_tn(a, b):
    return lax.dot_general(a, b, (((0,), (0,)), ((), ())), preferred_element_type=F32)


def _prenorm_kernel(x_ref, g_ref, h_ref):
    h_ref[...] = _rms(x_ref[...], g_ref[...]).astype(h_ref.dtype)


def _prenorm(x, g):
    m, d = x.shape
    tm = _tile(m, 256, 8)
    return pl.pallas_call(
        _prenorm_kernel,
        out_shape=jax.ShapeDtypeStruct((m, d), BF16),
        grid=(m // tm,),
        in_specs=[pl.BlockSpec((tm, d), lambda i: (i, 0)),
                  pl.BlockSpec((1, d), lambda i: (0, 0))],
        out_specs=pl.BlockSpec((tm, d), lambda i: (i, 0)),
        compiler_params=pltpu.CompilerParams(
            dimension_semantics=("arbitrary",),
            vmem_limit_bytes=_vmem_limit(tm * d * 6, 3 * tm * d * 4)),
        name="prenorm",
    )(x, g)


def _gateup_kernel(h_ref, wg_ref, wu_ref, o_ref):
    h = h_ref[...]
    g = _dot(h, wg_ref[...].astype(BF16))
    u = _dot(h, wu_ref[...].astype(BF16))
    o_ref[...] = (g * jax.nn.sigmoid(g) * u).astype(o_ref.dtype)


def _gateup(h, wg, wu):
    m, d = h.shape
    f = wg.shape[1]
    tm = _tile(m, 2048, 16)
    tn = _tile(f, 256, V7X_LANES)
    return pl.pallas_call(
        _gateup_kernel,
        out_shape=jax.ShapeDtypeStruct((m, f), BF16),
        grid=(m // tm, f // tn),
        in_specs=[pl.BlockSpec((tm, d), lambda i, j: (i, 0), pipeline_mode=pl.Buffered(1)),
                  pl.BlockSpec((d, tn), lambda i, j: (0, j)),
                  pl.BlockSpec((d, tn), lambda i, j: (0, j))],
        out_specs=pl.BlockSpec((tm, tn), lambda i, j: (i, j)),
        compiler_params=pltpu.CompilerParams(
            dimension_semantics=("arbitrary", "arbitrary"),
            vmem_limit_bytes=_vmem_limit(2 * d * tn * 4 + tm * tn * 2,
                                         tm * d * 2 + 2 * d * tn * 2 + 4 * tm * tn * 4)),
        name="swiglu_up",
    )(h, wg, wu)


def _accumulate(a_ref, w_ref, o_ref, nk, last):
    k = pl.program_id(1)
    tk = a_ref.shape[1]

    @pl.when(k == 0)
    def _():
        o_ref[...] = _dot(a_ref[...], w_ref[...])

    @pl.when((k > 0) & (k < nk - (last < tk)))
    def _():
        o_ref[...] += _dot(a_ref[...], w_ref[...])

    if last < tk:
        @pl.when(k == nk - 1)
        def _():
            o_ref[...] += _dot(a_ref[:, :last], w_ref[:last, :])


def _column_slab(a_ref, w_ref, o_ref):
    tn = w_ref.shape[1]
    o_ref[:, pl.ds(pl.multiple_of(pl.program_id(1) * tn, tn), tn)] = _dot(a_ref[...], w_ref[...])


def _rows_matmul(a_ref, w_ref, o_ref, nk, last):
    if a_ref.shape[1] == w_ref.shape[0] and w_ref.shape[1] < o_ref.shape[1]:
        _column_slab(a_ref, w_ref, o_ref)
    else:
        _accumulate(a_ref, w_ref, o_ref, nk, last)


def _rows_specs(m, kdim, d):
    tm = _tile(m, 512, 16)
    if kdim <= d:
        tn = _tile(d, 512, V7X_LANES)
        return tm, d // tn, kdim, (pl.BlockSpec((tm, kdim), lambda i, j: (i, 0)),
                                   pl.BlockSpec((kdim, tn), lambda i, j: (0, j)))
    tk = 512
    nk = pl.cdiv(kdim, tk)
    last = kdim - (nk - 1) * tk
    assert last % V7X_LANES == 0
    return tm, nk, last, (pl.BlockSpec((tm, tk), lambda i, k: (i, k)),
                          pl.BlockSpec((tk, d), lambda i, k: (k, 0)))


def _down_kernel(a_ref, w_ref, x_ref, gpost_ref, gnext_ref, o_ref, h_ref, *, nk, last, scale, rows):
    _rows_matmul(a_ref, w_ref, o_ref, nk, last)

    @pl.when(pl.program_id(1) == nk - 1)
    def _():
        def body(s, carry):
            sl = pl.ds(pl.multiple_of(s * rows, rows), rows)
            xn = x_ref[sl, :] + scale * _rms(o_ref[sl, :], gpost_ref[...])
            o_ref[sl, :] = xn
            h_ref[sl, :] = _rms(xn, gnext_ref[...]).astype(h_ref.dtype)
            return carry
        lax.fori_loop(0, o_ref.shape[0] // rows, body, 0)


def _down(a, w, x, gpost, gnext, scale):
    m, kdim = a.shape
    d = w.shape[1]
    tm, nk, last, aw_specs = _rows_specs(m, kdim, d)
    rows = _tile(tm, 64, 8)
    aw_bytes = sum(math.prod(sp.block_shape) for sp in aw_specs) * 2
    return pl.pallas_call(
        functools.partial(_down_kernel, nk=nk, last=last, scale=scale, rows=rows),
        out_shape=(jax.ShapeDtypeStruct((m, d), F32), jax.ShapeDtypeStruct((m, d), BF16)),
        grid=(m // tm, nk),
        in_specs=[*aw_specs,
                  pl.BlockSpec((tm, d), lambda i, k: (i, 0)),
                  pl.BlockSpec((1, d), lambda i, k: (0, 0)),
                  pl.BlockSpec((1, d), lambda i, k: (0, 0))],
        out_specs=(pl.BlockSpec((tm, d), lambda i, k: (i, 0)),
                   pl.BlockSpec((tm, d), lambda i, k: (i, 0))),
        compiler_params=pltpu.CompilerParams(
            dimension_semantics=("arbitrary", "arbitrary"),
            vmem_limit_bytes=_vmem_limit(aw_bytes + tm * d * 10, 6 * rows * d * 4)),
        name="down_norm_residual",
    )(a, w, x, gpost, gnext)


def _ple_kernel(a_ref, w_ref, x_ref, p_ref, wp_ref, gpost_ref, o_ref, *, nk, last, rows):
    _rows_matmul(a_ref, w_ref, o_ref, nk, last)

    @pl.when(pl.program_id(1) == nk - 1)
    def _():
        def body(s, carry):
            sl = pl.ds(pl.multiple_of(s * rows, rows), rows)
            gate = jax.nn.sigmoid(o_ref[sl, :])
            proj = _dot(p_ref[sl, :], wp_ref[...])
            o_ref[sl, :] = x_ref[sl, :] + _rms(proj * gate, gpost_ref[...])
            return carry
        lax.fori_loop(0, o_ref.shape[0] // rows, body, 0)


def _ple(a, w, x, p, wp, gpost):
    m, kdim = a.shape
    d = w.shape[1]
    pd = p.shape[1]
    tm, nk, last, aw_specs = _rows_specs(m, kdim, d)
    rows = _tile(tm, 64, 8)
    aw_bytes = sum(math.prod(sp.block_shape) for sp in aw_specs) * 2
    return pl.pallas_call(
        functools.partial(_ple_kernel, nk=nk, last=last, rows=rows),
        out_shape=jax.ShapeDtypeStruct((m, d), F32),
        grid=(m // tm, nk),
        in_specs=[*aw_specs,
                  pl.BlockSpec((tm, d), lambda i, k: (i, 0)),
                  pl.BlockSpec((tm, pd), lambda i, k: (i, 0)),
                  pl.BlockSpec((pd, d), lambda i, k: (0, 0)),
                  pl.BlockSpec((1, d), lambda i, k: (0, 0))],
        out_specs=pl.BlockSpec((tm, d), lambda i, k: (i, 0)),
        compiler_params=pltpu.CompilerParams(
            dimension_semantics=("arbitrary", "arbitrary"),
            vmem_limit_bytes=_vmem_limit(aw_bytes + tm * d * 8 + tm * pd * 2 + pd * d * 2, 6 * rows * d * 4)),
        name="ple_gate_norm_residual",
    )(a, w, x, p, wp, gpost)


def _qk_kernel(h_ref, w_ref, cos_ref, sa_ref, sb_ref, o_ref, *, q_tiles, qscale):
    j = pl.program_id(1)
    w = w_ref[...].astype(BF16)
    sc = jnp.where(j < q_tiles, qscale, 1.0).astype(F32)
    half = h_ref.shape[0] // 2
    for r in range(2):
        rows = pl.ds(r * half, half)
        z = _dot(h_ref[rows, :], w)
        c, sa, sb = cos_ref[rows, :], sa_ref[rows, :], sb_ref[rows, :]
        outs = []
        for g in range(z.shape[1] // V7X_LANES):
            zg = z[:, g * V7X_LANES:(g + 1) * V7X_LANES]
            rg = zg * c + pltpu.roll(zg, V7X_LANES - 16, 1) * sa + pltpu.roll(zg, 16, 1) * sb
            outs.append(rg * sc)
        o_ref[rows, :] = jnp.concatenate(outs, axis=1).astype(o_ref.dtype)


def _qk_proj(h, w, cos_t, sa_t, sb_t, *, seq, qk_cols, qscale):
    m, d = h.shape
    n = 2 * qk_cols
    tm = _tile(seq, 1024, 16)
    tn = _tile(qk_cols, 512, V7X_LANES)
    tpb = seq // tm
    return pl.pallas_call(
        functools.partial(_qk_kernel, q_tiles=qk_cols // tn, qscale=qscale),
        out_shape=jax.ShapeDtypeStruct((m, n), BF16),
        grid=(m // tm, n // tn),
        in_specs=[pl.BlockSpec((tm, d), lambda i, j: (i, 0)),
                  pl.BlockSpec((d, tn), lambda i, j: (0, j)),
                  pl.BlockSpec((tm, V7X_LANES), lambda i, j: (i % tpb, 0)),
                  pl.BlockSpec((tm, V7X_LANES), lambda i, j: (i % tpb, 0)),
                  pl.BlockSpec((tm, V7X_LANES), lambda i, j: (i % tpb, 0))],
        out_specs=pl.BlockSpec((tm, tn), lambda i, j: (i, j)),
        compiler_params=pltpu.CompilerParams(
            dimension_semantics=("arbitrary", "arbitrary"),
            vmem_limit_bytes=_vmem_limit(tm * d * 2 + d * tn * 4 + tm * tn * 2 + 3 * tm * V7X_LANES * 4,
                                         d * tn * 2 + 5 * tm * tn * 4)),
        name="qk_proj_rotary",
    )(h, w, cos_t, sa_t, sb_t)


def _vt_proj_kernel(wt_ref, h_ref, o_ref):
    o_ref[...] = _dot_nt(wt_ref[...], h_ref[...]).astype(o_ref.dtype)


def _vt_proj(h, wt):
    m, d = h.shape
    n = wt.shape[0]
    tm = _tile(m, 1024, V7X_LANES)
    tn = _tile(n, 512, 16)
    return pl.pallas_call(
        _vt_proj_kernel,
        out_shape=jax.ShapeDtypeStruct((n, m), BF16),
        grid=(m // tm, n // tn),
        in_specs=[pl.BlockSpec((tn, d), lambda i, j: (j, 0)),
                  pl.BlockSpec((tm, d), lambda i, j: (i, 0))],
        out_specs=pl.BlockSpec((tn, tm), lambda i, j: (j, i)),
        compiler_params=pltpu.CompilerParams(
            dimension_semantics=("arbitrary", "arbitrary"),
            vmem_limit_bytes=_vmem_limit(tm * d * 2 + d * tn * 2 + tm * tn * 2, 2 * tm * tn * 4)),
        name="v_proj_transposed",
    )(wt, h)


def _plain_proj_kernel(h_ref, w_ref, o_ref):
    o_ref[...] = _dot(h_ref[...], w_ref[...].astype(BF16)).astype(o_ref.dtype)


def _plain_proj(h, w, col0, n, out_dtype):
    m, d = h.shape
    tm = _tile(m, 1024, 16)
    tn = _tile(math.gcd(n, col0) if col0 else n, 512, V7X_LANES)
    j0 = col0 // tn
    return pl.pallas_call(
        _plain_proj_kernel,
        out_shape=jax.ShapeDtypeStruct((m, n), out_dtype),
        grid=(m // tm, n // tn),
        in_specs=[pl.BlockSpec((tm, d), lambda i, j: (i, 0)),
                  pl.BlockSpec((d, tn), lambda i, j: (0, j0 + j))],
        out_specs=pl.BlockSpec((tm, tn), lambda i, j: (i, j)),
        compiler_params=pltpu.CompilerParams(
            dimension_semantics=("arbitrary", "arbitrary"),
            vmem_limit_bytes=_vmem_limit(tm * d * 2 + d * tn * 4 + tm * tn * 4, d * tn * 2 + 2 * tm * tn * 4)),
        name="rwkv_proj",
    )(h, w)


def _attn_kernel(q_ref, k_ref, vt_ref, lq1_ref, lk1_ref, lq2_ref, lk2_ref, sg_ref, o_ref,
                 m_ref, l_ref, acc_ref, *, tq, hd, lambda_init):
    qi = pl.program_id(2)
    q = q_ref[...]
    m_ref[...] = jnp.full(m_ref.shape, -jnp.inf, F32)
    l_ref[...] = jnp.zeros(l_ref.shape, F32)
    acc_ref[...] = jnp.zeros(acc_ref.shape, F32)

    def tile(start, diagonal):
        kt = k_ref[pl.ds(start, tq), :]
        vt = vt_ref[:, pl.ds(start, tq)]
        ss = [_dot_nt(kt[:, c * hd:(c + 1) * hd], q[:, c * hd:(c + 1) * hd]) for c in range(2)]
        if diagonal:
            sh = ATTN_CHUNK.bit_length() - 1
            ck = lax.shift_right_logical(lax.broadcasted_iota(jnp.int32, ss[0].shape, 0), sh)
            rq = lax.shift_right_logical(lax.broadcasted_iota(jnp.int32, ss[0].shape, 1), sh)
            ss = [jnp.where(ck <= rq, s, -jnp.inf) for s in ss]
        m_olds = [m_ref[c] for c in range(2)]
        m_news = [jnp.maximum(m_old, jnp.max(s, axis=0, keepdims=True)) for m_old, s in zip(m_olds, ss)]
        ps = [jnp.exp2(s - m_new) for s, m_new in zip(ss, m_news)]
        alphas = [jnp.exp2(m_old - m_new) for m_old, m_new in zip(m_olds, m_news)]
        pvs = [_dot(vt, p.astype(BF16)) for p in ps]
        for c in range(2):
            l_ref[c] = alphas[c] * l_ref[c] + jnp.sum(ps[c], axis=0, keepdims=True)
            acc_ref[c] = alphas[c] * acc_ref[c] + pvs[c]
            m_ref[c] = m_news[c]

    def body(j, carry):
        tile(pl.multiple_of(j * tq, tq), False)
        return carry

    lax.fori_loop(0, qi, body, 0)
    tile(pl.multiple_of(qi * tq, tq), True)

    lam = (jnp.exp(jnp.sum(lq1_ref[...] * lk1_ref[...], axis=-1, keepdims=True))
           - jnp.exp(jnp.sum(lq2_ref[...] * lk2_ref[...], axis=-1, keepdims=True)) + lambda_init)
    o = acc_ref[0] / l_ref[0] - lam * (acc_ref[1] / l_ref[1])
    o = o * lax.rsqrt(jnp.mean(o * o, axis=0, keepdims=True) + RMS_EPS) * sg_ref[...] * (1.0 - lambda_init)
    o_ref[...] = o.T.astype(o_ref.dtype)


def _diff_attention(zqk, vt, lq1, lk1, lq2, lk2, subln_g, *, batch, seq, heads, hd, out_cols, lambda_init):
    m = zqk.shape[0]
    vd = 2 * hd
    tq = _tile(seq, 512, V7X_LANES)
    nq = seq // tq
    vec = pl.BlockSpec((1, hd), lambda b, h, i: (0, 0))
    return pl.pallas_call(
        functools.partial(_attn_kernel, tq=tq, hd=hd, lambda_init=lambda_init),
        out_shape=jax.ShapeDtypeStruct((m, out_cols), BF16),
        grid=(batch, heads, nq),
        in_specs=[pl.BlockSpec((tq, vd), lambda b, h, i: (b * nq + i, h)),
                  pl.BlockSpec((seq, vd), lambda b, h, i: (b, heads + h)),
                  pl.BlockSpec((vd, seq), lambda b, h, i: (h, b)),
                  vec, vec, vec, vec,
                  pl.BlockSpec((vd, 1), lambda b, h, i: (0, 0))],
        out_specs=pl.BlockSpec((tq, vd), lambda b, h, i: (b * nq + i, h)),
        scratch_shapes=[pltpu.VMEM((2, 1, tq), F32), pltpu.VMEM((2, 1, tq), F32),
                        pltpu.VMEM((2, vd, tq), F32)],
        compiler_params=pltpu.CompilerParams(
            dimension_semantics=("arbitrary", "arbitrary", "arbitrary"),
            vmem_limit_bytes=_vmem_limit(2 * tq * vd * 2 + 2 * seq * vd * 2,
                                         2 * tq * vd * 4 + 10 * tq * tq * 4)),
        name="diff_attention",
    )(zqk, zqk, vt, lq1, lk1, lq2, lk2, subln_g)


def _split3_bf16(x):
    h1 = x.astype(BF16)
    r1 = x - h1.astype(F32)
    h2 = r1.astype(BF16)
    h3 = (r1 - h2.astype(F32)).astype(BF16)
    return h1, h2, h3


def _rwkv_kernel(z_ref, zl_ref, mu_ref, mul_ref, w0_ref, w2_ref, a0_ref, a2_ref, g2_ref, kk_ref, ka_ref, rk_ref,
                 lnw_ref, lnb_ref, alias_ref, o_ref,
                 prev_ref, prevl_ref, st_ref, lw_ref, cum_ref, as_ref, gt_ref, *, rw, hn, pairs_per_iter):
    del alias_ref
    c = pl.program_id(1)
    lt = z_ref.shape[0]
    ln = V7X_LANES

    @pl.when(c == 0)
    def _():
        prev_ref[...] = jnp.zeros(prev_ref.shape, F32)
        prevl_ref[...] = jnp.zeros(prevl_ref.shape, F32)
        st_ref[...] = jnp.zeros(st_ref.shape, F32)

    def shifted(zt, prev_row, mu):
        row = lax.broadcasted_iota(jnp.int32, zt.shape, 0)
        zp = jnp.where(row == 0, prev_row, pltpu.roll(zt, 1, 0))
        return zt + (zp - zt) * mu

    zl = zl_ref[...]
    zlm = shifted(zl, prevl_ref[0:1, :], mul_ref[...])
    prevl_ref[0:1, :] = zl[lt - 1:lt, :]
    dl = w2_ref.shape[0]
    al = a2_ref.shape[0]
    wpre = w0_ref[...] + _dot(jnp.tanh(zlm[:, :dl]).astype(BF16), w2_ref[...])
    softplus = jnp.maximum(-wpre, 0.0) + jnp.log(1.0 + jnp.exp(-jnp.abs(wpre)))
    lw = -jnp.exp(-softplus - 0.5)
    lw_ref[...] = lw
    tri = (lax.broadcasted_iota(jnp.int32, (lt, lt), 0)
           >= lax.broadcasted_iota(jnp.int32, (lt, lt), 1)).astype(BF16)
    h1, h2, h3 = _split3_bf16(lw)
    cum_ref[...] = _dot(tri, h1) + _dot(tri, h2) + _dot(tri, h3)
    as_ref[...] = jax.nn.sigmoid(a0_ref[...] + _dot(zlm[:, dl:dl + al].astype(BF16), a2_ref[...]))
    gt_ref[...] = _dot(jax.nn.sigmoid(zlm[:, dl + al:]).astype(BF16), g2_ref[...])

    lane = lax.broadcasted_iota(jnp.int32, (lt, ln), 1)
    first = lane < hn
    r2 = lax.broadcasted_iota(jnp.int32, (2 * hn, 2 * hn), 0)
    c2 = lax.broadcasted_iota(jnp.int32, (2 * hn, 2 * hn), 1)
    hsh = hn.bit_length() - 1
    same = lax.shift_right_logical(r2, hsh) == lax.shift_right_logical(c2, hsh)
    strict = same & (r2 > c2)
    incl = same & (r2 >= c2)
    eye = (r2 == c2).astype(F32)
    zeros = jnp.zeros((2 * hn, ln), F32)

    def headsum(x):
        s0 = jnp.sum(jnp.where(first, x, 0.0), axis=-1, keepdims=True)
        s1 = jnp.sum(jnp.where(first, 0.0, x), axis=-1, keepdims=True)
        return jnp.where(first, s0, s1)

    def bd(x):
        return jnp.concatenate([jnp.where(first, x, 0.0), jnp.where(first, 0.0, x)], axis=0)

    def fold(x):
        return x[:hn] + x[hn:]

    def mixed(off):
        sl = pl.ds(off, ln)
        return shifted(z_ref[:, sl], prev_ref[0:1, sl], mu_ref[:, sl])

    def pairs(offs):
        n2 = 2 * hn
        sls = [pl.ds(off, ln) for off in offs]
        rs = [mixed(off) for off in offs]
        ks = [mixed(rw + off) for off in offs]
        vs = [mixed(2 * rw + off) for off in offs]
        asgs = [as_ref[:, sl] for sl in sls]
        kks = [k * kk_ref[:, sl] for k, sl in zip(ks, sls)]
        kks = [kk / jnp.maximum(jnp.sqrt(headsum(kk * kk)), 1e-12) for kk in kks]
        kns = [k * (1.0 + (asg - 1.0) * ka_ref[:, sl]) for k, asg, sl in zip(ks, asgs, sls)]
        bns = [kk * asg for kk, asg in zip(kks, asgs)]

        lws = [lw_ref[:, sl] for sl in sls]
        cums = [cum_ref[:, sl] for sl in sls]
        tots = [cum[lt - 1:lt, :] for cum in cums]
        e_invs = [jnp.exp(-cum) for cum in cums]
        ats = [bd(-kk * jnp.exp(cum - lw)) for kk, cum, lw in zip(kks, cums, lws)]
        rts = [bd(r * jnp.exp(cum)) for r, cum in zip(rs, cums)]
        vbs = [bd(v) for v in vs]

        fulls = [_dot_nt(jnp.concatenate([at, rt], axis=0).astype(BF16),
                         jnp.concatenate([bd(bn * e), bd(kn * e)], axis=0).astype(BF16))
                 for at, rt, bn, kn, e in zip(ats, rts, bns, kns, e_invs)]
        a_abs = [jnp.where(strict, f[:n2, :n2], 0.0) for f in fulls]
        a_aks = [jnp.where(strict, f[:n2, n2:], 0.0) for f in fulls]
        a_rs = [jnp.concatenate([jnp.where(incl, f[n2:, :n2], 0.0), jnp.where(incl, f[n2:, n2:], 0.0)],
                                axis=1).astype(BF16) for f in fulls]

        akvs = [_dot(a_ak.astype(BF16), vb.astype(BF16)) for a_ak, vb in zip(a_aks, vbs)]
        zqs = [jnp.concatenate([at, akv], axis=1) for at, akv in zip(ats, akvs)]
        apows = a_abs
        steps = max(1, (lt - 1).bit_length())
        for it in range(steps):
            if it + 1 < steps:
                aws = [_dot(ap.astype(BF16), jnp.concatenate([zq, ap], axis=1).astype(BF16))
                       for ap, zq in zip(apows, zqs)]
                zqs = [zq + aw[:, :2 * ln] for zq, aw in zip(zqs, aws)]
                apows = [aw[:, 2 * ln:] for aw in aws]
            else:
                zqs = [zq + _dot(ap.astype(BF16), zq.astype(BF16)) for ap, zq in zip(apows, zqs)]

        rhs2s = [jnp.concatenate([zq, jnp.concatenate([zeros, vb], axis=1)], axis=0).astype(BF16)
                 for zq, vb in zip(zqs, vbs)]
        rys = [_dot(a_r, rhs2) for a_r, rhs2 in zip(a_rs, rhs2s)]
        e_ends = [jnp.exp(tot - cum) for tot, cum in zip(tots, cums)]
        ghs = [_dot_tn(jnp.concatenate([bd(bn * e), bd(kn * e)], axis=0).astype(BF16), rhs2)
               for bn, kn, e, rhs2 in zip(bns, kns, e_ends, rhs2s)]
        lhs4s = [jnp.concatenate([fold(rt + ry[:, :ln]), fold(gh[:, :ln] + eye * jnp.exp(tot))],
                                 axis=0).astype(BF16)
                 for rt, ry, gh, tot in zip(rts, rys, ghs, tots)]
        outs = [_dot(lhs4, bd(st_ref[:, sl]).astype(BF16)) for lhs4, sl in zip(lhs4s, sls)]
        for sl, out, ry, gh, r, kn, v in zip(sls, outs, rys, ghs, rs, kns, vs):
            y = out[:hn] + fold(ry[:, ln:])
            st_ref[:, sl] = out[hn:] + fold(gh[:, ln:])
            mean = headsum(y) * (1.0 / hn)
            yc = y - mean
            var = headsum(yc * yc) * (1.0 / hn)
            yn = yc * lax.rsqrt(var + RW_LN_EPS) * lnw_ref[:, sl] + lnb_ref[:, sl]
            bonus = headsum(r * kn * rk_ref[:, sl]) * v
            o_ref[:, sl] = ((yn + bonus) * gt_ref[:, sl]).astype(o_ref.dtype)

    npairs = rw // ln

    def body(i, carry):
        pairs([pl.multiple_of((i * pairs_per_iter + u) * ln, ln) for u in range(pairs_per_iter)])
        return carry

    lax.fori_loop(0, npairs // pairs_per_iter, body, 0)
    prev_ref[0:1, :] = z_ref[lt - 1:lt, :]


def _rwkv(zr, zl, omix, mu, mul, w0, w2, a0, a2, g2, k_k, k_a, r_k, ln_w, ln_b, *, batch, seq, rw, hn):
    m, zc = zr.shape
    lc = zl.shape[1]
    assert hn * 2 == V7X_LANES and RW_CHUNK == hn and seq % RW_CHUNK == 0 and zc == 3 * rw
    lt = RW_CHUNK
    nc = seq // lt
    npairs = rw // V7X_LANES
    ppi = _tile(npairs, 8, 1)
    ocol = omix.shape[1] // rw - 1
    assert (ocol + 1) * rw == omix.shape[1]
    row = lambda n: pl.BlockSpec((1, n), lambda b, c: (0, 0))
    full = lambda a: pl.BlockSpec(a.shape, lambda b, c: (0, 0))
    return pl.pallas_call(
        functools.partial(_rwkv_kernel, rw=rw, hn=hn, pairs_per_iter=ppi),
        out_shape=jax.ShapeDtypeStruct(omix.shape, omix.dtype),
        grid=(batch, nc),
        in_specs=[pl.BlockSpec((lt, zc), lambda b, c: (b * nc + c, 0)),
                  pl.BlockSpec((lt, lc), lambda b, c: (b * nc + c, 0)),
                  row(zc), row(lc), row(rw), full(w2), row(rw), full(a2), full(g2),
                  row(rw), row(rw), row(rw), row(rw), row(rw),
                  pl.BlockSpec(memory_space=pl.ANY)],
        out_specs=pl.BlockSpec((lt, rw), lambda b, c: (b * nc + c, ocol)),
        scratch_shapes=[pltpu.VMEM((8, zc), F32), pltpu.VMEM((8, lc), F32), pltpu.VMEM((hn, rw), F32),
                        pltpu.VMEM((lt, rw), F32), pltpu.VMEM((lt, rw), F32),
                        pltpu.VMEM((lt, rw), F32), pltpu.VMEM((lt, rw), F32)],
        input_output_aliases={14: 0},
        compiler_params=pltpu.CompilerParams(
            dimension_semantics=("arbitrary", "arbitrary"),
            vmem_limit_bytes=_vmem_limit(lt * (zc + lc) * 4 + (w2.size + a2.size + g2.size) * 2 + lt * rw * 2,
                                         8 * (zc + lc) * 4 + 5 * lt * rw * 4 + (8 << 20))),
        name="rwkv7_chunked",
    )(zr, zl, mu, mul, w0, w2, a0, a2, g2, k_k, k_a, r_k, ln_w, ln_b, omix)


def _pad_to(a, axis, n):
    pad = n - a.shape[axis]
    if pad == 0:
        return a
    widths = [(0, 0)] * a.ndim
    widths[axis] = (0, pad)
    return jnp.pad(a, widths)


def _ffn(h, x, w_gate, w_up, w_down, gpost, gnext):
    act = _gateup(h, w_gate, w_up)
    return _down(act, w_down.astype(BF16), x, gpost, gnext, 0.5)


def kernel(x, p, ffn1_pre_g, ffn1_w_gate, ffn1_w_up, ffn1_w_down, ffn1_post_g, mix_pre_g, w_in, diff_lambda_q1, diff_lambda_k1, diff_lambda_q2, diff_lambda_k2, diff_subln_g, rwkv_mu, rwkv_w0, rwkv_w2, rwkv_a0, rwkv_a2, rwkv_g2, rwkv_k_k, rwkv_k_a, rwkv_r_k, rwkv_ln_w, rwkv_ln_b, w_out, mix_post_g, ffn2_pre_g, ffn2_w_gate, ffn2_w_up, ffn2_w_down, ffn2_post_g, ple_pre_g, ple_w_gate, ple_w_proj, ple_post_g):
    batch, seq, d = x.shape
    depth = p.shape[0]
    m = batch * seq
    hd = diff_subln_g.shape[-1] // 2
    rw = rwkv_w2.shape[-1]
    da = w_out.shape[1] - rw
    heads = da // (2 * hd)
    qk = heads * 2 * hd
    rheads, hn = rwkv_r_k.shape[1], rwkv_r_k.shape[2]
    dl, al, gl = rwkv_w2.shape[1], rwkv_a2.shape[1], rwkv_g2.shape[1]
    dlp, alp, glp = (_round_up(n, V7X_LANES) for n in (dl, al, gl))
    rot = hd // 4
    assert rheads * hn == rw and da == rw and w_in.shape[2] == 2 * qk + da + 3 * rw + dl + al + gl

    inv = ROPE_THETA ** (-jnp.arange(0, rot, 2, dtype=F32) / rot)
    ang = jnp.arange(seq).astype(F32)[:, None] * inv[None, :]
    half = rot // 2
    ones = jnp.ones((seq, hd - rot), F32)
    zeros_h = jnp.zeros((seq, half), F32)
    zeros_r = jnp.zeros((seq, hd - rot), F32)
    cos_t = jnp.concatenate([jnp.cos(ang), jnp.cos(ang), ones], axis=1)
    sa_t = jnp.concatenate([-jnp.sin(ang), zeros_h, zeros_r], axis=1)
    sb_t = jnp.concatenate([zeros_h, jnp.sin(ang), zeros_r], axis=1)
    assert half == 16 and hd == V7X_LANES

    row = lambda a: a.reshape(1, -1).astype(F32)
    xf = x.reshape(m, d)
    for i in range(depth):
        lambda_init = 0.8 - 0.6 * math.exp(-0.3 * i)

        h = _prenorm(xf, row(ffn1_pre_g[i]))
        xf, h = _ffn(h, xf, ffn1_w_gate[i], ffn1_w_up[i], ffn1_w_down[i],
                     row(ffn1_post_g[i]), row(mix_pre_g[i]))

        wi = w_in[i]
        c0 = 2 * qk + da + 3 * rw
        w_l = jnp.concatenate([_pad_to(wi[:, c0:c0 + dl], 1, dlp),
                               _pad_to(wi[:, c0 + dl:c0 + dl + al], 1, alp),
                               _pad_to(wi[:, c0 + dl + al:], 1, glp)], axis=1).astype(BF16)
        mu = rwkv_mu[i]
        mu_l = jnp.concatenate([_pad_to(mu[3 * rw:3 * rw + dl], 0, dlp),
                                _pad_to(mu[3 * rw + dl:3 * rw + dl + al], 0, alp),
                                _pad_to(mu[3 * rw + dl + al:], 0, glp)])
        zqk = _qk_proj(h, wi[:, :2 * qk].astype(BF16), cos_t, sa_t, sb_t, seq=seq, qk_cols=qk,
                       qscale=hd ** -0.5 * math.log2(math.e))
        vt = _vt_proj(h, wi[:, 2 * qk:2 * qk + da].astype(BF16).T)
        zr = _plain_proj(h, wi[:, 2 * qk + da:c0].astype(BF16), 0, 3 * rw, F32)
        zl = _plain_proj(h, w_l, 0, w_l.shape[1], F32)

        omix = _diff_attention(zqk, vt, row(diff_lambda_q1[i]), row(diff_lambda_k1[i]),
                               row(diff_lambda_q2[i]), row(diff_lambda_k2[i]),
                               diff_subln_g[i].reshape(-1, 1).astype(F32),
                               batch=batch, seq=seq, heads=heads, hd=hd, out_cols=da + rw,
                               lambda_init=lambda_init)
        omix = _rwkv(zr, zl, omix, row(mu[:3 * rw]), row(mu_l), row(rwkv_w0[i]),
                     _pad_to(rwkv_w2[i].astype(BF16), 0, dlp), row(rwkv_a0[i]),
                     _pad_to(rwkv_a2[i].astype(BF16), 0, alp), _pad_to(rwkv_g2[i].astype(BF16), 0, glp),
                     row(rwkv_k_k[i]), row(rwkv_k_a[i]), row(rwkv_r_k[i]),
                     row(rwkv_ln_w[i]), row(rwkv_ln_b[i]), batch=batch, seq=seq, rw=rw, hn=hn)

        xf, h = _down(omix, w_out[i].astype(BF16), xf, row(mix_post_g[i]), row(ffn2_pre_g[i]), 1.0)
        xf, h = _ffn(h, xf, ffn2_w_gate[i], ffn2_w_up[i], ffn2_w_down[i],
                     row(ffn2_post_g[i]), row(ple_pre_g[i]))
        xf = _ple(h, ple_w_gate[i].astype(BF16), xf, p[i].reshape(m, -1).astype(BF16),
                  ple_w_proj[i].astype(BF16), row(ple_post_g[i]))
    return xf.reshape(batch, seq, d)
```

```python
import functools
import math

import jax
import jax.numpy as jnp
from jax import lax
from jax.experimental import pallas as pl
from jax.experimental.pallas import tpu as pltpu

F32 = jnp.float32
BF16 = jnp.bfloat16

RMS_EPS = 1e-6
RW_LN_EPS = 64e-5
ATTN_CHUNK = 64
ROPE_THETA = 500000.0
RW_CHUNK = 64

V7X_LANES = 128
V7X_VMEM_CAP_BYTES = 58 * 1024 * 1024


def _tile(n, pref, align):
    t = (min(n, pref) // align) * align
    while t >= align:
        if n % t == 0:
            return t
        t -= align
    return n


def _round_up(n, m):
    return (n + m - 1) // m * m


def _vmem_limit(pipelined_bytes, resident_bytes):
    need = 2 * pipelined_bytes + resident_bytes + (4 << 20)
    return int(min(V7X_VMEM_CAP_BYTES, max(need, 16 << 20)))


def _rms(x, g, eps=RMS_EPS):
    return x * lax.rsqrt(jnp.mean(x * x, axis=-1, keepdims=True) + eps) * g


def _dot(a, b):
    return jnp.dot(a, b, preferred_element_type=F32)


def _dot_nt(a, b):
    return lax.dot_general(a, b, (((1,), (1,)), ((), ())), preferred_element_type=F32)


def _dot_tn(a, b):
    return lax.dot_general(a, b, (((0,), (0,)), ((), ())), preferred_element_type=F32)


def _prenorm_kernel(x_ref, g_ref, h_ref):
    h_ref[...] = _rms(x_ref[...], g_ref[...]).astype(h_ref.dtype)


def _prenorm(x, g):
    m, d = x.shape
    tm = _tile(m, 256, 8)
    return pl.pallas_call(
        _prenorm_kernel,
        out_shape=jax.ShapeDtypeStruct((m, d), BF16),
        grid=(m // tm,),
        in_specs=[pl.BlockSpec((tm, d), lambda i: (i, 0)),
                  pl.BlockSpec((1, d), lambda i: (0, 0))],
        out_specs=pl.BlockSpec((tm, d), lambda i: (i, 0)),
        compiler_params=pltpu.CompilerParams(
            dimension_semantics=("arbitrary",),
            vmem_limit_bytes=_vmem_limit(tm * d * 6, 3 * tm * d * 4)),
        name="prenorm",
    )(x, g)


def _gateup_kernel(h_ref, wg_ref, wu_ref, o_ref):
    h = h_ref[...]
    g = _dot(h, wg_ref[...].astype(BF16))
    u = _dot(h, wu_ref[...].astype(BF16))
    o_ref[...] = (g * jax.nn.sigmoid(g) * u).astype(o_ref.dtype)


def _gateup(h, wg, wu):
    m, d = h.shape
    f = wg.shape[1]
    tm = _tile(m, 2048, 16)
    tn = _tile(f, 256, V7X_LANES)
    return pl.pallas_call(
        _gateup_kernel,
        out_shape=jax.ShapeDtypeStruct((m, f), BF16),
        grid=(m // tm, f // tn),
        in_specs=[pl.BlockSpec((tm, d), lambda i, j: (i, 0), pipeline_mode=pl.Buffered(1)),
                  pl.BlockSpec((d, tn), lambda i, j: (0, j)),
                  pl.BlockSpec((d, tn), lambda i, j: (0, j))],
        out_specs=pl.BlockSpec((tm, tn), lambda i, j: (i, j)),
        compiler_params=pltpu.CompilerParams(
            dimension_semantics=("arbitrary", "arbitrary"),
            vmem_limit_bytes=_vmem_limit(2 * d * tn * 4 + tm * tn * 2,
                                         tm * d * 2 + 2 * d * tn * 2 + 4 * tm * tn * 4)),
        name="swiglu_up",
    )(h, wg, wu)


def _deferred_rows(a_ref, w_ref, acc_ref, *, ntiles, nk, last, nchunks, epilogue):
    i = pl.program_id(0)
    k = pl.program_id(1)
    tk = a_ref.shape[1]
    cur = acc_ref.at[i % 2]
    prev = acc_ref.at[(i + 1) % 2]
    rc = acc_ref.shape[1] // nchunks
    full_end = nk - (last < tk)
    assert nchunks <= full_end
    mm = i < ntiles
    epi = (i >= 1) & (k < nchunks)

    def run_epilogue():
        epilogue(prev[pl.ds(pl.multiple_of(k * rc, rc), rc), :])

    @pl.when(mm & (k == 0) & (i == 0))
    def _():
        cur[...] = _dot(a_ref[...], w_ref[...])

    @pl.when(mm & (k == 0) & (i >= 1))
    def _():
        run_epilogue()
        cur[...] = _dot(a_ref[...], w_ref[...])

    @pl.when(mm & (k > 0) & (k < full_end) & epi)
    def _():
        run_epilogue()
        cur[...] += _dot(a_ref[...], w_ref[...])

    @pl.when(mm & (k > 0) & (k < full_end) & jnp.logical_not(epi))
    def _():
        cur[...] += _dot(a_ref[...], w_ref[...])

    if last < tk:
        @pl.when(mm & (k == nk - 1))
        def _():
            cur[...] += _dot(a_ref[:, :last], w_ref[:last, :])

    @pl.when(jnp.logical_not(mm) & (k < nchunks))
    def _():
        run_epilogue()


def _deferred_specs(m, kdim, d):
    tm = _tile(m, 1024, 16)
    tk = 512
    ntiles = m // tm
    nk = pl.cdiv(kdim, tk)
    last = kdim - (nk - 1) * tk
    assert last % V7X_LANES == 0 and nk >= 2
    nchunks = 1 << ((nk - (last < tk)).bit_length() - 1)
    nchunks = min(nchunks, tm // 8)
    rc = tm // nchunks
    a_spec = pl.BlockSpec((tm, tk), lambda i, k: (jnp.minimum(i, ntiles - 1), jnp.where(i < ntiles, k, nk - 1)))
    w_spec = pl.BlockSpec((tk, d), lambda i, k: (jnp.where(i < ntiles, k, nk - 1), 0))
    chunk = lambda i, k: (jnp.where(i >= 1, (i - 1) * nchunks + jnp.minimum(k, nchunks - 1), 0), 0)
    return tm, tk, ntiles, nk, last, nchunks, rc, a_spec, w_spec, chunk


def _down_kernel(a_ref, w_ref, x_ref, gpost_ref, gnext_ref, o_ref, h_ref, acc_ref, *, scale, **tiling):
    def epilogue(f):
        xn = x_ref[...] + scale * _rms(f, gpost_ref[...])
        o_ref[...] = xn
        h_ref[...] = _rms(xn, gnext_ref[...]).astype(h_ref.dtype)

    _deferred_rows(a_ref, w_ref, acc_ref, epilogue=epilogue, **tiling)


def _down(a, w, x, gpost, gnext, scale):
    m, kdim = a.shape
    d = w.shape[1]
    tm, tk, ntiles, nk, last, nchunks, rc, a_spec, w_spec, chunk = _deferred_specs(m, kdim, d)
    return pl.pallas_call(
        functools.partial(_down_kernel, scale=scale, ntiles=ntiles, nk=nk, last=last, nchunks=nchunks),
        out_shape=(jax.ShapeDtypeStruct((m, d), F32), jax.ShapeDtypeStruct((m, d), BF16)),
        grid=(ntiles + 1, nk),
        in_specs=[a_spec, w_spec,
                  pl.BlockSpec((rc, d), chunk),
                  pl.BlockSpec((1, d), lambda i, k: (0, 0)),
                  pl.BlockSpec((1, d), lambda i, k: (0, 0))],
        out_specs=(pl.BlockSpec((rc, d), chunk), pl.BlockSpec((rc, d), chunk)),
        scratch_shapes=[pltpu.VMEM((2, tm, d), F32)],
        compiler_params=pltpu.CompilerParams(
            dimension_semantics=("arbitrary", "arbitrary"),
            vmem_limit_bytes=_vmem_limit(tm * tk * 2 + tk * d * 2 + rc * d * 10, 2 * tm * d * 4 + 6 * rc * d * 4)),
        name="down_norm_residual",
    )(a, w, x, gpost, gnext)


def _ple_kernel(a_ref, w_ref, x_ref, p_ref, wp_ref, gpost_ref, o_ref, acc_ref, **tiling):
    def epilogue(f):
        proj = _dot(p_ref[...], wp_ref[...])
        o_ref[...] = x_ref[...] + _rms(proj * jax.nn.sigmoid(f), gpost_ref[...])

    _deferred_rows(a_ref, w_ref, acc_ref, epilogue=epilogue, **tiling)


def _ple(a, w, x, p, wp, gpost):
    m, kdim = a.shape
    d = w.shape[1]
    pd = p.shape[1]
    tm, tk, ntiles, nk, last, nchunks, rc, a_spec, w_spec, chunk = _deferred_specs(m, kdim, d)
    return pl.pallas_call(
        functools.partial(_ple_kernel, ntiles=ntiles, nk=nk, last=last, nchunks=nchunks),
        out_shape=jax.ShapeDtypeStruct((m, d), F32),
        grid=(ntiles + 1, nk),
        in_specs=[a_spec, w_spec,
                  pl.BlockSpec((rc, d), chunk),
                  pl.BlockSpec((rc, pd), chunk),
                  pl.BlockSpec((pd, d), lambda i, k: (0, 0)),
                  pl.BlockSpec((1, d), lambda i, k: (0, 0))],
        out_specs=pl.BlockSpec((rc, d), chunk),
        scratch_shapes=[pltpu.VMEM((2, tm, d), F32)],
        compiler_params=pltpu.CompilerParams(
            dimension_semantics=("arbitrary", "arbitrary"),
            vmem_limit_bytes=_vmem_limit(tm * tk * 2 + tk * d * 2 + rc * d * 8 + rc * pd * 2 + pd * d * 2,
                                         2 * tm * d * 4 + 6 * rc * d * 4)),
        name="ple_gate_norm_residual",
    )(a, w, x, p, wp, gpost)


def _qk_kernel(h_ref, w_ref, cos_ref, sa_ref, sb_ref, o_ref, *, q_tiles, qscale):
    j = pl.program_id(1)
    w = w_ref[...].astype(BF16)
    sc = jnp.where(j < q_tiles, qscale, 1.0).astype(F32)
    half = h_ref.shape[0] // 2
    for r in range(2):
        rows = pl.ds(r * half, half)
        z = _dot(h_ref[rows, :], w)
        c, sa, sb = cos_ref[rows, :], sa_ref[rows, :], sb_ref[rows, :]
        outs = []
        for g in range(z.shape[1] // V7X_LANES):
            zg = z[:, g * V7X_LANES:(g + 1) * V7X_LANES]
            rg = zg * c + pltpu.roll(zg, V7X_LANES - 16, 1) * sa + pltpu.roll(zg, 16, 1) * sb
            outs.append(rg * sc)
        o_ref[rows, :] = jnp.concatenate(outs, axis=1).astype(o_ref.dtype)


def _qk_proj(h, w, cos_t, sa_t, sb_t, *, seq, qk_cols, qscale):
    m, d = h.shape
    n = 2 * qk_cols
    tm = _tile(seq, 1024, 16)
    tn = _tile(qk_cols, 512, V7X_LANES)
    tpb = seq // tm
    return pl.pallas_call(
        functools.partial(_qk_kernel, q_tiles=qk_cols // tn, qscale=qscale),
        out_shape=jax.ShapeDtypeStruct((m, n), BF16),
        grid=(m // tm, n // tn),
        in_specs=[pl.BlockSpec((tm, d), lambda i, j: (i, 0)),
                  pl.BlockSpec((d, tn), lambda i, j: (0, j)),
                  pl.BlockSpec((tm, V7X_LANES), lambda i, j: (i % tpb, 0)),
                  pl.BlockSpec((tm, V7X_LANES), lambda i, j: (i % tpb, 0)),
                  pl.BlockSpec((tm, V7X_LANES), lambda i, j: (i % tpb, 0))],
        out_specs=pl.BlockSpec((tm, tn), lambda i, j: (i, j)),
        compiler_params=pltpu.CompilerParams(
            dimension_semantics=("arbitrary", "arbitrary"),
            vmem_limit_bytes=_vmem_limit(tm * d * 2 + d * tn * 4 + tm * tn * 2 + 3 * tm * V7X_LANES * 4,
                                         d * tn * 2 + 5 * tm * tn * 4)),
        name="qk_proj_rotary",
    )(h, w, cos_t, sa_t, sb_t)


def _vt_proj_kernel(wt_ref, h_ref, o_ref):
    o_ref[...] = _dot_nt(wt_ref[...], h_ref[...]).astype(o_ref.dtype)


def _vt_proj(h, wt):
    m, d = h.shape
    n = wt.shape[0]
    tm = _tile(m, 1024, V7X_LANES)
    tn = _tile(n, 512, 16)
    return pl.pallas_call(
        _vt_proj_kernel,
        out_shape=jax.ShapeDtypeStruct((n, m), BF16),
        grid=(m // tm, n // tn),
        in_specs=[pl.BlockSpec((tn, d), lambda i, j: (j, 0)),
                  pl.BlockSpec((tm, d), lambda i, j: (i, 0))],
        out_specs=pl.BlockSpec((tn, tm), lambda i, j: (j, i)),
        compiler_params=pltpu.CompilerParams(
            dimension_semantics=("arbitrary", "arbitrary"),
            vmem_limit_bytes=_vmem_limit(tm * d * 2 + d * tn * 2 + tm * tn * 2, 2 * tm * tn * 4)),
        name="v_proj_transposed",
    )(wt, h)


def _plain_proj_kernel(h_ref, w_ref, o_ref):
    o_ref[...] = _dot(h_ref[...], w_ref[...].astype(BF16)).astype(o_ref.dtype)


def _plain_proj(h, w, col0, n, out_dtype):
    m, d = h.shape
    tm = _tile(m, 1024, 16)
    tn = _tile(math.gcd(n, col0) if col0 else n, 512, V7X_LANES)
    j0 = col0 // tn
    return pl.pallas_call(
        _plain_proj_kernel,
        out_shape=jax.ShapeDtypeStruct((m, n), out_dtype),
        grid=(m // tm, n // tn),
        in_specs=[pl.BlockSpec((tm, d), lambda i, j: (i, 0)),
                  pl.BlockSpec((d, tn), lambda i, j: (0, j0 + j))],
        out_specs=pl.BlockSpec((tm, tn), lambda i, j: (i, j)),
        compiler_params=pltpu.CompilerParams(
            dimension_semantics=("arbitrary", "arbitrary"),
            vmem_limit_bytes=_vmem_limit(tm * d * 2 + d * tn * 4 + tm * tn * 4, d * tn * 2 + 2 * tm * tn * 4)),
        name="rwkv_proj",
    )(h, w)


def _attn_kernel(q_ref, k_ref, vt_ref, lq1_ref, lk1_ref, lq2_ref, lk2_ref, sg_ref, o_ref,
                 m_ref, l_ref, acc_ref, *, tq, hd, lambda_init):
    qi = pl.program_id(2)
    q = q_ref[...]
    m_ref[...] = jnp.full(m_ref.shape, -jnp.inf, F32)
    l_ref[...] = jnp.zeros(l_ref.shape, F32)
    acc_ref[...] = jnp.zeros(acc_ref.shape, F32)

    def tile(start, diagonal):
        kt = k_ref[pl.ds(start, tq), :]
        vt = vt_ref[:, pl.ds(start, tq)]
        ss = [_dot_nt(kt[:, c * hd:(c + 1) * hd], q[:, c * hd:(c + 1) * hd]) for c in range(2)]
        if diagonal:
            sh = ATTN_CHUNK.bit_length() - 1
            ck = lax.shift_right_logical(lax.broadcasted_iota(jnp.int32, ss[0].shape, 0), sh)
            rq = lax.shift_right_logical(lax.broadcasted_iota(jnp.int32, ss[0].shape, 1), sh)
            ss = [jnp.where(ck <= rq, s, -jnp.inf) for s in ss]
        m_olds = [m_ref[c] for c in range(2)]
        m_news = [jnp.maximum(m_old, jnp.max(s, axis=0, keepdims=True)) for m_old, s in zip(m_olds, ss)]
        ps = [jnp.exp2(s - m_new) for s, m_new in zip(ss, m_news)]
        alphas = [jnp.exp2(m_old - m_new) for m_old, m_new in zip(m_olds, m_news)]
        pvs = [_dot(vt, p.astype(BF16)) for p in ps]
        for c in range(2):
            l_ref[c] = alphas[c] * l_ref[c] + jnp.sum(ps[c], axis=0, keepdims=True)
            acc_ref[c] = alphas[c] * acc_ref[c] + pvs[c]
            m_ref[c] = m_news[c]

    def body(j, carry):
        tile(pl.multiple_of(j * tq, tq), False)
        return carry

    lax.fori_loop(0, qi, body, 0)
    tile(pl.multiple_of(qi * tq, tq), True)

    lam = (jnp.exp(jnp.sum(lq1_ref[...] * lk1_ref[...], axis=-1, keepdims=True))
           - jnp.exp(jnp.sum(lq2_ref[...] * lk2_ref[...], axis=-1, keepdims=True)) + lambda_init)
    o = acc_ref[0] / l_ref[0] - lam * (acc_ref[1] / l_ref[1])
    o = o * lax.rsqrt(jnp.mean(o * o, axis=0, keepdims=True) + RMS_EPS) * sg_ref[...] * (1.0 - lambda_init)
    o_ref[...] = o.T.astype(o_ref.dtype)


def _diff_attention(zqk, vt, lq1, lk1, lq2, lk2, subln_g, *, batch, seq, heads, hd, out_cols, lambda_init):
    m = zqk.shape[0]
    vd = 2 * hd
    tq = _tile(seq, 512, V7X_LANES)
    nq = seq // tq
    vec = pl.BlockSpec((1, hd), lambda b, h, i: (0, 0))
    return pl.pallas_call(
        functools.partial(_attn_kernel, tq=tq, hd=hd, lambda_init=lambda_init),
        out_shape=jax.ShapeDtypeStruct((m, out_cols), BF16),
        grid=(batch, heads, nq),
        in_specs=[pl.BlockSpec((tq, vd), lambda b, h, i: (b * nq + i, h)),
                  pl.BlockSpec((seq, vd), lambda b, h, i: (b, heads + h)),
                  pl.BlockSpec((vd, seq), lambda b, h, i: (h, b)),
                  vec, vec, vec, vec,
                  pl.BlockSpec((vd, 1), lambda b, h, i: (0, 0))],
        out_specs=pl.BlockSpec((tq, vd), lambda b, h, i: (b * nq + i, h)),
        scratch_shapes=[pltpu.VMEM((2, 1, tq), F32), pltpu.VMEM((2, 1, tq), F32),
                        pltpu.VMEM((2, vd, tq), F32)],
        compiler_params=pltpu.CompilerParams(
            dimension_semantics=("arbitrary", "arbitrary", "arbitrary"),
            vmem_limit_bytes=_vmem_limit(2 * tq * vd * 2 + 2 * seq * vd * 2,
                                         2 * tq * vd * 4 + 10 * tq * tq * 4)),
        name="diff_attention",
    )(zqk, zqk, vt, lq1, lk1, lq2, lk2, subln_g)


def _split3_bf16(x):
    h1 = x.astype(BF16)
    r1 = x - h1.astype(F32)
    h2 = r1.astype(BF16)
    h3 = (r1 - h2.astype(F32)).astype(BF16)
    return h1, h2, h3


def _rwkv_kernel(z_ref, zl_ref, mu_ref, mul_ref, w0_ref, w2_ref, a0_ref, a2_ref, g2_ref, kk_ref, ka_ref, rk_ref,
                 lnw_ref, lnb_ref, alias_ref, o_ref,
                 prev_ref, prevl_ref, st_ref, lw_ref, cum_ref, as_ref, gt_ref, *, rw, hn, pairs_per_iter):
    del alias_ref
    c = pl.program_id(1)
    lt = z_ref.shape[0]
    ln = V7X_LANES

    @pl.when(c == 0)
    def _():
        prev_ref[...] = jnp.zeros(prev_ref.shape, F32)
        prevl_ref[...] = jnp.zeros(prevl_ref.shape, F32)
        st_ref[...] = jnp.zeros(st_ref.shape, F32)

    def shifted(zt, prev_row, mu):
        row = lax.broadcasted_iota(jnp.int32, zt.shape, 0)
        zp = jnp.where(row == 0, prev_row, pltpu.roll(zt, 1, 0))
        return zt + (zp - zt) * mu

    zl = zl_ref[...]
    zlm = shifted(zl, prevl_ref[0:1, :], mul_ref[...])
    prevl_ref[0:1, :] = zl[lt - 1:lt, :]
    dl = w2_ref.shape[0]
    al = a2_ref.shape[0]
    wpre = w0_ref[...] + _dot(jnp.tanh(zlm[:, :dl]).astype(BF16), w2_ref[...])
    softplus = jnp.maximum(-wpre, 0.0) + jnp.log(1.0 + jnp.exp(-jnp.abs(wpre)))
    lw = -jnp.exp(-softplus - 0.5)
    lw_ref[...] = lw
    tri = (lax.broadcasted_iota(jnp.int32, (lt, lt), 0)
           >= lax.broadcasted_iota(jnp.int32, (lt, lt), 1)).astype(BF16)
    h1, h2, h3 = _split3_bf16(lw)
    cum_ref[...] = _dot(tri, h1) + _dot(tri, h2) + _dot(tri, h3)
    as_ref[...] = jax.nn.sigmoid(a0_ref[...] + _dot(zlm[:, dl:dl + al].astype(BF16), a2_ref[...]))
    gt_ref[...] = _dot(jax.nn.sigmoid(zlm[:, dl + al:]).astype(BF16), g2_ref[...])

    lane = lax.broadcasted_iota(jnp.int32, (lt, ln), 1)
    first = lane < hn
    r2 = lax.broadcasted_iota(jnp.int32, (2 * hn, 2 * hn), 0)
    c2 = lax.broadcasted_iota(jnp.int32, (2 * hn, 2 * hn), 1)
    hsh = hn.bit_length() - 1
    same = lax.shift_right_logical(r2, hsh) == lax.shift_right_logical(c2, hsh)
    strict = same & (r2 > c2)
    incl = same & (r2 >= c2)
    eye = (r2 == c2).astype(F32)
    zeros = jnp.zeros((2 * hn, ln), F32)

    def headsum(x):
        s0 = jnp.sum(jnp.where(first, x, 0.0), axis=-1, keepdims=True)
        s1 = jnp.sum(jnp.where(first, 0.0, x), axis=-1, keepdims=True)
        return jnp.where(first, s0, s1)

    def bd(x):
        return jnp.concatenate([jnp.where(first, x, 0.0), jnp.where(first, 0.0, x)], axis=0)

    def fold(x):
        return x[:hn] + x[hn:]

    def mixed(off):
        sl = pl.ds(off, ln)
        return shifted(z_ref[:, sl], prev_ref[0:1, sl], mu_ref[:, sl])

    def pairs(offs):
        n2 = 2 * hn
        sls = [pl.ds(off, ln) for off in offs]
        rs = [mixed(off) for off in offs]
        ks = [mixed(rw + off) for off in offs]
        vs = [mixed(2 * rw + off) for off in offs]
        asgs = [as_ref[:, sl] for sl in sls]
        kks = [k * kk_ref[:, sl] for k, sl in zip(ks, sls)]
        kks = [kk / jnp.maximum(jnp.sqrt(headsum(kk * kk)), 1e-12) for kk in kks]
        kns = [k * (1.0 + (asg - 1.0) * ka_ref[:, sl]) for k, asg, sl in zip(ks, asgs, sls)]
        bns = [kk * asg for kk, asg in zip(kks, asgs)]

        lws = [lw_ref[:, sl] for sl in sls]
        cums = [cum_ref[:, sl] for sl in sls]
        tots = [cum[lt - 1:lt, :] for cum in cums]
        e_invs = [jnp.exp(-cum) for cum in cums]
        ats = [bd(-kk * jnp.exp(cum - lw)) for kk, cum, lw in zip(kks, cums, lws)]
        rts = [bd(r * jnp.exp(cum)) for r, cum in zip(rs, cums)]
        vbs = [bd(v) for v in vs]

        fulls = [_dot_nt(jnp.concatenate([at, rt], axis=0).astype(BF16),
                         jnp.concatenate([bd(bn * e), bd(kn * e)], axis=0).astype(BF16))
                 for at, rt, bn, kn, e in zip(ats, rts, bns, kns, e_invs)]
        a_abs = [jnp.where(strict, f[:n2, :n2], 0.0) for f in fulls]
        a_aks = [jnp.where(strict, f[:n2, n2:], 0.0) for f in fulls]
        a_rs = [jnp.concatenate([jnp.where(incl, f[n2:, :n2], 0.0), jnp.where(incl, f[n2:, n2:], 0.0)],
                                axis=1).astype(BF16) for f in fulls]

        akvs = [_dot(a_ak.astype(BF16), vb.astype(BF16)) for a_ak, vb in zip(a_aks, vbs)]
        zqs = [jnp.concatenate([at, akv], axis=1) for at, akv in zip(ats, akvs)]
        apows = a_abs
        steps = max(1, (lt - 1).bit_length())
        for it in range(steps):
            if it + 1 < steps:
                aws = [_dot(ap.astype(BF16), jnp.concatenate([zq, ap], axis=1).astype(BF16))
                       for ap, zq in zip(apows, zqs)]
                zqs = [zq + aw[:, :2 * ln] for zq, aw in zip(zqs, aws)]
                apows = [aw[:, 2 * ln:] for aw in aws]
            else:
                zqs = [zq + _dot(ap.astype(BF16), zq.astype(BF16)) for ap, zq in zip(apows, zqs)]

        rhs2s = [jnp.concatenate([zq, jnp.concatenate([zeros, vb], axis=1)], axis=0).astype(BF16)
                 for zq, vb in zip(zqs, vbs)]
        rys = [_dot(a_r, rhs2) for a_r, rhs2 in zip(a_rs, rhs2s)]
        e_ends = [jnp.exp(tot - cum) for tot, cum in zip(tots, cums)]
        ghs = [_dot_tn(jnp.concatenate([bd(bn * e), bd(kn * e)], axis=0).astype(BF16), rhs2)
               for bn, kn, e, rhs2 in zip(bns, kns, e_ends, rhs2s)]
        lhs4s = [jnp.concatenate([fold(rt + ry[:, :ln]), fold(gh[:, :ln] + eye * jnp.exp(tot))],
                                 axis=0).astype(BF16)
                 for rt, ry, gh, tot in zip(rts, rys, ghs, tots)]
        outs = [_dot(lhs4, bd(st_ref[:, sl]).astype(BF16)) for lhs4, sl in zip(lhs4s, sls)]
        for sl, out, ry, gh, r, kn, v in zip(sls, outs, rys, ghs, rs, kns, vs):
            y = out[:hn] + fold(ry[:, ln:])
            st_ref[:, sl] = out[hn:] + fold(gh[:, ln:])
            mean = headsum(y) * (1.0 / hn)
            yc = y - mean
            var = headsum(yc * yc) * (1.0 / hn)
            yn = yc * lax.rsqrt(var + RW_LN_EPS) * lnw_ref[:, sl] + lnb_ref[:, sl]
            bonus = headsum(r * kn * rk_ref[:, sl]) * v
            o_ref[:, sl] = ((yn + bonus) * gt_ref[:, sl]).astype(o_ref.dtype)

    npairs = rw // ln

    def body(i, carry):
        pairs([pl.multiple_of((i * pairs_per_iter + u) * ln, ln) for u in range(pairs_per_iter)])
        return carry

    lax.fori_loop(0, npairs // pairs_per_iter, body, 0)
    prev_ref[0:1, :] = z_ref[lt - 1:lt, :]


def _rwkv(zr, zl, omix, mu, mul, w0, w2, a0, a2, g2, k_k, k_a, r_k, ln_w, ln_b, *, batch, seq, rw, hn):
    m, zc = zr.shape
    lc = zl.shape[1]
    assert hn * 2 == V7X_LANES and RW_CHUNK == hn and seq % RW_CHUNK == 0 and zc == 3 * rw
    lt = RW_CHUNK
    nc = seq // lt
    npairs = rw // V7X_LANES
    ppi = _tile(npairs, 16, 1)
    ocol = omix.shape[1] // rw - 1
    assert (ocol + 1) * rw == omix.shape[1]
    row = lambda n: pl.BlockSpec((1, n), lambda b, c: (0, 0))
    full = lambda a: pl.BlockSpec(a.shape, lambda b, c: (0, 0))
    return pl.pallas_call(
        functools.partial(_rwkv_kernel, rw=rw, hn=hn, pairs_per_iter=ppi),
        out_shape=jax.ShapeDtypeStruct(omix.shape, omix.dtype),
        grid=(batch, nc),
        in_specs=[pl.BlockSpec((lt, zc), lambda b, c: (b * nc + c, 0)),
                  pl.BlockSpec((lt, lc), lambda b, c: (b * nc + c, 0)),
                  row(zc), row(lc), row(rw), full(w2), row(rw), full(a2), full(g2),
                  row(rw), row(rw), row(rw), row(rw), row(rw),
                  pl.BlockSpec(memory_space=pl.ANY)],
        out_specs=pl.BlockSpec((lt, rw), lambda b, c: (b * nc + c, ocol)),
        scratch_shapes=[pltpu.VMEM((8, zc), F32), pltpu.VMEM((8, lc), F32), pltpu.VMEM((hn, rw), F32),
                        pltpu.VMEM((lt, rw), F32), pltpu.VMEM((lt, rw), F32),
                        pltpu.VMEM((lt, rw), F32), pltpu.VMEM((lt, rw), F32)],
        input_output_aliases={14: 0},
        compiler_params=pltpu.CompilerParams(
            dimension_semantics=("arbitrary", "arbitrary"),
            vmem_limit_bytes=_vmem_limit(lt * (zc + lc) * 4 + (w2.size + a2.size + g2.size) * 2 + lt * rw * 2,
                                         8 * (zc + lc) * 4 + 5 * lt * rw * 4 + (8 << 20))),
        name="rwkv7_chunked",
    )(zr, zl, mu, mul, w0, w2, a0, a2, g2, k_k, k_a, r_k, ln_w, ln_b, omix)


def _pad_to(a, axis, n):
    pad = n - a.shape[axis]
    if pad == 0:
        return a
    widths = [(0, 0)] * a.ndim
    widths[axis] = (0, pad)
    return jnp.pad(a, widths)


def _ffn(h, x, w_gate, w_up, w_down, gpost, gnext):
    act = _gateup(h, w_gate, w_up)
    return _down(act, w_down.astype(BF16), x, gpost, gnext, 0.5)


def kernel(x, p, ffn1_pre_g, ffn1_w_gate, ffn1_w_up, ffn1_w_down, ffn1_post_g, mix_pre_g, w_in, diff_lambda_q1, diff_lambda_k1, diff_lambda_q2, diff_lambda_k2, diff_subln_g, rwkv_mu, rwkv_w0, rwkv_w2, rwkv_a0, rwkv_a2, rwkv_g2, rwkv_k_k, rwkv_k_a, rwkv_r_k, rwkv_ln_w, rwkv_ln_b, w_out, mix_post_g, ffn2_pre_g, ffn2_w_gate, ffn2_w_up, ffn2_w_down, ffn2_post_g, ple_pre_g, ple_w_gate, ple_w_proj, ple_post_g):
    batch, seq, d = x.shape
    depth = p.shape[0]
    m = batch * seq
    hd = diff_subln_g.shape[-1] // 2
    rw = rwkv_w2.shape[-1]
    da = w_out.shape[1] - rw
    heads = da // (2 * hd)
    qk = heads * 2 * hd
    rheads, hn = rwkv_r_k.shape[1], rwkv_r_k.shape[2]
    dl, al, gl = rwkv_w2.shape[1], rwkv_a2.shape[1], rwkv_g2.shape[1]
    dlp, alp, glp = (_round_up(n, V7X_LANES) for n in (dl, al, gl))
    rot = hd // 4
    assert rheads * hn == rw and da == rw and w_in.shape[2] == 2 * qk + da + 3 * rw + dl + al + gl

    inv = ROPE_THETA ** (-jnp.arange(0, rot, 2, dtype=F32) / rot)
    ang = jnp.arange(seq).astype(F32)[:, None] * inv[None, :]
    half = rot // 2
    ones = jnp.ones((seq, hd - rot), F32)
    zeros_h = jnp.zeros((seq, half), F32)
    zeros_r = jnp.zeros((seq, hd - rot), F32)
    cos_t = jnp.concatenate([jnp.cos(ang), jnp.cos(ang), ones], axis=1)
    sa_t = jnp.concatenate([-jnp.sin(ang), zeros_h, zeros_r], axis=1)
    sb_t = jnp.concatenate([zeros_h, jnp.sin(ang), zeros_r], axis=1)
    assert half == 16 and hd == V7X_LANES

    row = lambda a: a.reshape(1, -1).astype(F32)
    xf = x.reshape(m, d)
    for i in range(depth):
        lambda_init = 0.8 - 0.6 * math.exp(-0.3 * i)

        h = _prenorm(xf, row(ffn1_pre_g[i]))
        xf, h = _ffn(h, xf, ffn1_w_gate[i], ffn1_w_up[i], ffn1_w_down[i],
                     row(ffn1_post_g[i]), row(mix_pre_g[i]))

        wi = w_in[i]
        c0 = 2 * qk + da + 3 * rw
        w_l = jnp.concatenate([_pad_to(wi[:, c0:c0 + dl], 1, dlp),
                               _pad_to(wi[:, c0 + dl:c0 + dl + al], 1, alp),
                               _pad_to(wi[:, c0 + dl + al:], 1, glp)], axis=1).astype(BF16)
        mu = rwkv_mu[i]
        mu_l = jnp.concatenate([_pad_to(mu[3 * rw:3 * rw + dl], 0, dlp),
                                _pad_to(mu[3 * rw + dl:3 * rw + dl + al], 0, alp),
                                _pad_to(mu[3 * rw + dl + al:], 0, glp)])
        zqk = _qk_proj(h, wi[:, :2 * qk].astype(BF16), cos_t, sa_t, sb_t, seq=seq, qk_cols=qk,
                       qscale=hd ** -0.5 * math.log2(math.e))
        vt = _vt_proj(h, wi[:, 2 * qk:2 * qk + da].astype(BF16).T)
        zr = _plain_proj(h, wi[:, 2 * qk + da:c0].astype(BF16), 0, 3 * rw, F32)
        zl = _plain_proj(h, w_l, 0, w_l.shape[1], F32)

        omix = _diff_attention(zqk, vt, row(diff_lambda_q1[i]), row(diff_lambda_k1[i]),
                               row(diff_lambda_q2[i]), row(diff_lambda_k2[i]),
                               diff_subln_g[i].reshape(-1, 1).astype(F32),
                               batch=batch, seq=seq, heads=heads, hd=hd, out_cols=da + rw,
                               lambda_init=lambda_init)
        omix = _rwkv(zr, zl, omix, row(mu[:3 * rw]), row(mu_l), row(rwkv_w0[i]),
                     _pad_to(rwkv_w2[i].astype(BF16), 0, dlp), row(rwkv_a0[i]),
                     _pad_to(rwkv_a2[i].astype(BF16), 0, alp), _pad_to(rwkv_g2[i].astype(BF16), 0, glp),
                     row(rwkv_k_k[i]), row(rwkv_k_a[i]), row(rwkv_r_k[i]),
                     row(rwkv_ln_w[i]), row(rwkv_ln_b[i]), batch=batch, seq=seq, rw=rw, hn=hn)

        xf, h = _down(omix, w_out[i].astype(BF16), xf, row(mix_post_g[i]), row(ffn2_pre_g[i]), 1.0)
        xf, h = _ffn(h, xf, ffn2_w_gate[i], ffn2_w_up[i], ffn2_w_down[i],
                     row(ffn2_post_g[i]), row(ple_pre_g[i]))
        xf = _ple(h, ple_w_gate[i].astype(BF16), xf, p[i].reshape(m, -1).astype(BF16),
                  ple_w_proj[i].astype(BF16), row(ple_post_g[i]))
    return xf.reshape(batch, seq, d)
```

```python
import functools
import math

import jax
import jax.numpy as jnp
from jax import lax
from jax.experimental import pallas as pl
from jax.experimental.pallas import tpu as pltpu

F32 = jnp.float32
BF16 = jnp.bfloat16

RMS_EPS = 1e-6
RW_LN_EPS = 64e-5
ATTN_CHUNK = 64
ROPE_THETA = 500000.0
RW_CHUNK = 64

V7X_LANES = 128
V7X_VMEM_CAP_BYTES = 58 * 1024 * 1024


def _tile(n, pref, align):
    t = (min(n, pref) // align) * align
    while t >= align:
        if n % t == 0:
            return t
        t -= align
    return n


def _round_up(n, m):
    return (n + m - 1) // m * m


def _vmem_limit(pipelined_bytes, resident_bytes):
    need = 2 * pipelined_bytes + resident_bytes + (4 << 20)
    return int(min(V7X_VMEM_CAP_BYTES, max(need, 16 << 20)))


def _rms(x, g, eps=RMS_EPS):
    return x * lax.rsqrt(jnp.mean(x * x, axis=-1, keepdims=True) + eps) * g


def _dot(a, b):
    return jnp.dot(a, b, preferred_element_type=F32)


def _dot_nt(a, b):
    return lax.dot_general(a, b, (((1,), (1,)), ((), ())), preferred_element_type=F32)


def _dot_tn(a, b):
    return lax.dot_general(a, b, (((0,), (0,)), ((), ())), preferred_element_type=F32)


def _prenorm_kernel(x_ref, g_ref, h_ref):
    h_ref[...] = _rms(x_ref[...], g_ref[...]).astype(h_ref.dtype)


def _prenorm(x, g):
    m, d = x.shape
    tm = _tile(m, 256, 8)
    return pl.pallas_call(
        _prenorm_kernel,
        out_shape=jax.ShapeDtypeStruct((m, d), BF16),
        grid=(m // tm,),
        in_specs=[pl.BlockSpec((tm, d), lambda i: (i, 0)),
                  pl.BlockSpec((1, d), lambda i: (0, 0))],
        out_specs=pl.BlockSpec((tm, d), lambda i: (i, 0)),
        compiler_params=pltpu.CompilerParams(
            dimension_semantics=("arbitrary",),
            vmem_limit_bytes=_vmem_limit(tm * d * 6, 3 * tm * d * 4)),
        name="prenorm",
    )(x, g)


def _gateup_kernel(h_ref, wg_ref, wu_ref, wd_ref, o_ref, wdo_ref):
    h = h_ref[...]
    g = _dot(h, wg_ref[...].astype(BF16))
    u = _dot(h, wu_ref[...].astype(BF16))
    o_ref[...] = (g * jax.nn.sigmoid(g) * u).astype(o_ref.dtype)
    wdo_ref[...] = wd_ref[...].astype(wdo_ref.dtype)


def _gateup(h, wg, wu, wd):
    m, d = h.shape
    f = wg.shape[1]
    tm = _tile(m, 2048, 16)
    tn = _tile(f, 256, V7X_LANES)
    nj = f // tn
    steps = (m // tm) * nj
    slab = wd.shape[0] // steps
    assert slab * steps == wd.shape[0] and slab % 16 == 0
    return pl.pallas_call(
        _gateup_kernel,
        out_shape=(jax.ShapeDtypeStruct((m, f), BF16), jax.ShapeDtypeStruct(wd.shape, BF16)),
        grid=(m // tm, nj),
        in_specs=[pl.BlockSpec((tm, d), lambda i, j: (i, 0), pipeline_mode=pl.Buffered(1)),
                  pl.BlockSpec((d, tn), lambda i, j: (0, j)),
                  pl.BlockSpec((d, tn), lambda i, j: (0, j)),
                  pl.BlockSpec((slab, wd.shape[1]), lambda i, j: (i * nj + j, 0))],
        out_specs=(pl.BlockSpec((tm, tn), lambda i, j: (i, j)),
                   pl.BlockSpec((slab, wd.shape[1]), lambda i, j: (i * nj + j, 0))),
        compiler_params=pltpu.CompilerParams(
            dimension_semantics=("arbitrary", "arbitrary"),
            vmem_limit_bytes=_vmem_limit(2 * d * tn * 4 + tm * tn * 2 + slab * wd.shape[1] * 6,
                                         tm * d * 2 + 2 * d * tn * 2 + 4 * tm * tn * 4)),
        name="swiglu_up",
    )(h, wg, wu, wd)


def _deferred_rows(a_ref, w_ref, acc_ref, *, ntiles, nk, last, nchunks, epilogue):
    i = pl.program_id(0)
    k = pl.program_id(1)
    tk = a_ref.shape[1]
    cur = acc_ref.at[i % 2]
    prev = acc_ref.at[(i + 1) % 2]
    rc = acc_ref.shape[1] // nchunks
    full_end = nk - (last < tk)
    assert nchunks <= full_end
    mm = i < ntiles
    epi = (i >= 1) & (k < nchunks)

    def run_epilogue():
        epilogue(prev[pl.ds(pl.multiple_of(k * rc, rc), rc), :])

    @pl.when(mm & (k == 0) & (i == 0))
    def _():
        cur[...] = _dot(a_ref[...], w_ref[...])

    @pl.when(mm & (k == 0) & (i >= 1))
    def _():
        run_epilogue()
        cur[...] = _dot(a_ref[...], w_ref[...])

    @pl.when(mm & (k > 0) & (k < full_end) & epi)
    def _():
        run_epilogue()
        cur[...] += _dot(a_ref[...], w_ref[...])

    @pl.when(mm & (k > 0) & (k < full_end) & jnp.logical_not(epi))
    def _():
        cur[...] += _dot(a_ref[...], w_ref[...])

    if last < tk:
        @pl.when(mm & (k == nk - 1))
        def _():
            cur[...] += _dot(a_ref[:, :last], w_ref[:last, :])

    @pl.when(jnp.logical_not(mm) & (k < nchunks))
    def _():
        run_epilogue()


def _deferred_specs(m, kdim, d):
    tm = _tile(m, 1024, 16)
    tk = 512
    ntiles = m // tm
    nk = pl.cdiv(kdim, tk)
    last = kdim - (nk - 1) * tk
    assert last % V7X_LANES == 0 and nk >= 2
    nchunks = 1 << ((nk - (last < tk)).bit_length() - 1)
    nchunks = min(nchunks, tm // 8)
    rc = tm // nchunks
    a_spec = pl.BlockSpec((tm, tk), lambda i, k: (jnp.minimum(i, ntiles - 1), jnp.where(i < ntiles, k, nk - 1)))
    w_spec = pl.BlockSpec((tk, d), lambda i, k: (jnp.where(i < ntiles, k, nk - 1), 0))
    chunk = lambda i, k: (jnp.where(i >= 1, (i - 1) * nchunks + jnp.minimum(k, nchunks - 1), 0), 0)
    return tm, tk, ntiles, nk, last, nchunks, rc, a_spec, w_spec, chunk


def _down_kernel(a_ref, w_ref, x_ref, gpost_ref, gnext_ref, o_ref, h_ref, acc_ref, *, scale, **tiling):
    def epilogue(f):
        xn = x_ref[...] + scale * _rms(f, gpost_ref[...])
        o_ref[...] = xn
        h_ref[...] = _rms(xn, gnext_ref[...]).astype(h_ref.dtype)

    _deferred_rows(a_ref, w_ref, acc_ref, epilogue=epilogue, **tiling)


def _down(a, w, x, gpost, gnext, scale):
    m, kdim = a.shape
    d = w.shape[1]
    tm, tk, ntiles, nk, last, nchunks, rc, a_spec, w_spec, chunk = _deferred_specs(m, kdim, d)
    return pl.pallas_call(
        functools.partial(_down_kernel, scale=scale, ntiles=ntiles, nk=nk, last=last, nchunks=nchunks),
        out_shape=(jax.ShapeDtypeStruct((m, d), F32), jax.ShapeDtypeStruct((m, d), BF16)),
        grid=(ntiles + 1, nk),
        in_specs=[a_spec, w_spec,
                  pl.BlockSpec((rc, d), chunk),
                  pl.BlockSpec((1, d), lambda i, k: (0, 0)),
                  pl.BlockSpec((1, d), lambda i, k: (0, 0))],
        out_specs=(pl.BlockSpec((rc, d), chunk), pl.BlockSpec((rc, d), chunk)),
        scratch_shapes=[pltpu.VMEM((2, tm, d), F32)],
        compiler_params=pltpu.CompilerParams(
            dimension_semantics=("arbitrary", "arbitrary"),
            vmem_limit_bytes=_vmem_limit(tm * tk * 2 + tk * d * 2 + rc * d * 10, 2 * tm * d * 4 + 6 * rc * d * 4)),
        name="down_norm_residual",
    )(a, w, x, gpost, gnext)


def _ple_kernel(a_ref, w_ref, x_ref, p_ref, wp_ref, gpost_ref, o_ref, acc_ref, **tiling):
    def epilogue(f):
        proj = _dot(p_ref[...], wp_ref[...])
        o_ref[...] = x_ref[...] + _rms(proj * jax.nn.sigmoid(f), gpost_ref[...])

    _deferred_rows(a_ref, w_ref, acc_ref, epilogue=epilogue, **tiling)


def _ple(a, w, x, p, wp, gpost):
    m, kdim = a.shape
    d = w.shape[1]
    pd = p.shape[1]
    tm, tk, ntiles, nk, last, nchunks, rc, a_spec, w_spec, chunk = _deferred_specs(m, kdim, d)
    return pl.pallas_call(
        functools.partial(_ple_kernel, ntiles=ntiles, nk=nk, last=last, nchunks=nchunks),
        out_shape=jax.ShapeDtypeStruct((m, d), F32),
        grid=(ntiles + 1, nk),
        in_specs=[a_spec, w_spec,
                  pl.BlockSpec((rc, d), chunk),
                  pl.BlockSpec((rc, pd), chunk),
                  pl.BlockSpec((pd, d), lambda i, k: (0, 0)),
                  pl.BlockSpec((1, d), lambda i, k: (0, 0))],
        out_specs=pl.BlockSpec((rc, d), chunk),
        scratch_shapes=[pltpu.VMEM((2, tm, d), F32)],
        compiler_params=pltpu.CompilerParams(
            dimension_semantics=("arbitrary", "arbitrary"),
            vmem_limit_bytes=_vmem_limit(tm * tk * 2 + tk * d * 2 + rc * d * 8 + rc * pd * 2 + pd * d * 2,
                                         2 * tm * d * 4 + 6 * rc * d * 4)),
        name="ple_gate_norm_residual",
    )(a, w, x, p, wp, gpost)


def _qk_kernel(h_ref, w_ref, cos_ref, sa_ref, sb_ref, o_ref, *, q_tiles, qscale):
    j = pl.program_id(1)
    w = w_ref[...].astype(BF16)
    sc = jnp.where(j < q_tiles, qscale, 1.0).astype(F32)
    half = h_ref.shape[0] // 2
    for r in range(2):
        rows = pl.ds(r * half, half)
        z = _dot(h_ref[rows, :], w)
        c, sa, sb = cos_ref[rows, :], sa_ref[rows, :], sb_ref[rows, :]
        outs = []
        for g in range(z.shape[1] // V7X_LANES):
            zg = z[:, g * V7X_LANES:(g + 1) * V7X_LANES]
            rg = zg * c + pltpu.roll(zg, V7X_LANES - 16, 1) * sa + pltpu.roll(zg, 16, 1) * sb
            outs.append(rg * sc)
        o_ref[rows, :] = jnp.concatenate(outs, axis=1).astype(o_ref.dtype)


def _qk_proj(h, w, cos_t, sa_t, sb_t, *, seq, qk_cols, qscale):
    m, d = h.shape
    n = 2 * qk_cols
    tm = _tile(seq, 1024, 16)
    tn = _tile(qk_cols, 512, V7X_LANES)
    tpb = seq // tm
    return pl.pallas_call(
        functools.partial(_qk_kernel, q_tiles=qk_cols // tn, qscale=qscale),
        out_shape=jax.ShapeDtypeStruct((m, n), BF16),
        grid=(m // tm, n // tn),
        in_specs=[pl.BlockSpec((tm, d), lambda i, j: (i, 0)),
                  pl.BlockSpec((d, tn), lambda i, j: (0, j)),
                  pl.BlockSpec((tm, V7X_LANES), lambda i, j: (i % tpb, 0)),
                  pl.BlockSpec((tm, V7X_LANES), lambda i, j: (i % tpb, 0)),
                  pl.BlockSpec((tm, V7X_LANES), lambda i, j: (i % tpb, 0))],
        out_specs=pl.BlockSpec((tm, tn), lambda i, j: (i, j)),
        compiler_params=pltpu.CompilerParams(
            dimension_semantics=("arbitrary", "arbitrary"),
            vmem_limit_bytes=_vmem_limit(tm * d * 2 + d * tn * 4 + tm * tn * 2 + 3 * tm * V7X_LANES * 4,
                                         d * tn * 2 + 5 * tm * tn * 4)),
        name="qk_proj_rotary",
    )(h, w, cos_t, sa_t, sb_t)


def _vt_proj_kernel(wt_ref, h_ref, o_ref):
    o_ref[...] = _dot_nt(wt_ref[...], h_ref[...]).astype(o_ref.dtype)


def _vt_proj(h, wt):
    m, d = h.shape
    n = wt.shape[0]
    tm = _tile(m, 1024, V7X_LANES)
    tn = _tile(n, 512, 16)
    return pl.pallas_call(
        _vt_proj_kernel,
        out_shape=jax.ShapeDtypeStruct((n, m), BF16),
        grid=(m // tm, n // tn),
        in_specs=[pl.BlockSpec((tn, d), lambda i, j: (j, 0)),
                  pl.BlockSpec((tm, d), lambda i, j: (i, 0))],
        out_specs=pl.BlockSpec((tn, tm), lambda i, j: (j, i)),
        compiler_params=pltpu.CompilerParams(
            dimension_semantics=("arbitrary", "arbitrary"),
            vmem_limit_bytes=_vmem_limit(tm * d * 2 + d * tn * 2 + tm * tn * 2, 2 * tm * tn * 4)),
        name="v_proj_transposed",
    )(wt, h)


def _plain_proj_kernel(h_ref, w_ref, o_ref):
    o_ref[...] = _dot(h_ref[...], w_ref[...].astype(BF16)).astype(o_ref.dtype)


def _plain_proj(h, w, col0, n, out_dtype):
    m, d = h.shape
    tm = _tile(m, 1024, 16)
    tn = _tile(math.gcd(n, col0) if col0 else n, 512, V7X_LANES)
    j0 = col0 // tn
    return pl.pallas_call(
        _plain_proj_kernel,
        out_shape=jax.ShapeDtypeStruct((m, n), out_dtype),
        grid=(m // tm, n // tn),
        in_specs=[pl.BlockSpec((tm, d), lambda i, j: (i, 0)),
                  pl.BlockSpec((d, tn), lambda i, j: (0, j0 + j))],
        out_specs=pl.BlockSpec((tm, tn), lambda i, j: (i, j)),
        compiler_params=pltpu.CompilerParams(
            dimension_semantics=("arbitrary", "arbitrary"),
            vmem_limit_bytes=_vmem_limit(tm * d * 2 + d * tn * 4 + tm * tn * 4, d * tn * 2 + 2 * tm * tn * 4)),
        name="rwkv_proj",
    )(h, w)


def _attn_kernel(q_ref, k_ref, vt_ref, lq1_ref, lk1_ref, lq2_ref, lk2_ref, sg_ref, o_ref,
                 m_ref, l_ref, acc_ref, *, tq, hd, lambda_init):
    qi = pl.program_id(2)
    q = q_ref[...]
    m_ref[...] = jnp.full(m_ref.shape, -jnp.inf, F32)
    l_ref[...] = jnp.zeros(l_ref.shape, F32)
    acc_ref[...] = jnp.zeros(acc_ref.shape, F32)

    def tile(start, diagonal):
        kt = k_ref[pl.ds(start, tq), :]
        vt = vt_ref[:, pl.ds(start, tq)]
        ss = [_dot_nt(kt[:, c * hd:(c + 1) * hd], q[:, c * hd:(c + 1) * hd]) for c in range(2)]
        if diagonal:
            sh = ATTN_CHUNK.bit_length() - 1
            ck = lax.shift_right_logical(lax.broadcasted_iota(jnp.int32, ss[0].shape, 0), sh)
            rq = lax.shift_right_logical(lax.broadcasted_iota(jnp.int32, ss[0].shape, 1), sh)
            ss = [jnp.where(ck <= rq, s, -jnp.inf) for s in ss]
        m_olds = [m_ref[c] for c in range(2)]
        m_news = [jnp.maximum(m_old, jnp.max(s, axis=0, keepdims=True)) for m_old, s in zip(m_olds, ss)]
        ps = [jnp.exp2(s - m_new) for s, m_new in zip(ss, m_news)]
        alphas = [jnp.exp2(m_old - m_new) for m_old, m_new in zip(m_olds, m_news)]
        pvs = [_dot(vt, p.astype(BF16)) for p in ps]
        for c in range(2):
            l_ref[c] = alphas[c] * l_ref[c] + jnp.sum(ps[c], axis=0, keepdims=True)
            acc_ref[c] = alphas[c] * acc_ref[c] + pvs[c]
            m_ref[c] = m_news[c]

    def body(j, carry):
        tile(pl.multiple_of(j * tq, tq), False)
        return carry

    lax.fori_loop(0, qi, body, 0)
    tile(pl.multiple_of(qi * tq, tq), True)

    lam = (jnp.exp(jnp.sum(lq1_ref[...] * lk1_ref[...], axis=-1, keepdims=True))
           - jnp.exp(jnp.sum(lq2_ref[...] * lk2_ref[...], axis=-1, keepdims=True)) + lambda_init)
    o = acc_ref[0] / l_ref[0] - lam * (acc_ref[1] / l_ref[1])
    o = o * lax.rsqrt(jnp.mean(o * o, axis=0, keepdims=True) + RMS_EPS) * sg_ref[...] * (1.0 - lambda_init)
    o_ref[...] = o.T.astype(o_ref.dtype)


def _diff_attention(zqk, vt, lq1, lk1, lq2, lk2, subln_g, *, batch, seq, heads, hd, out_cols, lambda_init):
    m = zqk.shape[0]
    vd = 2 * hd
    tq = _tile(seq, 512, V7X_LANES)
    nq = seq // tq
    vec = pl.BlockSpec((1, hd), lambda b, h, i: (0, 0))
    return pl.pallas_call(
        functools.partial(_attn_kernel, tq=tq, hd=hd, lambda_init=lambda_init),
        out_shape=jax.ShapeDtypeStruct((m, out_cols), BF16),
        grid=(batch, heads, nq),
        in_specs=[pl.BlockSpec((tq, vd), lambda b, h, i: (b * nq + i, h)),
                  pl.BlockSpec((seq, vd), lambda b, h, i: (b, heads + h)),
                  pl.BlockSpec((vd, seq), lambda b, h, i: (h, b)),
                  vec, vec, vec, vec,
                  pl.BlockSpec((vd, 1), lambda b, h, i: (0, 0))],
        out_specs=pl.BlockSpec((tq, vd), lambda b, h, i: (b * nq + i, h)),
        scratch_shapes=[pltpu.VMEM((2, 1, tq), F32), pltpu.VMEM((2, 1, tq), F32),
                        pltpu.VMEM((2, vd, tq), F32)],
        compiler_params=pltpu.CompilerParams(
            dimension_semantics=("arbitrary", "arbitrary", "arbitrary"),
            vmem_limit_bytes=_vmem_limit(2 * tq * vd * 2 + 2 * seq * vd * 2,
                                         2 * tq * vd * 4 + 10 * tq * tq * 4)),
        name="diff_attention",
    )(zqk, zqk, vt, lq1, lk1, lq2, lk2, subln_g)


def _split3_bf16(x):
    h1 = x.astype(BF16)
    r1 = x - h1.astype(F32)
    h2 = r1.astype(BF16)
    h3 = (r1 - h2.astype(F32)).astype(BF16)
    return h1, h2, h3


def _rwkv_kernel(z_ref, zlc_ref, zln_ref, mu_ref, mul_ref, w0_ref, w2_ref, a0_ref, a2_ref, g2_ref,
                 kk_ref, ka_ref, rk_ref, lnw_ref, lnb_ref, alias_ref, o_ref,
                 prev_ref, prevl_ref, st_ref, *side_refs, rw, hn, nc):
    del alias_ref
    c = pl.program_id(1)
    g = pl.program_id(0) * nc + c
    lt = z_ref.shape[0]
    ln = V7X_LANES
    sides = (side_refs[:4], side_refs[4:])

    @pl.when(c == 0)
    def _():
        prev_ref[...] = jnp.zeros(prev_ref.shape, F32)
        st_ref[...] = jnp.zeros(st_ref.shape, F32)

    def shifted(zt, prev_row, mu):
        row = lax.broadcasted_iota(jnp.int32, zt.shape, 0)
        zp = jnp.where(row == 0, prev_row, pltpu.roll(zt, 1, 0))
        return zt + (zp - zt) * mu

    def side_paths(zl, prev_row, dst):
        lw_ref, cum_ref, as_ref, gt_ref = dst
        zlm = shifted(zl, prev_row, mul_ref[...])
        dl = w2_ref.shape[0]
        al = a2_ref.shape[0]
        wpre = w0_ref[...] + _dot(jnp.tanh(zlm[:, :dl]).astype(BF16), w2_ref[...])
        softplus = jnp.maximum(-wpre, 0.0) + jnp.log(1.0 + jnp.exp(-jnp.abs(wpre)))
        lw = -jnp.exp(-softplus - 0.5)
        lw_ref[...] = lw
        tri = (lax.broadcasted_iota(jnp.int32, (lt, lt), 0)
               >= lax.broadcasted_iota(jnp.int32, (lt, lt), 1)).astype(BF16)
        h1, h2, h3 = _split3_bf16(lw)
        cum_ref[...] = _dot(tri, h1) + _dot(tri, h2) + _dot(tri, h3)
        as_ref[...] = jax.nn.sigmoid(a0_ref[...] + _dot(zlm[:, dl:dl + al].astype(BF16), a2_ref[...]))
        gt_ref[...] = _dot(jax.nn.sigmoid(zlm[:, dl + al:]).astype(BF16), g2_ref[...])

    @pl.when(g == 0)
    def _():
        zl = zlc_ref[...]
        side_paths(zl, jnp.zeros((1, zl.shape[1]), F32), sides[0])
        prevl_ref[0:1, :] = zl[lt - 1:lt, :]

    lane = lax.broadcasted_iota(jnp.int32, (lt, ln), 1)
    first = lane < hn
    r2 = lax.broadcasted_iota(jnp.int32, (2 * hn, 2 * hn), 0)
    c2 = lax.broadcasted_iota(jnp.int32, (2 * hn, 2 * hn), 1)
    hsh = hn.bit_length() - 1
    same = lax.shift_right_logical(r2, hsh) == lax.shift_right_logical(c2, hsh)
    strict = same & (r2 > c2)
    incl = same & (r2 >= c2)
    eye = (r2 == c2).astype(F32)
    zeros = jnp.zeros((2 * hn, ln), F32)

    def headsum(x):
        s0 = jnp.sum(jnp.where(first, x, 0.0), axis=-1, keepdims=True)
        s1 = jnp.sum(jnp.where(first, 0.0, x), axis=-1, keepdims=True)
        return jnp.where(first, s0, s1)

    def bd(x):
        return jnp.concatenate([jnp.where(first, x, 0.0), jnp.where(first, 0.0, x)], axis=0)

    def fold(x):
        return x[:hn] + x[hn:]

    def mixed(off):
        sl = pl.ds(off, ln)
        return shifted(z_ref[:, sl], prev_ref[0:1, sl], mu_ref[:, sl])

    def pairs(offs, src):
        lw_ref, cum_ref, as_ref, gt_ref = src
        n2 = 2 * hn
        sls = [pl.ds(off, ln) for off in offs]
        rs = [mixed(off) for off in offs]
        ks = [mixed(rw + off) for off in offs]
        vs = [mixed(2 * rw + off) for off in offs]
        asgs = [as_ref[:, sl] for sl in sls]
        kks = [k * kk_ref[:, sl] for k, sl in zip(ks, sls)]
        kks = [kk / jnp.maximum(jnp.sqrt(headsum(kk * kk)), 1e-12) for kk in kks]
        kns = [k * (1.0 + (asg - 1.0) * ka_ref[:, sl]) for k, asg, sl in zip(ks, asgs, sls)]
        bns = [kk * asg for kk, asg in zip(kks, asgs)]

        lws = [lw_ref[:, sl] for sl in sls]
        cums = [cum_ref[:, sl] for sl in sls]
        tots = [cum[lt - 1:lt, :] for cum in cums]
        e_invs = [jnp.exp(-cum) for cum in cums]
        ats = [bd(-kk * jnp.exp(cum - lw)) for kk, cum, lw in zip(kks, cums, lws)]
        rts = [bd(r * jnp.exp(cum)) for r, cum in zip(rs, cums)]
        vbs = [bd(v) for v in vs]

        fulls = [_dot_nt(jnp.concatenate([at, rt], axis=0).astype(BF16),
                         jnp.concatenate([bd(bn * e), bd(kn * e)], axis=0).astype(BF16))
                 for at, rt, bn, kn, e in zip(ats, rts, bns, kns, e_invs)]
        a_abs = [jnp.where(strict, f[:n2, :n2], 0.0) for f in fulls]
        a_aks = [jnp.where(strict, f[:n2, n2:], 0.0) for f in fulls]
        a_rs = [jnp.concatenate([jnp.where(incl, f[n2:, :n2], 0.0), jnp.where(incl, f[n2:, n2:], 0.0)],
                                axis=1).astype(BF16) for f in fulls]

        akvs = [_dot(a_ak.astype(BF16), vb.astype(BF16)) for a_ak, vb in zip(a_aks, vbs)]
        zqs = [jnp.concatenate([at, akv], axis=1) for at, akv in zip(ats, akvs)]
        apows = a_abs
        steps = max(1, (lt - 1).bit_length())
        for it in range(steps):
            if it + 1 < steps:
                aws = [_dot(ap.astype(BF16), jnp.concatenate([zq, ap], axis=1).astype(BF16))
                       for ap, zq in zip(apows, zqs)]
                zqs = [zq + aw[:, :2 * ln] for zq, aw in zip(zqs, aws)]
                apows = [aw[:, 2 * ln:] for aw in aws]
            else:
                zqs = [zq + _dot(ap.astype(BF16), zq.astype(BF16)) for ap, zq in zip(apows, zqs)]

        rhs2s = [jnp.concatenate([zq, jnp.concatenate([zeros, vb], axis=1)], axis=0).astype(BF16)
                 for zq, vb in zip(zqs, vbs)]
        rys = [_dot(a_r, rhs2) for a_r, rhs2 in zip(a_rs, rhs2s)]
        e_ends = [jnp.exp(tot - cum) for tot, cum in zip(tots, cums)]
        ghs = [_dot_tn(jnp.concatenate([bd(bn * e), bd(kn * e)], axis=0).astype(BF16), rhs2)
               for bn, kn, e, rhs2 in zip(bns, kns, e_ends, rhs2s)]
        lhs4s = [jnp.concatenate([fold(rt + ry[:, :ln]), fold(gh[:, :ln] + eye * jnp.exp(tot))],
                                 axis=0).astype(BF16)
                 for rt, ry, gh, tot in zip(rts, rys, ghs, tots)]
        outs = [_dot(lhs4, bd(st_ref[:, sl]).astype(BF16)) for lhs4, sl in zip(lhs4s, sls)]
        for sl, out, ry, gh, r, kn, v in zip(sls, outs, rys, ghs, rs, kns, vs):
            y = out[:hn] + fold(ry[:, ln:])
            st_ref[:, sl] = out[hn:] + fold(gh[:, ln:])
            mean = headsum(y) * (1.0 / hn)
            yc = y - mean
            var = headsum(yc * yc) * (1.0 / hn)
            yn = yc * lax.rsqrt(var + RW_LN_EPS) * lnw_ref[:, sl] + lnb_ref[:, sl]
            bonus = headsum(r * kn * rk_ref[:, sl]) * v
            o_ref[:, sl] = ((yn + bonus) * gt_ref[:, sl]).astype(o_ref.dtype)

    def step(cur, nxt):
        zln = zln_ref[...]
        prev_row = jnp.where(c == nc - 1, 0.0, prevl_ref[0:1, :])
        side_paths(zln, prev_row, nxt)
        prevl_ref[0:1, :] = zln[lt - 1:lt, :]
        pairs([u * ln for u in range(rw // ln)], cur)

    for parity in range(2):
        @pl.when(g % 2 == parity)
        def _():
            step(sides[parity], sides[1 - parity])

    prev_ref[0:1, :] = z_ref[lt - 1:lt, :]


def _rwkv(zr, zl, omix, mu, mul, w0, w2, a0, a2, g2, k_k, k_a, r_k, ln_w, ln_b, *, batch, seq, rw, hn):
    m, zc = zr.shape
    lc = zl.shape[1]
    assert hn * 2 == V7X_LANES and RW_CHUNK == hn and seq % RW_CHUNK == 0 and zc == 3 * rw
    lt = RW_CHUNK
    nc = seq // lt
    nsteps = batch * nc
    ocol = omix.shape[1] // rw - 1
    assert (ocol + 1) * rw == omix.shape[1]
    row = lambda n: pl.BlockSpec((1, n), lambda b, c: (0, 0))
    full = lambda a: pl.BlockSpec(a.shape, lambda b, c: (0, 0))
    return pl.pallas_call(
        functools.partial(_rwkv_kernel, rw=rw, hn=hn, nc=nc),
        out_shape=jax.ShapeDtypeStruct(omix.shape, omix.dtype),
        grid=(batch, nc),
        in_specs=[pl.BlockSpec((lt, zc), lambda b, c: (b * nc + c, 0)),
                  pl.BlockSpec((lt, lc), lambda b, c: (b * nc + c, 0)),
                  pl.BlockSpec((lt, lc), lambda b, c: (jnp.minimum(b * nc + c + 1, nsteps - 1), 0)),
                  row(zc), row(lc), row(rw), full(w2), row(rw), full(a2), full(g2),
                  row(rw), row(rw), row(rw), row(rw), row(rw),
                  pl.BlockSpec(memory_space=pl.ANY)],
        out_specs=pl.BlockSpec((lt, rw), lambda b, c: (b * nc + c, ocol)),
        scratch_shapes=[pltpu.VMEM((8, zc), F32), pltpu.VMEM((8, lc), F32), pltpu.VMEM((hn, rw), F32)]
                       + [pltpu.VMEM((lt, rw), F32)] * 8,
        input_output_aliases={15: 0},
        compiler_params=pltpu.CompilerParams(
            dimension_semantics=("arbitrary", "arbitrary"),
            vmem_limit_bytes=_vmem_limit(lt * (zc + 2 * lc) * 4 + (w2.size + a2.size + g2.size) * 2 + lt * rw * 2,
                                         8 * (zc + lc) * 4 + 9 * lt * rw * 4 + (8 << 20))),
        name="rwkv7_chunked",
    )(zr, zl, zl, mu, mul, w0, w2, a0, a2, g2, k_k, k_a, r_k, ln_w, ln_b, omix)


def _pad_to(a, axis, n):
    pad = n - a.shape[axis]
    if pad == 0:
        return a
    widths = [(0, 0)] * a.ndim
    widths[axis] = (0, pad)
    return jnp.pad(a, widths)


def _ffn(h, x, w_gate, w_up, w_down, gpost, gnext):
    act, wd = _gateup(h, w_gate, w_up, w_down)
    return _down(act, wd, x, gpost, gnext, 0.5)


def kernel(x, p, ffn1_pre_g, ffn1_w_gate, ffn1_w_up, ffn1_w_down, ffn1_post_g, mix_pre_g, w_in, diff_lambda_q1, diff_lambda_k1, diff_lambda_q2, diff_lambda_k2, diff_subln_g, rwkv_mu, rwkv_w0, rwkv_w2, rwkv_a0, rwkv_a2, rwkv_g2, rwkv_k_k, rwkv_k_a, rwkv_r_k, rwkv_ln_w, rwkv_ln_b, w_out, mix_post_g, ffn2_pre_g, ffn2_w_gate, ffn2_w_up, ffn2_w_down, ffn2_post_g, ple_pre_g, ple_w_gate, ple_w_proj, ple_post_g):
    batch, seq, d = x.shape
    depth = p.shape[0]
    m = batch * seq
    hd = diff_subln_g.shape[-1] // 2
    rw = rwkv_w2.shape[-1]
    da = w_out.shape[1] - rw
    heads = da // (2 * hd)
    qk = heads * 2 * hd
    rheads, hn = rwkv_r_k.shape[1], rwkv_r_k.shape[2]
    dl, al, gl = rwkv_w2.shape[1], rwkv_a2.shape[1], rwkv_g2.shape[1]
    dlp, alp, glp = (_round_up(n, V7X_LANES) for n in (dl, al, gl))
    rot = hd // 4
    assert rheads * hn == rw and da == rw and w_in.shape[2] == 2 * qk + da + 3 * rw + dl + al + gl

    inv = ROPE_THETA ** (-jnp.arange(0, rot, 2, dtype=F32) / rot)
    ang = jnp.arange(seq).astype(F32)[:, None] * inv[None, :]
    half = rot // 2
    ones = jnp.ones((seq, hd - rot), F32)
    zeros_h = jnp.zeros((seq, half), F32)
    zeros_r = jnp.zeros((seq, hd - rot), F32)
    cos_t = jnp.concatenate([jnp.cos(ang), jnp.cos(ang), ones], axis=1)
    sa_t = jnp.concatenate([-jnp.sin(ang), zeros_h, zeros_r], axis=1)
    sb_t = jnp.concatenate([zeros_h, jnp.sin(ang), zeros_r], axis=1)
    assert half == 16 and hd == V7X_LANES

    row = lambda a: a.reshape(1, -1).astype(F32)
    xf = x.reshape(m, d)
    for i in range(depth):
        lambda_init = 0.8 - 0.6 * math.exp(-0.3 * i)

        h = _prenorm(xf, row(ffn1_pre_g[i]))
        xf, h = _ffn(h, xf, ffn1_w_gate[i], ffn1_w_up[i], ffn1_w_down[i],
                     row(ffn1_post_g[i]), row(mix_pre_g[i]))

        wi = w_in[i]
        c0 = 2 * qk + da + 3 * rw
        w_l = jnp.concatenate([_pad_to(wi[:, c0:c0 + dl], 1, dlp),
                               _pad_to(wi[:, c0 + dl:c0 + dl + al], 1, alp),
                               _pad_to(wi[:, c0 + dl + al:], 1, glp)], axis=1).astype(BF16)
        mu = rwkv_mu[i]
        mu_l = jnp.concatenate([_pad_to(mu[3 * rw:3 * rw + dl], 0, dlp),
                                _pad_to(mu[3 * rw + dl:3 * rw + dl + al], 0, alp),
                                _pad_to(mu[3 * rw + dl + al:], 0, glp)])
        zqk = _qk_proj(h, wi[:, :2 * qk].astype(BF16), cos_t, sa_t, sb_t, seq=seq, qk_cols=qk,
                       qscale=hd ** -0.5 * math.log2(math.e))
        vt = _vt_proj(h, wi[:, 2 * qk:2 * qk + da].astype(BF16).T)
        zr = _plain_proj(h, wi[:, 2 * qk + da:c0].astype(BF16), 0, 3 * rw, F32)
        zl = _plain_proj(h, w_l, 0, w_l.shape[1], F32)

        omix = _diff_attention(zqk, vt, row(diff_lambda_q1[i]), row(diff_lambda_k1[i]),
                               row(diff_lambda_q2[i]), row(diff_lambda_k2[i]),
                               diff_subln_g[i].reshape(-1, 1).astype(F32),
                               batch=batch, seq=seq, heads=heads, hd=hd, out_cols=da + rw,
                               lambda_init=lambda_init)
        omix = _rwkv(zr, zl, omix, row(mu[:3 * rw]), row(mu_l), row(rwkv_w0[i]),
                     _pad_to(rwkv_w2[i].astype(BF16), 0, dlp), row(rwkv_a0[i]),
                     _pad_to(rwkv_a2[i].astype(BF16), 0, alp), _pad_to(rwkv_g2[i].astype(BF16), 0, glp),
                     row(rwkv_k_k[i]), row(rwkv_k_a[i]), row(rwkv_r_k[i]),
                     row(rwkv_ln_w[i]), row(rwkv_ln_b[i]), batch=batch, seq=seq, rw=rw, hn=hn)

        xf, h = _down(omix, w_out[i].astype(BF16), xf, row(mix_post_g[i]), row(ffn2_pre_g[i]), 1.0)
        xf, h = _ffn(h, xf, ffn2_w_gate[i], ffn2_w_up[i], ffn2_w_down[i],
                     row(ffn2_post_g[i]), row(ple_pre_g[i]))
        xf = _ple(h, ple_w_gate[i].astype(BF16), xf, p[i].reshape(m, -1).astype(BF16),
                  ple_w_proj[i].astype(BF16), row(ple_post_g[i]))
    return xf.reshape(batch, seq, d)
```

```python
import functools
import math

import jax
import jax.numpy as jnp
from jax import lax
from jax.experimental import pallas as pl
from jax.experimental.pallas import tpu as pltpu

F32 = jnp.float32
BF16 = jnp.bfloat16

RMS_EPS = 1e-6
RW_LN_EPS = 64e-5
ATTN_CHUNK = 64
ROPE_THETA = 500000.0
RW_CHUNK = 64

V7X_LANES = 128
V7X_VMEM_CAP_BYTES = 58 * 1024 * 1024


def _tile(n, pref, align):
    t = (min(n, pref) // align) * align
    while t >= align:
        if n % t == 0:
            return t
        t -= align
    return n


def _round_up(n, m):
    return (n + m - 1) // m * m


def _vmem_limit(pipelined_bytes, resident_bytes):
    need = 2 * pipelined_bytes + resident_bytes + (4 << 20)
    return int(min(V7X_VMEM_CAP_BYTES, max(need, 16 << 20)))


def _rms(x, g, eps=RMS_EPS):
    return x * lax.rsqrt(jnp.mean(x * x, axis=-1, keepdims=True) + eps) * g


def _dot(a, b):
    return jnp.dot(a, b, preferred_element_type=F32)


def _dot_nt(a, b):
    return lax.dot_general(a, b, (((1,), (1,)), ((), ())), preferred_element_type=F32)


def _dot_tn(a, b):
    return lax.dot_general(a, b, (((0,), (0,)), ((), ())), preferred_element_type=F32)


def _prenorm_kernel(x_ref, g_ref, h_ref):
    h_ref[...] = _rms(x_ref[...], g_ref[...]).astype(h_ref.dtype)


def _prenorm(x, g):
    m, d = x.shape
    tm = _tile(m, 256, 8)
    return pl.pallas_call(
        _prenorm_kernel,
        out_shape=jax.ShapeDtypeStruct((m, d), BF16),
        grid=(m // tm,),
        in_specs=[pl.BlockSpec((tm, d), lambda i: (i, 0)),
                  pl.BlockSpec((1, d), lambda i: (0, 0))],
        out_specs=pl.BlockSpec((tm, d), lambda i: (i, 0)),
        compiler_params=pltpu.CompilerParams(
            dimension_semantics=("arbitrary",),
            vmem_limit_bytes=_vmem_limit(tm * d * 6, 3 * tm * d * 4)),
        name="prenorm",
    )(x, g)


def _gateup_kernel(h_ref, wg_ref, wu_ref, wd_ref, o_ref, wdo_ref):
    h = h_ref[...]
    g = _dot(h, wg_ref[...].astype(BF16))
    u = _dot(h, wu_ref[...].astype(BF16))
    o_ref[...] = (g * jax.nn.sigmoid(g) * u).astype(o_ref.dtype)
    wdo_ref[...] = wd_ref[...].astype(wdo_ref.dtype)


def _gateup(h, wg, wu, wd):
    m, d = h.shape
    f = wg.shape[1]
    tm = _tile(m, 2048, 16)
    tn = _tile(f, 256, V7X_LANES)
    nj = f // tn
    steps = (m // tm) * nj
    slab = wd.shape[0] // steps
    assert slab * steps == wd.shape[0] and slab % 16 == 0
    return pl.pallas_call(
        _gateup_kernel,
        out_shape=(jax.ShapeDtypeStruct((m, f), BF16), jax.ShapeDtypeStruct(wd.shape, BF16)),
        grid=(m // tm, nj),
        in_specs=[pl.BlockSpec((tm, d), lambda i, j: (i, 0), pipeline_mode=pl.Buffered(1)),
                  pl.BlockSpec((d, tn), lambda i, j: (0, j)),
                  pl.BlockSpec((d, tn), lambda i, j: (0, j)),
                  pl.BlockSpec((slab, wd.shape[1]), lambda i, j: (i * nj + j, 0))],
        out_specs=(pl.BlockSpec((tm, tn), lambda i, j: (i, j)),
                   pl.BlockSpec((slab, wd.shape[1]), lambda i, j: (i * nj + j, 0))),
        compiler_params=pltpu.CompilerParams(
            dimension_semantics=("arbitrary", "arbitrary"),
            vmem_limit_bytes=_vmem_limit(2 * d * tn * 4 + tm * tn * 2 + slab * wd.shape[1] * 6,
                                         tm * d * 2 + 2 * d * tn * 2 + 4 * tm * tn * 4)),
        name="swiglu_up",
    )(h, wg, wu, wd)


def _deferred_rows(a_ref, w_ref, acc_ref, *, ntiles, nk, last, nchunks, epilogue):
    i = pl.program_id(0)
    k = pl.program_id(1)
    tk = a_ref.shape[1]
    cur = acc_ref.at[i % 2]
    prev = acc_ref.at[(i + 1) % 2]
    rc = acc_ref.shape[1] // nchunks
    full_end = nk - (last < tk)
    assert nchunks <= full_end
    mm = i < ntiles
    epi = (i >= 1) & (k < nchunks)

    def run_epilogue():
        epilogue(prev[pl.ds(pl.multiple_of(k * rc, rc), rc), :])

    @pl.when(mm & (k == 0) & (i == 0))
    def _():
        cur[...] = _dot(a_ref[...], w_ref[...])

    @pl.when(mm & (k == 0) & (i >= 1))
    def _():
        run_epilogue()
        cur[...] = _dot(a_ref[...], w_ref[...])

    @pl.when(mm & (k > 0) & (k < full_end) & epi)
    def _():
        run_epilogue()
        cur[...] += _dot(a_ref[...], w_ref[...])

    @pl.when(mm & (k > 0) & (k < full_end) & jnp.logical_not(epi))
    def _():
        cur[...] += _dot(a_ref[...], w_ref[...])

    if last < tk:
        @pl.when(mm & (k == nk - 1))
        def _():
            cur[...] += _dot(a_ref[:, :last], w_ref[:last, :])

    @pl.when(jnp.logical_not(mm) & (k < nchunks))
    def _():
        run_epilogue()


def _deferred_specs(m, kdim, d):
    tm = _tile(m, 1024, 16)
    tk = 512
    ntiles = m // tm
    nk = pl.cdiv(kdim, tk)
    last = kdim - (nk - 1) * tk
    assert last % V7X_LANES == 0 and nk >= 2
    nchunks = 1 << ((nk - (last < tk)).bit_length() - 1)
    nchunks = min(nchunks, tm // 8)
    rc = tm // nchunks
    a_spec = pl.BlockSpec((tm, tk), lambda i, k: (jnp.minimum(i, ntiles - 1), jnp.where(i < ntiles, k, nk - 1)))
    w_spec = pl.BlockSpec((tk, d), lambda i, k: (jnp.where(i < ntiles, k, nk - 1), 0))
    chunk = lambda i, k: (jnp.where(i >= 1, (i - 1) * nchunks + jnp.minimum(k, nchunks - 1), 0), 0)
    return tm, tk, ntiles, nk, last, nchunks, rc, a_spec, w_spec, chunk


def _down_kernel(a_ref, w_ref, x_ref, gpost_ref, gnext_ref, o_ref, h_ref, acc_ref, *, scale, **tiling):
    def epilogue(f):
        xn = x_ref[...] + scale * _rms(f, gpost_ref[...])
        o_ref[...] = xn
        h_ref[...] = _rms(xn, gnext_ref[...]).astype(h_ref.dtype)

    _deferred_rows(a_ref, w_ref, acc_ref, epilogue=epilogue, **tiling)


def _down(a, w, x, gpost, gnext, scale):
    m, kdim = a.shape
    d = w.shape[1]
    tm, tk, ntiles, nk, last, nchunks, rc, a_spec, w_spec, chunk = _deferred_specs(m, kdim, d)
    return pl.pallas_call(
        functools.partial(_down_kernel, scale=scale, ntiles=ntiles, nk=nk, last=last, nchunks=nchunks),
        out_shape=(jax.ShapeDtypeStruct((m, d), F32), jax.ShapeDtypeStruct((m, d), BF16)),
        grid=(ntiles + 1, nk),
        in_specs=[a_spec, w_spec,
                  pl.BlockSpec((rc, d), chunk),
                  pl.BlockSpec((1, d), lambda i, k: (0, 0)),
                  pl.BlockSpec((1, d), lambda i, k: (0, 0))],
        out_specs=(pl.BlockSpec((rc, d), chunk), pl.BlockSpec((rc, d), chunk)),
        scratch_shapes=[pltpu.VMEM((2, tm, d), F32)],
        compiler_params=pltpu.CompilerParams(
            dimension_semantics=("arbitrary", "arbitrary"),
            vmem_limit_bytes=_vmem_limit(tm * tk * 2 + tk * d * 2 + rc * d * 10, 2 * tm * d * 4 + 6 * rc * d * 4)),
        name="down_norm_residual",
    )(a, w, x, gpost, gnext)


def _ple_kernel(a_ref, w_ref, x_ref, p_ref, wp_ref, gpost_ref, o_ref, acc_ref, **tiling):
    def epilogue(f):
        proj = _dot(p_ref[...], wp_ref[...])
        o_ref[...] = x_ref[...] + _rms(proj * jax.nn.sigmoid(f), gpost_ref[...])

    _deferred_rows(a_ref, w_ref, acc_ref, epilogue=epilogue, **tiling)


def _ple(a, w, x, p, wp, gpost):
    m, kdim = a.shape
    d = w.shape[1]
    pd = p.shape[1]
    tm, tk, ntiles, nk, last, nchunks, rc, a_spec, w_spec, chunk = _deferred_specs(m, kdim, d)
    return pl.pallas_call(
        functools.partial(_ple_kernel, ntiles=ntiles, nk=nk, last=last, nchunks=nchunks),
        out_shape=jax.ShapeDtypeStruct((m, d), F32),
        grid=(ntiles + 1, nk),
        in_specs=[a_spec, w_spec,
                  pl.BlockSpec((rc, d), chunk),
                  pl.BlockSpec((rc, pd), chunk),
                  pl.BlockSpec((pd, d), lambda i, k: (0, 0)),
                  pl.BlockSpec((1, d), lambda i, k: (0, 0))],
        out_specs=pl.BlockSpec((rc, d), chunk),
        scratch_shapes=[pltpu.VMEM((2, tm, d), F32)],
        compiler_params=pltpu.CompilerParams(
            dimension_semantics=("arbitrary", "arbitrary"),
            vmem_limit_bytes=_vmem_limit(tm * tk * 2 + tk * d * 2 + rc * d * 8 + rc * pd * 2 + pd * d * 2,
                                         2 * tm * d * 4 + 6 * rc * d * 4)),
        name="ple_gate_norm_residual",
    )(a, w, x, p, wp, gpost)


def _qk_kernel(h_ref, w_ref, cos_ref, sa_ref, sb_ref, o_ref, *, q_tiles, qscale):
    j = pl.program_id(1)
    w = w_ref[...].astype(BF16)
    sc = jnp.where(j < q_tiles, qscale, 1.0).astype(F32)
    half = h_ref.shape[0] // 2
    for r in range(2):
        rows = pl.ds(r * half, half)
        z = _dot(h_ref[rows, :], w)
        c, sa, sb = cos_ref[rows, :], sa_ref[rows, :], sb_ref[rows, :]
        outs = []
        for g in range(z.shape[1] // V7X_LANES):
            zg = z[:, g * V7X_LANES:(g + 1) * V7X_LANES]
            rg = zg * c + pltpu.roll(zg, V7X_LANES - 16, 1) * sa + pltpu.roll(zg, 16, 1) * sb
            outs.append(rg * sc)
        o_ref[rows, :] = jnp.concatenate(outs, axis=1).astype(o_ref.dtype)


def _qk_proj(h, w, cos_t, sa_t, sb_t, *, seq, qk_cols, qscale):
    m, d = h.shape
    n = 2 * qk_cols
    tm = _tile(seq, 1024, 16)
    tn = _tile(qk_cols, 512, V7X_LANES)
    tpb = seq // tm
    return pl.pallas_call(
        functools.partial(_qk_kernel, q_tiles=qk_cols // tn, qscale=qscale),
        out_shape=jax.ShapeDtypeStruct((m, n), BF16),
        grid=(m // tm, n // tn),
        in_specs=[pl.BlockSpec((tm, d), lambda i, j: (i, 0)),
                  pl.BlockSpec((d, tn), lambda i, j: (0, j)),
                  pl.BlockSpec((tm, V7X_LANES), lambda i, j: (i % tpb, 0)),
                  pl.BlockSpec((tm, V7X_LANES), lambda i, j: (i % tpb, 0)),
                  pl.BlockSpec((tm, V7X_LANES), lambda i, j: (i % tpb, 0))],
        out_specs=pl.BlockSpec((tm, tn), lambda i, j: (i, j)),
        compiler_params=pltpu.CompilerParams(
            dimension_semantics=("arbitrary", "arbitrary"),
            vmem_limit_bytes=_vmem_limit(tm * d * 2 + d * tn * 4 + tm * tn * 2 + 3 * tm * V7X_LANES * 4,
                                         d * tn * 2 + 5 * tm * tn * 4)),
        name="qk_proj_rotary",
    )(h, w, cos_t, sa_t, sb_t)


def _vt_proj_kernel(wt_ref, h_ref, o_ref):
    o_ref[...] = _dot_nt(wt_ref[...], h_ref[...]).astype(o_ref.dtype)


def _vt_proj(h, wt):
    m, d = h.shape
    n = wt.shape[0]
    tm = _tile(m, 1024, V7X_LANES)
    tn = _tile(n, 512, 16)
    return pl.pallas_call(
        _vt_proj_kernel,
        out_shape=jax.ShapeDtypeStruct((n, m), BF16),
        grid=(m // tm, n // tn),
        in_specs=[pl.BlockSpec((tn, d), lambda i, j: (j, 0)),
                  pl.BlockSpec((tm, d), lambda i, j: (i, 0))],
        out_specs=pl.BlockSpec((tn, tm), lambda i, j: (j, i)),
        compiler_params=pltpu.CompilerParams(
            dimension_semantics=("arbitrary", "arbitrary"),
            vmem_limit_bytes=_vmem_limit(tm * d * 2 + d * tn * 2 + tm * tn * 2, 2 * tm * tn * 4)),
        name="v_proj_transposed",
    )(wt, h)


def _plain_proj_kernel(h_ref, w_ref, o_ref):
    o_ref[...] = _dot(h_ref[...], w_ref[...].astype(BF16)).astype(o_ref.dtype)


def _plain_proj(h, w, col0, n, out_dtype):
    m, d = h.shape
    tm = _tile(m, 1024, 16)
    tn = _tile(math.gcd(n, col0) if col0 else n, 512, V7X_LANES)
    j0 = col0 // tn
    return pl.pallas_call(
        _plain_proj_kernel,
        out_shape=jax.ShapeDtypeStruct((m, n), out_dtype),
        grid=(m // tm, n // tn),
        in_specs=[pl.BlockSpec((tm, d), lambda i, j: (i, 0)),
                  pl.BlockSpec((d, tn), lambda i, j: (0, j0 + j))],
        out_specs=pl.BlockSpec((tm, tn), lambda i, j: (i, j)),
        compiler_params=pltpu.CompilerParams(
            dimension_semantics=("arbitrary", "arbitrary"),
            vmem_limit_bytes=_vmem_limit(tm * d * 2 + d * tn * 4 + tm * tn * 4, d * tn * 2 + 2 * tm * tn * 4)),
        name="rwkv_proj",
    )(h, w)


def _attn_kernel(q_ref, k_ref, vt_ref, lq1_ref, lk1_ref, lq2_ref, lk2_ref, sg_ref, o_ref,
                 m_ref, l_ref, acc_ref, *, tq, hd, lambda_init):
    qi = pl.program_id(2)
    qs = [q_ref[h * tq:(h + 1) * tq, :] for h in range(2)]
    m_ref[...] = jnp.full(m_ref.shape, -jnp.inf, F32)
    l_ref[...] = jnp.zeros(l_ref.shape, F32)
    acc_ref[...] = jnp.zeros(acc_ref.shape, F32)

    def tile(start, work):
        kt = k_ref[pl.ds(start, tq), :]
        vt = vt_ref[:, pl.ds(start, tq)]
        chains = [(h, c, diag) for h, diag in work for c in range(2)]
        ss = [_dot_nt(kt[:, c * hd:(c + 1) * hd], qs[h][:, c * hd:(c + 1) * hd]) for h, c, _ in chains]
        if any(diag for _, _, diag in chains):
            sh = ATTN_CHUNK.bit_length() - 1
            ck = lax.shift_right_logical(lax.broadcasted_iota(jnp.int32, ss[0].shape, 0), sh)
            rq = lax.shift_right_logical(lax.broadcasted_iota(jnp.int32, ss[0].shape, 1), sh)
            ss = [jnp.where(ck <= rq, s, -jnp.inf) if diag else s for s, (_, _, diag) in zip(ss, chains)]
        m_olds = [m_ref[h, c] for h, c, _ in chains]
        m_news = [jnp.maximum(m_old, jnp.max(s, axis=0, keepdims=True)) for m_old, s in zip(m_olds, ss)]
        ps = [jnp.exp2(s - m_new) for s, m_new in zip(ss, m_news)]
        alphas = [jnp.exp2(m_old - m_new) for m_old, m_new in zip(m_olds, m_news)]
        pvs = [_dot(vt, p.astype(BF16)) for p in ps]
        for (h, c, _), p, alpha, pv, m_new in zip(chains, ps, alphas, pvs, m_news):
            l_ref[h, c] = alpha * l_ref[h, c] + jnp.sum(p, axis=0, keepdims=True)
            acc_ref[h, c] = alpha * acc_ref[h, c] + pv
            m_ref[h, c] = m_new

    def body(j, carry):
        tile(pl.multiple_of(j * tq, tq), [(0, False), (1, False)])
        return carry

    lax.fori_loop(0, 2 * qi, body, 0)
    tile(pl.multiple_of(2 * qi * tq, tq), [(0, True), (1, False)])
    tile(pl.multiple_of((2 * qi + 1) * tq, tq), [(1, True)])

    lam = (jnp.exp(jnp.sum(lq1_ref[...] * lk1_ref[...], axis=-1, keepdims=True))
           - jnp.exp(jnp.sum(lq2_ref[...] * lk2_ref[...], axis=-1, keepdims=True)) + lambda_init)
    for h in range(2):
        o = acc_ref[h, 0] / l_ref[h, 0] - lam * (acc_ref[h, 1] / l_ref[h, 1])
        o = o * lax.rsqrt(jnp.mean(o * o, axis=0, keepdims=True) + RMS_EPS) * sg_ref[...] * (1.0 - lambda_init)
        o_ref[h * tq:(h + 1) * tq, :] = o.T.astype(o_ref.dtype)


def _diff_attention(zqk, vt, lq1, lk1, lq2, lk2, subln_g, *, batch, seq, heads, hd, out_cols, lambda_init):
    m = zqk.shape[0]
    vd = 2 * hd
    tq = _tile(seq // 2, 512, V7X_LANES)
    nq = seq // (2 * tq)
    vec = pl.BlockSpec((1, hd), lambda b, h, i: (0, 0))
    return pl.pallas_call(
        functools.partial(_attn_kernel, tq=tq, hd=hd, lambda_init=lambda_init),
        out_shape=jax.ShapeDtypeStruct((m, out_cols), BF16),
        grid=(batch, heads, nq),
        in_specs=[pl.BlockSpec((2 * tq, vd), lambda b, h, i: (b * nq + i, h)),
                  pl.BlockSpec((seq, vd), lambda b, h, i: (b, heads + h)),
                  pl.BlockSpec((vd, seq), lambda b, h, i: (h, b)),
                  vec, vec, vec, vec,
                  pl.BlockSpec((vd, 1), lambda b, h, i: (0, 0))],
        out_specs=pl.BlockSpec((2 * tq, vd), lambda b, h, i: (b * nq + i, h)),
        scratch_shapes=[pltpu.VMEM((2, 2, 1, tq), F32), pltpu.VMEM((2, 2, 1, tq), F32),
                        pltpu.VMEM((2, 2, vd, tq), F32)],
        compiler_params=pltpu.CompilerParams(
            dimension_semantics=("arbitrary", "arbitrary", "arbitrary"),
            vmem_limit_bytes=_vmem_limit(4 * tq * vd * 2 + 2 * seq * vd * 2,
                                         4 * tq * vd * 4 + 20 * tq * tq * 4)),
        name="diff_attention",
    )(zqk, zqk, vt, lq1, lk1, lq2, lk2, subln_g)


def _split3_bf16(x):
    h1 = x.astype(BF16)
    r1 = x - h1.astype(F32)
    h2 = r1.astype(BF16)
    h3 = (r1 - h2.astype(F32)).astype(BF16)
    return h1, h2, h3


def _rwkv_kernel(z_ref, zlc_ref, zln_ref, mu_ref, mul_ref, w0_ref, w2_ref, a0_ref, a2_ref, g2_ref,
                 kk_ref, ka_ref, rk_ref, lnw_ref, lnb_ref, alias_ref, o_ref,
                 prev_ref, prevl_ref, st_ref, *side_refs, rw, hn, nc):
    del alias_ref
    c = pl.program_id(1)
    g = pl.program_id(0) * nc + c
    lt = z_ref.shape[0]
    ln = V7X_LANES
    sides = (side_refs[:4], side_refs[4:])

    @pl.when(c == 0)
    def _():
        prev_ref[...] = jnp.zeros(prev_ref.shape, F32)
        st_ref[...] = jnp.zeros(st_ref.shape, F32)

    def shifted(zt, prev_row, mu):
        row = lax.broadcasted_iota(jnp.int32, zt.shape, 0)
        zp = jnp.where(row == 0, prev_row, pltpu.roll(zt, 1, 0))
        return zt + (zp - zt) * mu

    def side_paths(zl, prev_row, dst):
        lw_ref, cum_ref, as_ref, gt_ref = dst
        zlm = shifted(zl, prev_row, mul_ref[...])
        dl = w2_ref.shape[0]
        al = a2_ref.shape[0]
        wpre = w0_ref[...] + _dot(jnp.tanh(zlm[:, :dl]).astype(BF16), w2_ref[...])
        softplus = jnp.maximum(-wpre, 0.0) + jnp.log(1.0 + jnp.exp(-jnp.abs(wpre)))
        lw = -jnp.exp(-softplus - 0.5)
        lw_ref[...] = lw
        tri = (lax.broadcasted_iota(jnp.int32, (lt, lt), 0)
               >= lax.broadcasted_iota(jnp.int32, (lt, lt), 1)).astype(BF16)
        h1, h2, h3 = _split3_bf16(lw)
        cum_ref[...] = _dot(tri, h1) + _dot(tri, h2) + _dot(tri, h3)
        as_ref[...] = jax.nn.sigmoid(a0_ref[...] + _dot(zlm[:, dl:dl + al].astype(BF16), a2_ref[...]))
        gt_ref[...] = _dot(jax.nn.sigmoid(zlm[:, dl + al:]).astype(BF16), g2_ref[...])

    @pl.when(g == 0)
    def _():
        zl = zlc_ref[...]
        side_paths(zl, jnp.zeros((1, zl.shape[1]), F32), sides[0])
        prevl_ref[0:1, :] = zl[lt - 1:lt, :]

    lane = lax.broadcasted_iota(jnp.int32, (lt, ln), 1)
    first = lane < hn
    r2 = lax.broadcasted_iota(jnp.int32, (2 * hn, 2 * hn), 0)
    c2 = lax.broadcasted_iota(jnp.int32, (2 * hn, 2 * hn), 1)
    hsh = hn.bit_length() - 1
    same = lax.shift_right_logical(r2, hsh) == lax.shift_right_logical(c2, hsh)
    strict = same & (r2 > c2)
    incl = same & (r2 >= c2)
    eye = (r2 == c2).astype(F32)
    zeros = jnp.zeros((2 * hn, ln), F32)

    def headsum(x):
        s0 = jnp.sum(jnp.where(first, x, 0.0), axis=-1, keepdims=True)
        s1 = jnp.sum(jnp.where(first, 0.0, x), axis=-1, keepdims=True)
        return jnp.where(first, s0, s1)

    def bd(x):
        return jnp.concatenate([jnp.where(first, x, 0.0), jnp.where(first, 0.0, x)], axis=0)

    def fold(x):
        return x[:hn] + x[hn:]

    def mixed(off):
        sl = pl.ds(off, ln)
        return shifted(z_ref[:, sl], prev_ref[0:1, sl], mu_ref[:, sl])

    def pairs(offs, src):
        lw_ref, cum_ref, as_ref, gt_ref = src
        n2 = 2 * hn
        sls = [pl.ds(off, ln) for off in offs]
        rs = [mixed(off) for off in offs]
        ks = [mixed(rw + off) for off in offs]
        vs = [mixed(2 * rw + off) for off in offs]
        asgs = [as_ref[:, sl] for sl in sls]
        kks = [k * kk_ref[:, sl] for k, sl in zip(ks, sls)]
        kks = [kk / jnp.maximum(jnp.sqrt(headsum(kk * kk)), 1e-12) for kk in kks]
        kns = [k * (1.0 + (asg - 1.0) * ka_ref[:, sl]) for k, asg, sl in zip(ks, asgs, sls)]
        bns = [kk * asg for kk, asg in zip(kks, asgs)]

        lws = [lw_ref[:, sl] for sl in sls]
        cums = [cum_ref[:, sl] for sl in sls]
        tots = [cum[lt - 1:lt, :] for cum in cums]
        e_invs = [jnp.exp(-cum) for cum in cums]
        ats = [bd(-kk * jnp.exp(cum - lw)) for kk, cum, lw in zip(kks, cums, lws)]
        rts = [bd(r * jnp.exp(cum)) for r, cum in zip(rs, cums)]
        vbs = [bd(v) for v in vs]

        fulls = [_dot_nt(jnp.concatenate([at, rt], axis=0).astype(BF16),
                         jnp.concatenate([bd(bn * e), bd(kn * e)], axis=0).astype(BF16))
                 for at, rt, bn, kn, e in zip(ats, rts, bns, kns, e_invs)]
        a_abs = [jnp.where(strict, f[:n2, :n2], 0.0) for f in fulls]
        a_aks = [jnp.where(strict, f[:n2, n2:], 0.0) for f in fulls]
        a_rs = [jnp.concatenate([jnp.where(incl, f[n2:, :n2], 0.0), jnp.where(incl, f[n2:, n2:], 0.0)],
                                axis=1).astype(BF16) for f in fulls]

        akvs = [_dot(a_ak.astype(BF16), vb.astype(BF16)) for a_ak, vb in zip(a_aks, vbs)]
        zqs = [jnp.concatenate([at, akv], axis=1) for at, akv in zip(ats, akvs)]
        apows = a_abs
        steps = max(1, (lt - 1).bit_length())
        for it in range(steps):
            if it + 1 < steps:
                aws = [_dot(ap.astype(BF16), jnp.concatenate([zq, ap], axis=1).astype(BF16))
                       for ap, zq in zip(apows, zqs)]
                zqs = [zq + aw[:, :2 * ln] for zq, aw in zip(zqs, aws)]
                apows = [aw[:, 2 * ln:] for aw in aws]
            else:
                zqs = [zq + _dot(ap.astype(BF16), zq.astype(BF16)) for ap, zq in zip(apows, zqs)]

        rhs2s = [jnp.concatenate([zq, jnp.concatenate([zeros, vb], axis=1)], axis=0).astype(BF16)
                 for zq, vb in zip(zqs, vbs)]
        rys = [_dot(a_r, rhs2) for a_r, rhs2 in zip(a_rs, rhs2s)]
        e_ends = [jnp.exp(tot - cum) for tot, cum in zip(tots, cums)]
        ghs = [_dot_tn(jnp.concatenate([bd(bn * e), bd(kn * e)], axis=0).astype(BF16), rhs2)
               for bn, kn, e, rhs2 in zip(bns, kns, e_ends, rhs2s)]
        lhs4s = [jnp.concatenate([fold(rt + ry[:, :ln]), fold(gh[:, :ln] + eye * jnp.exp(tot))],
                                 axis=0).astype(BF16)
                 for rt, ry, gh, tot in zip(rts, rys, ghs, tots)]
        outs = [_dot(lhs4, bd(st_ref[:, sl]).astype(BF16)) for lhs4, sl in zip(lhs4s, sls)]
        for sl, out, ry, gh, r, kn, v in zip(sls, outs, rys, ghs, rs, kns, vs):
            y = out[:hn] + fold(ry[:, ln:])
            st_ref[:, sl] = out[hn:] + fold(gh[:, ln:])
            mean = headsum(y) * (1.0 / hn)
            yc = y - mean
            var = headsum(yc * yc) * (1.0 / hn)
            yn = yc * lax.rsqrt(var + RW_LN_EPS) * lnw_ref[:, sl] + lnb_ref[:, sl]
            bonus = headsum(r * kn * rk_ref[:, sl]) * v
            o_ref[:, sl] = ((yn + bonus) * gt_ref[:, sl]).astype(o_ref.dtype)

    def step(cur, nxt):
        zln = zln_ref[...]
        prev_row = jnp.where(c == nc - 1, 0.0, prevl_ref[0:1, :])
        side_paths(zln, prev_row, nxt)
        prevl_ref[0:1, :] = zln[lt - 1:lt, :]
        pairs([u * ln for u in range(rw // ln)], cur)

    for parity in range(2):
        @pl.when(g % 2 == parity)
        def _():
            step(sides[parity], sides[1 - parity])

    prev_ref[0:1, :] = z_ref[lt - 1:lt, :]


def _rwkv(zr, zl, omix, mu, mul, w0, w2, a0, a2, g2, k_k, k_a, r_k, ln_w, ln_b, *, batch, seq, rw, hn):
    m, zc = zr.shape
    lc = zl.shape[1]
    assert hn * 2 == V7X_LANES and RW_CHUNK == hn and seq % RW_CHUNK == 0 and zc == 3 * rw
    lt = RW_CHUNK
    nc = seq // lt
    nsteps = batch * nc
    ocol = omix.shape[1] // rw - 1
    assert (ocol + 1) * rw == omix.shape[1]
    row = lambda n: pl.BlockSpec((1, n), lambda b, c: (0, 0))
    full = lambda a: pl.BlockSpec(a.shape, lambda b, c: (0, 0))
    return pl.pallas_call(
        functools.partial(_rwkv_kernel, rw=rw, hn=hn, nc=nc),
        out_shape=jax.ShapeDtypeStruct(omix.shape, omix.dtype),
        grid=(batch, nc),
        in_specs=[pl.BlockSpec((lt, zc), lambda b, c: (b * nc + c, 0)),
                  pl.BlockSpec((lt, lc), lambda b, c: (b * nc + c, 0)),
                  pl.BlockSpec((lt, lc), lambda b, c: (jnp.minimum(b * nc + c + 1, nsteps - 1), 0)),
                  row(zc), row(lc), row(rw), full(w2), row(rw), full(a2), full(g2),
                  row(rw), row(rw), row(rw), row(rw), row(rw),
                  pl.BlockSpec(memory_space=pl.ANY)],
        out_specs=pl.BlockSpec((lt, rw), lambda b, c: (b * nc + c, ocol)),
        scratch_shapes=[pltpu.VMEM((8, zc), F32), pltpu.VMEM((8, lc), F32), pltpu.VMEM((hn, rw), F32)]
                       + [pltpu.VMEM((lt, rw), F32)] * 8,
        input_output_aliases={15: 0},
        compiler_params=pltpu.CompilerParams(
            dimension_semantics=("arbitrary", "arbitrary"),
            vmem_limit_bytes=_vmem_limit(lt * (zc + 2 * lc) * 4 + (w2.size + a2.size + g2.size) * 2 + lt * rw * 2,
                                         8 * (zc + lc) * 4 + 9 * lt * rw * 4 + (8 << 20))),
        name="rwkv7_chunked",
    )(zr, zl, zl, mu, mul, w0, w2, a0, a2, g2, k_k, k_a, r_k, ln_w, ln_b, omix)


def _pad_to(a, axis, n):
    pad = n - a.shape[axis]
    if pad == 0:
        return a
    widths = [(0, 0)] * a.ndim
    widths[axis] = (0, pad)
    return jnp.pad(a, widths)


def _ffn(h, x, w_gate, w_up, w_down, gpost, gnext):
    act, wd = _gateup(h, w_gate, w_up, w_down)
    return _down(act, wd, x, gpost, gnext, 0.5)


def kernel(x, p, ffn1_pre_g, ffn1_w_gate, ffn1_w_up, ffn1_w_down, ffn1_post_g, mix_pre_g, w_in, diff_lambda_q1, diff_lambda_k1, diff_lambda_q2, diff_lambda_k2, diff_subln_g, rwkv_mu, rwkv_w0, rwkv_w2, rwkv_a0, rwkv_a2, rwkv_g2, rwkv_k_k, rwkv_k_a, rwkv_r_k, rwkv_ln_w, rwkv_ln_b, w_out, mix_post_g, ffn2_pre_g, ffn2_w_gate, ffn2_w_up, ffn2_w_down, ffn2_post_g, ple_pre_g, ple_w_gate, ple_w_proj, ple_post_g):
    batch, seq, d = x.shape
    depth = p.shape[0]
    m = batch * seq
    hd = diff_subln_g.shape[-1] // 2
    rw = rwkv_w2.shape[-1]
    da = w_out.shape[1] - rw
    heads = da // (2 * hd)
    qk = heads * 2 * hd
    rheads, hn = rwkv_r_k.shape[1], rwkv_r_k.shape[2]
    dl, al, gl = rwkv_w2.shape[1], rwkv_a2.shape[1], rwkv_g2.shape[1]
    dlp, alp, glp = (_round_up(n, V7X_LANES) for n in (dl, al, gl))
    rot = hd // 4
    assert rheads * hn == rw and da == rw and w_in.shape[2] == 2 * qk + da + 3 * rw + dl + al + gl

    inv = ROPE_THETA ** (-jnp.arange(0, rot, 2, dtype=F32) / rot)
    ang = jnp.arange(seq).astype(F32)[:, None] * inv[None, :]
    half = rot // 2
    ones = jnp.ones((seq, hd - rot), F32)
    zeros_h = jnp.zeros((seq, half), F32)
    zeros_r = jnp.zeros((seq, hd - rot), F32)
    cos_t = jnp.concatenate([jnp.cos(ang), jnp.cos(ang), ones], axis=1)
    sa_t = jnp.concatenate([-jnp.sin(ang), zeros_h, zeros_r], axis=1)
    sb_t = jnp.concatenate([zeros_h, jnp.sin(ang), zeros_r], axis=1)
    assert half == 16 and hd == V7X_LANES

    row = lambda a: a.reshape(1, -1).astype(F32)
    xf = x.reshape(m, d)
    for i in range(depth):
        lambda_init = 0.8 - 0.6 * math.exp(-0.3 * i)

        h = _prenorm(xf, row(ffn1_pre_g[i]))
        xf, h = _ffn(h, xf, ffn1_w_gate[i], ffn1_w_up[i], ffn1_w_down[i],
                     row(ffn1_post_g[i]), row(mix_pre_g[i]))

        wi = w_in[i]
        c0 = 2 * qk + da + 3 * rw
        w_l = jnp.concatenate([_pad_to(wi[:, c0:c0 + dl], 1, dlp),
                               _pad_to(wi[:, c0 + dl:c0 + dl + al], 1, alp),
                               _pad_to(wi[:, c0 + dl + al:], 1, glp)], axis=1).astype(BF16)
        mu = rwkv_mu[i]
        mu_l = jnp.concatenate([_pad_to(mu[3 * rw:3 * rw + dl], 0, dlp),
                                _pad_to(mu[3 * rw + dl:3 * rw + dl + al], 0, alp),
                                _pad_to(mu[3 * rw + dl + al:], 0, glp)])
        zqk = _qk_proj(h, wi[:, :2 * qk].astype(BF16), cos_t, sa_t, sb_t, seq=seq, qk_cols=qk,
                       qscale=hd ** -0.5 * math.log2(math.e))
        vt = _vt_proj(h, wi[:, 2 * qk:2 * qk + da].astype(BF16).T)
        zr = _plain_proj(h, wi[:, 2 * qk + da:c0].astype(BF16), 0, 3 * rw, F32)
        zl = _plain_proj(h, w_l, 0, w_l.shape[1], F32)

        omix = _diff_attention(zqk, vt, row(diff_lambda_q1[i]), row(diff_lambda_k1[i]),
                               row(diff_lambda_q2[i]), row(diff_lambda_k2[i]),
                               diff_subln_g[i].reshape(-1, 1).astype(F32),
                               batch=batch, seq=seq, heads=heads, hd=hd, out_cols=da + rw,
                               lambda_init=lambda_init)
        omix = _rwkv(zr, zl, omix, row(mu[:3 * rw]), row(mu_l), row(rwkv_w0[i]),
                     _pad_to(rwkv_w2[i].astype(BF16), 0, dlp), row(rwkv_a0[i]),
                     _pad_to(rwkv_a2[i].astype(BF16), 0, alp), _pad_to(rwkv_g2[i].astype(BF16), 0, glp),
                     row(rwkv_k_k[i]), row(rwkv_k_a[i]), row(rwkv_r_k[i]),
                     row(rwkv_ln_w[i]), row(rwkv_ln_b[i]), batch=batch, seq=seq, rw=rw, hn=hn)

        xf, h = _down(omix, w_out[i].astype(BF16), xf, row(mix_post_g[i]), row(ffn2_pre_g[i]), 1.0)
        xf, h = _ffn(h, xf, ffn2_w_gate[i], ffn2_w_up[i], ffn2_w_down[i],
                     row(ffn2_post_g[i]), row(ple_pre_g[i]))
        xf = _ple(h, ple_w_gate[i].astype(BF16), xf, p[i].reshape(m, -1).astype(BF16),
                  ple_w_proj[i].astype(BF16), row(ple_post_g[i]))
    return xf.reshape(batch, seq, d)
```

```python
import functools
import math

import jax
import jax.numpy as jnp
from jax import lax
from jax.experimental import pallas as pl
from jax.experimental.pallas import tpu as pltpu

F32 = jnp.float32
BF16 = jnp.bfloat16

RMS_EPS = 1e-6
RW_LN_EPS = 64e-5
ATTN_CHUNK = 64
ROPE_THETA = 500000.0
RW_CHUNK = 64

V7X_LANES = 128
V7X_VMEM_CAP_BYTES = 58 * 1024 * 1024


def _tile(n, pref, align):
    t = (min(n, pref) // align) * align
    while t >= align:
        if n % t == 0:
            return t
        t -= align
    return n


def _round_up(n, m):
    return (n + m - 1) // m * m


def _vmem_limit(pipelined_bytes, resident_bytes):
    need = 2 * pipelined_bytes + resident_bytes + (4 << 20)
    return int(min(V7X_VMEM_CAP_BYTES, max(need, 16 << 20)))


def _rms(x, g, eps=RMS_EPS):
    return x * lax.rsqrt(jnp.mean(x * x, axis=-1, keepdims=True) + eps) * g


def _dot(a, b):
    return jnp.dot(a, b, preferred_element_type=F32)


def _dot_nt(a, b):
    return lax.dot_general(a, b, (((1,), (1,)), ((), ())), preferred_element_type=F32)


def _dot_tn(a, b):
    return lax.dot_general(a, b, (((0,), (0,)), ((), ())), preferred_element_type=F32)


def _prenorm_kernel(x_ref, g_ref, h_ref):
    h_ref[...] = _rms(x_ref[...], g_ref[...]).astype(h_ref.dtype)


def _prenorm(x, g):
    m, d = x.shape
    tm = _tile(m, 256, 8)
    return pl.pallas_call(
        _prenorm_kernel,
        out_shape=jax.ShapeDtypeStruct((m, d), BF16),
        grid=(m // tm,),
        in_specs=[pl.BlockSpec((tm, d), lambda i: (i, 0)),
                  pl.BlockSpec((1, d), lambda i: (0, 0))],
        out_specs=pl.BlockSpec((tm, d), lambda i: (i, 0)),
        compiler_params=pltpu.CompilerParams(
            dimension_semantics=("arbitrary",),
            vmem_limit_bytes=_vmem_limit(tm * d * 6, 3 * tm * d * 4)),
        name="prenorm",
    )(x, g)


def _gateup_kernel(h_ref, wg_ref, wu_ref, *refs):
    ncast = len(refs) // 2
    o_ref = refs[ncast]
    h = h_ref[...]
    g = _dot(h, wg_ref[...].astype(BF16))
    u = _dot(h, wu_ref[...].astype(BF16))
    o_ref[...] = (g * jax.nn.sigmoid(g) * u).astype(o_ref.dtype)
    for src_ref, dst_ref in zip(refs[:ncast], refs[ncast + 1:]):
        dst_ref[...] = src_ref[...].astype(dst_ref.dtype)


def _gateup(h, wg, wu, casts):
    m, d = h.shape
    f = wg.shape[1]
    tm = _tile(m, 2048, 16)
    tn = _tile(f, 256, V7X_LANES)
    nj = f // tn
    steps = (m // tm) * nj
    cast_specs = []
    for w in casts:
        slab = _round_up(pl.cdiv(w.shape[0], steps), 16)
        while w.shape[0] % slab:
            slab += 16
        last = w.shape[0] // slab - 1
        cast_specs.append(pl.BlockSpec((slab, w.shape[1]),
                                       lambda i, j, last=last: (jnp.minimum(i * nj + j, last), 0)))
    cast_bytes = sum(sp.block_shape[0] * sp.block_shape[1] * 6 for sp in cast_specs)
    res = pl.pallas_call(
        _gateup_kernel,
        out_shape=(jax.ShapeDtypeStruct((m, f), BF16), *(jax.ShapeDtypeStruct(w.shape, BF16) for w in casts)),
        grid=(m // tm, nj),
        in_specs=[pl.BlockSpec((tm, d), lambda i, j: (i, 0), pipeline_mode=pl.Buffered(1)),
                  pl.BlockSpec((d, tn), lambda i, j: (0, j)),
                  pl.BlockSpec((d, tn), lambda i, j: (0, j)),
                  *cast_specs],
        out_specs=(pl.BlockSpec((tm, tn), lambda i, j: (i, j)), *cast_specs),
        compiler_params=pltpu.CompilerParams(
            dimension_semantics=("arbitrary", "arbitrary"),
            vmem_limit_bytes=_vmem_limit(2 * d * tn * 4 + tm * tn * 2 + cast_bytes,
                                         tm * d * 2 + 2 * d * tn * 2 + 4 * tm * tn * 4)),
        name="swiglu_up",
    )(h, wg, wu, *casts)
    return res[0], res[1:]


def _deferred_rows(a_ref, w_ref, acc_ref, *, ntiles, nk, last, nchunks, epilogue):
    i = pl.program_id(0)
    k = pl.program_id(1)
    tk = a_ref.shape[1]
    cur = acc_ref.at[i % 2]
    prev = acc_ref.at[(i + 1) % 2]
    rc = acc_ref.shape[1] // nchunks
    full_end = nk - (last < tk)
    assert nchunks <= full_end
    mm = i < ntiles
    epi = (i >= 1) & (k < nchunks)

    def run_epilogue():
        epilogue(prev[pl.ds(pl.multiple_of(k * rc, rc), rc), :])

    @pl.when(mm & (k == 0) & (i == 0))
    def _():
        cur[...] = _dot(a_ref[...], w_ref[...])

    @pl.when(mm & (k == 0) & (i >= 1))
    def _():
        run_epilogue()
        cur[...] = _dot(a_ref[...], w_ref[...])

    @pl.when(mm & (k > 0) & (k < full_end) & epi)
    def _():
        run_epilogue()
        cur[...] += _dot(a_ref[...], w_ref[...])

    @pl.when(mm & (k > 0) & (k < full_end) & jnp.logical_not(epi))
    def _():
        cur[...] += _dot(a_ref[...], w_ref[...])

    if last < tk:
        @pl.when(mm & (k == nk - 1))
        def _():
            cur[...] += _dot(a_ref[:, :last], w_ref[:last, :])

    @pl.when(jnp.logical_not(mm) & (k < nchunks))
    def _():
        run_epilogue()


def _deferred_specs(m, kdim, d):
    tm = _tile(m, 1024, 16)
    tk = 512
    ntiles = m // tm
    nk = pl.cdiv(kdim, tk)
    last = kdim - (nk - 1) * tk
    assert last % V7X_LANES == 0 and nk >= 2
    nchunks = 1 << ((nk - (last < tk)).bit_length() - 1)
    nchunks = min(nchunks, tm // 8)
    rc = tm // nchunks
    a_spec = pl.BlockSpec((tm, tk), lambda i, k: (jnp.minimum(i, ntiles - 1), jnp.where(i < ntiles, k, nk - 1)))
    w_spec = pl.BlockSpec((tk, d), lambda i, k: (jnp.where(i < ntiles, k, nk - 1), 0))
    chunk = lambda i, k: (jnp.where(i >= 1, (i - 1) * nchunks + jnp.minimum(k, nchunks - 1), 0), 0)
    return tm, tk, ntiles, nk, last, nchunks, rc, a_spec, w_spec, chunk


def _down_kernel(a_ref, w_ref, x_ref, gpost_ref, gnext_ref, o_ref, h_ref, acc_ref, *, scale, **tiling):
    def epilogue(f):
        xn = x_ref[...] + scale * _rms(f, gpost_ref[...])
        o_ref[...] = xn
        h_ref[...] = _rms(xn, gnext_ref[...]).astype(h_ref.dtype)

    _deferred_rows(a_ref, w_ref, acc_ref, epilogue=epilogue, **tiling)


def _down(a, w, x, gpost, gnext, scale):
    m, kdim = a.shape
    d = w.shape[1]
    tm, tk, ntiles, nk, last, nchunks, rc, a_spec, w_spec, chunk = _deferred_specs(m, kdim, d)
    return pl.pallas_call(
        functools.partial(_down_kernel, scale=scale, ntiles=ntiles, nk=nk, last=last, nchunks=nchunks),
        out_shape=(jax.ShapeDtypeStruct((m, d), F32), jax.ShapeDtypeStruct((m, d), BF16)),
        grid=(ntiles + 1, nk),
        in_specs=[a_spec, w_spec,
                  pl.BlockSpec((rc, d), chunk),
                  pl.BlockSpec((1, d), lambda i, k: (0, 0)),
                  pl.BlockSpec((1, d), lambda i, k: (0, 0))],
        out_specs=(pl.BlockSpec((rc, d), chunk), pl.BlockSpec((rc, d), chunk)),
        scratch_shapes=[pltpu.VMEM((2, tm, d), F32)],
        compiler_params=pltpu.CompilerParams(
            dimension_semantics=("arbitrary", "arbitrary"),
            vmem_limit_bytes=_vmem_limit(tm * tk * 2 + tk * d * 2 + rc * d * 10, 2 * tm * d * 4 + 6 * rc * d * 4)),
        name="down_norm_residual",
    )(a, w, x, gpost, gnext)


def _ple_kernel(a_ref, w_ref, x_ref, p_ref, wp_ref, gpost_ref, o_ref, acc_ref, **tiling):
    def epilogue(f):
        proj = _dot(p_ref[...], wp_ref[...])
        o_ref[...] = x_ref[...] + _rms(proj * jax.nn.sigmoid(f), gpost_ref[...])

    _deferred_rows(a_ref, w_ref, acc_ref, epilogue=epilogue, **tiling)


def _ple(a, w, x, p, wp, gpost):
    m, kdim = a.shape
    d = w.shape[1]
    pd = p.shape[1]
    tm, tk, ntiles, nk, last, nchunks, rc, a_spec, w_spec, chunk = _deferred_specs(m, kdim, d)
    return pl.pallas_call(
        functools.partial(_ple_kernel, ntiles=ntiles, nk=nk, last=last, nchunks=nchunks),
        out_shape=jax.ShapeDtypeStruct((m, d), F32),
        grid=(ntiles + 1, nk),
        in_specs=[a_spec, w_spec,
                  pl.BlockSpec((rc, d), chunk),
                  pl.BlockSpec((rc, pd), chunk),
                  pl.BlockSpec((pd, d), lambda i, k: (0, 0)),
                  pl.BlockSpec((1, d), lambda i, k: (0, 0))],
        out_specs=pl.BlockSpec((rc, d), chunk),
        scratch_shapes=[pltpu.VMEM((2, tm, d), F32)],
        compiler_params=pltpu.CompilerParams(
            dimension_semantics=("arbitrary", "arbitrary"),
            vmem_limit_bytes=_vmem_limit(tm * tk * 2 + tk * d * 2 + rc * d * 8 + rc * pd * 2 + pd * d * 2,
                                         2 * tm * d * 4 + 6 * rc * d * 4)),
        name="ple_gate_norm_residual",
    )(a, w, x, p, wp, gpost)


def _qk_kernel(h_ref, w_ref, cos_ref, sa_ref, sb_ref, o_ref, *, q_tiles, qscale):
    j = pl.program_id(1)
    w = w_ref[...].astype(BF16)
    sc = jnp.where(j < q_tiles, qscale, 1.0).astype(F32)
    parts = 4
    part = h_ref.shape[0] // parts
    for r in range(parts):
        rows = pl.ds(r * part, part)
        z = _dot(h_ref[rows, :], w)
        c, sa, sb = cos_ref[rows, :], sa_ref[rows, :], sb_ref[rows, :]
        outs = []
        for g in range(z.shape[1] // V7X_LANES):
            zg = z[:, g * V7X_LANES:(g + 1) * V7X_LANES]
            rg = zg * c + pltpu.roll(zg, V7X_LANES - 16, 1) * sa + pltpu.roll(zg, 16, 1) * sb
            outs.append(rg * sc)
        o_ref[rows, :] = jnp.concatenate(outs, axis=1).astype(o_ref.dtype)


def _qk_proj(h, w, cos_t, sa_t, sb_t, *, seq, qk_cols, qscale):
    m, d = h.shape
    n = 2 * qk_cols
    tm = _tile(seq, 1024, 16)
    tn = _tile(qk_cols, 512, V7X_LANES)
    tpb = seq // tm
    return pl.pallas_call(
        functools.partial(_qk_kernel, q_tiles=qk_cols // tn, qscale=qscale),
        out_shape=jax.ShapeDtypeStruct((m, n), BF16),
        grid=(m // tm, n // tn),
        in_specs=[pl.BlockSpec((tm, d), lambda i, j: (i, 0)),
                  pl.BlockSpec((d, tn), lambda i, j: (0, j)),
                  pl.BlockSpec((tm, V7X_LANES), lambda i, j: (i % tpb, 0)),
                  pl.BlockSpec((tm, V7X_LANES), lambda i, j: (i % tpb, 0)),
                  pl.BlockSpec((tm, V7X_LANES), lambda i, j: (i % tpb, 0))],
        out_specs=pl.BlockSpec((tm, tn), lambda i, j: (i, j)),
        compiler_params=pltpu.CompilerParams(
            dimension_semantics=("arbitrary", "arbitrary"),
            vmem_limit_bytes=_vmem_limit(tm * d * 2 + d * tn * 4 + tm * tn * 2 + 3 * tm * V7X_LANES * 4,
                                         d * tn * 2 + 5 * tm * tn * 4)),
        name="qk_proj_rotary",
    )(h, w, cos_t, sa_t, sb_t)


def _vt_proj_kernel(wt_ref, h_ref, o_ref):
    o_ref[...] = _dot_nt(wt_ref[...], h_ref[...]).astype(o_ref.dtype)


def _vt_proj(h, wt):
    m, d = h.shape
    n = wt.shape[0]
    tm = _tile(m, 1024, V7X_LANES)
    tn = _tile(n, 512, 16)
    return pl.pallas_call(
        _vt_proj_kernel,
        out_shape=jax.ShapeDtypeStruct((n, m), BF16),
        grid=(m // tm, n // tn),
        in_specs=[pl.BlockSpec((tn, d), lambda i, j: (j, 0)),
                  pl.BlockSpec((tm, d), lambda i, j: (i, 0))],
        out_specs=pl.BlockSpec((tn, tm), lambda i, j: (j, i)),
        compiler_params=pltpu.CompilerParams(
            dimension_semantics=("arbitrary", "arbitrary"),
            vmem_limit_bytes=_vmem_limit(tm * d * 2 + d * tn * 2 + tm * tn * 2, 2 * tm * tn * 4)),
        name="v_proj_transposed",
    )(wt, h)


def _shift_proj_kernel(h_ref, w_ref, mu_ref, o_ref, carry_ref, *, tiles_per_seq):
    i = pl.program_id(0)
    j = pl.program_id(1)

    @pl.when(i == 0)
    def _():
        carry_ref[j] = jnp.zeros(carry_ref.shape[1:], F32)

    z = _dot(h_ref[...], w_ref[...])
    prev_row = jnp.where(i % tiles_per_seq == 0, 0.0, carry_ref[j, 0:1, :])
    row = lax.broadcasted_iota(jnp.int32, z.shape, 0)
    zp = jnp.where(row == 0, prev_row, pltpu.roll(z, 1, 0))
    carry_ref[j, 0:1, :] = z[z.shape[0] - 1:, :]
    o_ref[...] = z + (zp - z) * mu_ref[...]


def _shift_proj(h, w, mu, *, seq):
    m, d = h.shape
    n = w.shape[1]
    tm = _tile(seq, 1024, 16)
    tn = _tile(n, 512, V7X_LANES)
    return pl.pallas_call(
        functools.partial(_shift_proj_kernel, tiles_per_seq=seq // tm),
        out_shape=jax.ShapeDtypeStruct((m, n), F32),
        grid=(m // tm, n // tn),
        in_specs=[pl.BlockSpec((tm, d), lambda i, j: (i, 0)),
                  pl.BlockSpec((d, tn), lambda i, j: (0, j)),
                  pl.BlockSpec((1, tn), lambda i, j: (0, j))],
        out_specs=pl.BlockSpec((tm, tn), lambda i, j: (i, j)),
        scratch_shapes=[pltpu.VMEM((n // tn, 8, tn), F32)],
        compiler_params=pltpu.CompilerParams(
            dimension_semantics=("arbitrary", "arbitrary"),
            vmem_limit_bytes=_vmem_limit(tm * d * 2 + d * tn * 2 + tm * tn * 4, 8 * n * 4 + 4 * tm * tn * 4)),
        name="rwkv_proj_token_shift",
    )(h, w, mu)


def _attn_kernel(q_ref, k_ref, vt_ref, lq1_ref, lk1_ref, lq2_ref, lk2_ref, sg_ref, o_ref,
                 m_ref, l_ref, acc_ref, *, tq, hd, lambda_init):
    qi = pl.program_id(2)
    qs = [q_ref[h * tq:(h + 1) * tq, :] for h in range(2)]
    m_ref[...] = jnp.full(m_ref.shape, -jnp.inf, F32)
    l_ref[...] = jnp.zeros(l_ref.shape, F32)
    acc_ref[...] = jnp.zeros(acc_ref.shape, F32)

    def tile(start, work):
        kt = k_ref[pl.ds(start, tq), :]
        vt = vt_ref[:, pl.ds(start, tq)]
        chains = [(h, c, diag) for h, diag in work for c in range(2)]
        ss = [_dot_nt(kt[:, c * hd:(c + 1) * hd], qs[h][:, c * hd:(c + 1) * hd]) for h, c, _ in chains]
        if any(diag for _, _, diag in chains):
            sh = ATTN_CHUNK.bit_length() - 1
            ck = lax.shift_right_logical(lax.broadcasted_iota(jnp.int32, ss[0].shape, 0), sh)
            rq = lax.shift_right_logical(lax.broadcasted_iota(jnp.int32, ss[0].shape, 1), sh)
            ss = [jnp.where(ck <= rq, s, -jnp.inf) if diag else s for s, (_, _, diag) in zip(ss, chains)]
        m_olds = [m_ref[h, c] for h, c, _ in chains]
        m_news = [jnp.maximum(m_old, jnp.max(s, axis=0, keepdims=True)) for m_old, s in zip(m_olds, ss)]
        ps = [jnp.exp2(s - m_new) for s, m_new in zip(ss, m_news)]
        alphas = [jnp.exp2(m_old - m_new) for m_old, m_new in zip(m_olds, m_news)]
        pvs = [_dot(vt, p.astype(BF16)) for p in ps]
        for (h, c, _), p, alpha, pv, m_new in zip(chains, ps, alphas, pvs, m_news):
            l_ref[h, c] = alpha * l_ref[h, c] + jnp.sum(p, axis=0, keepdims=True)
            acc_ref[h, c] = alpha * acc_ref[h, c] + pv
            m_ref[h, c] = m_new

    def body(j, carry):
        tile(pl.multiple_of(j * tq, tq), [(0, False), (1, False)])
        return carry

    lax.fori_loop(0, 2 * qi, body, 0)
    tile(pl.multiple_of(2 * qi * tq, tq), [(0, True), (1, False)])
    tile(pl.multiple_of((2 * qi + 1) * tq, tq), [(1, True)])

    lam = (jnp.exp(jnp.sum(lq1_ref[...] * lk1_ref[...], axis=-1, keepdims=True))
           - jnp.exp(jnp.sum(lq2_ref[...] * lk2_ref[...], axis=-1, keepdims=True)) + lambda_init)
    for h in range(2):
        o = acc_ref[h, 0] / l_ref[h, 0] - lam * (acc_ref[h, 1] / l_ref[h, 1])
        o = o * lax.rsqrt(jnp.mean(o * o, axis=0, keepdims=True) + RMS_EPS) * sg_ref[...] * (1.0 - lambda_init)
        o_ref[h * tq:(h + 1) * tq, :] = o.T.astype(o_ref.dtype)


def _diff_attention(zqk, vt, lq1, lk1, lq2, lk2, subln_g, *, batch, seq, heads, hd, out_cols, lambda_init):
    m = zqk.shape[0]
    vd = 2 * hd
    tq = _tile(seq // 2, 512, V7X_LANES)
    nq = seq // (2 * tq)
    vec = pl.BlockSpec((1, hd), lambda b, h, i: (0, 0))
    return pl.pallas_call(
        functools.partial(_attn_kernel, tq=tq, hd=hd, lambda_init=lambda_init),
        out_shape=jax.ShapeDtypeStruct((m, out_cols), BF16),
        grid=(batch, heads, nq),
        in_specs=[pl.BlockSpec((2 * tq, vd), lambda b, h, i: (b * nq + i, h)),
                  pl.BlockSpec((seq, vd), lambda b, h, i: (b, heads + h)),
                  pl.BlockSpec((vd, seq), lambda b, h, i: (h, b)),
                  vec, vec, vec, vec,
                  pl.BlockSpec((vd, 1), lambda b, h, i: (0, 0))],
        out_specs=pl.BlockSpec((2 * tq, vd), lambda b, h, i: (b * nq + i, h)),
        scratch_shapes=[pltpu.VMEM((2, 2, 1, tq), F32), pltpu.VMEM((2, 2, 1, tq), F32),
                        pltpu.VMEM((2, 2, vd, tq), F32)],
        compiler_params=pltpu.CompilerParams(
            dimension_semantics=("arbitrary", "arbitrary", "arbitrary"),
            vmem_limit_bytes=_vmem_limit(4 * tq * vd * 2 + 2 * seq * vd * 2,
                                         4 * tq * vd * 4 + 20 * tq * tq * 4)),
        name="diff_attention",
    )(zqk, zqk, vt, lq1, lk1, lq2, lk2, subln_g)


def _split3_bf16(x):
    h1 = x.astype(BF16)
    r1 = x - h1.astype(F32)
    h2 = r1.astype(BF16)
    h3 = (r1 - h2.astype(F32)).astype(BF16)
    return h1, h2, h3


def _rwkv_kernel(z_ref, zlc_ref, zln_ref, w0_ref, w2_ref, a0_ref, a2_ref, g2_ref,
                 kk_ref, ka_ref, rk_ref, lnw_ref, lnb_ref, alias_ref, o_ref,
                 st_ref, *side_refs, rw, hn, nc):
    del alias_ref
    c = pl.program_id(1)
    g = pl.program_id(0) * nc + c
    lt = z_ref.shape[0]
    ln = V7X_LANES
    sides = (side_refs[:4], side_refs[4:])

    @pl.when(c == 0)
    def _():
        st_ref[...] = jnp.zeros(st_ref.shape, F32)

    def side_paths(zlm, dst):
        lw_ref, cum_ref, as_ref, gt_ref = dst
        dl = w2_ref.shape[0]
        al = a2_ref.shape[0]
        wpre = w0_ref[...] + _dot(jnp.tanh(zlm[:, :dl]).astype(BF16), w2_ref[...])
        softplus = jnp.maximum(-wpre, 0.0) + jnp.log(1.0 + jnp.exp(-jnp.abs(wpre)))
        lw = -jnp.exp(-softplus - 0.5)
        lw_ref[...] = lw
        tri = (lax.broadcasted_iota(jnp.int32, (lt, lt), 0)
               >= lax.broadcasted_iota(jnp.int32, (lt, lt), 1)).astype(BF16)
        h1, h2, h3 = _split3_bf16(lw)
        cum_ref[...] = _dot(tri, h1) + _dot(tri, h2) + _dot(tri, h3)
        as_ref[...] = jax.nn.sigmoid(a0_ref[...] + _dot(zlm[:, dl:dl + al].astype(BF16), a2_ref[...]))
        gt_ref[...] = _dot(jax.nn.sigmoid(zlm[:, dl + al:]).astype(BF16), g2_ref[...])

    @pl.when(g == 0)
    def _():
        side_paths(zlc_ref[...], sides[0])

    lane = lax.broadcasted_iota(jnp.int32, (lt, ln), 1)
    first = lane < hn
    r2 = lax.broadcasted_iota(jnp.int32, (2 * hn, 2 * hn), 0)
    c2 = lax.broadcasted_iota(jnp.int32, (2 * hn, 2 * hn), 1)
    hsh = hn.bit_length() - 1
    same = lax.shift_right_logical(r2, hsh) == lax.shift_right_logical(c2, hsh)
    strict = same & (r2 > c2)
    incl = same & (r2 >= c2)
    eye = (r2 == c2).astype(F32)
    zeros = jnp.zeros((2 * hn, ln), F32)

    def headsum(x):
        s0 = jnp.sum(jnp.where(first, x, 0.0), axis=-1, keepdims=True)
        s1 = jnp.sum(jnp.where(first, 0.0, x), axis=-1, keepdims=True)
        return jnp.where(first, s0, s1)

    def bd(x):
        return jnp.concatenate([jnp.where(first, x, 0.0), jnp.where(first, 0.0, x)], axis=0)

    def fold(x):
        return x[:hn] + x[hn:]

    def pairs(offs, src):
        lw_ref, cum_ref, as_ref, gt_ref = src
        n2 = 2 * hn
        sls = [pl.ds(off, ln) for off in offs]
        rs = [z_ref[:, pl.ds(off, ln)] for off in offs]
        ks = [z_ref[:, pl.ds(rw + off, ln)] for off in offs]
        vs = [z_ref[:, pl.ds(2 * rw + off, ln)] for off in offs]
        asgs = [as_ref[:, sl] for sl in sls]
        kks = [k * kk_ref[:, sl] for k, sl in zip(ks, sls)]
        kks = [kk / jnp.maximum(jnp.sqrt(headsum(kk * kk)), 1e-12) for kk in kks]
        kns = [k * (1.0 + (asg - 1.0) * ka_ref[:, sl]) for k, asg, sl in zip(ks, asgs, sls)]
        bns = [kk * asg for kk, asg in zip(kks, asgs)]

        lws = [lw_ref[:, sl] for sl in sls]
        cums = [cum_ref[:, sl] for sl in sls]
        tots = [cum[lt - 1:lt, :] for cum in cums]
        e_invs = [jnp.exp(-cum) for cum in cums]
        ats = [bd(-kk * jnp.exp(cum - lw)) for kk, cum, lw in zip(kks, cums, lws)]
        rts = [bd(r * jnp.exp(cum)) for r, cum in zip(rs, cums)]
        vbs = [bd(v) for v in vs]

        fulls = [_dot_nt(jnp.concatenate([at, rt], axis=0).astype(BF16),
                         jnp.concatenate([bd(bn * e), bd(kn * e)], axis=0).astype(BF16))
                 for at, rt, bn, kn, e in zip(ats, rts, bns, kns, e_invs)]
        a_abs = [jnp.where(strict, f[:n2, :n2], 0.0) for f in fulls]
        a_aks = [jnp.where(strict, f[:n2, n2:], 0.0) for f in fulls]
        a_rs = [jnp.concatenate([jnp.where(incl, f[n2:, :n2], 0.0), jnp.where(incl, f[n2:, n2:], 0.0)],
                                axis=1).astype(BF16) for f in fulls]

        akvs = [_dot(a_ak.astype(BF16), vb.astype(BF16)) for a_ak, vb in zip(a_aks, vbs)]
        zqs = [jnp.concatenate([at, akv], axis=1) for at, akv in zip(ats, akvs)]
        apows = a_abs
        steps = max(1, (lt - 1).bit_length())
        for it in range(steps):
            if it + 1 < steps:
                aws = [_dot(ap.astype(BF16), jnp.concatenate([zq, ap], axis=1).astype(BF16))
                       for ap, zq in zip(apows, zqs)]
                zqs = [zq + aw[:, :2 * ln] for zq, aw in zip(zqs, aws)]
                apows = [aw[:, 2 * ln:] for aw in aws]
            else:
                zqs = [zq + _dot(ap.astype(BF16), zq.astype(BF16)) for ap, zq in zip(apows, zqs)]

        rhs2s = [jnp.concatenate([zq, jnp.concatenate([zeros, vb], axis=1)], axis=0).astype(BF16)
                 for zq, vb in zip(zqs, vbs)]
        rys = [_dot(a_r, rhs2) for a_r, rhs2 in zip(a_rs, rhs2s)]
        e_ends = [jnp.exp(tot - cum) for tot, cum in zip(tots, cums)]
        ghs = [_dot_tn(jnp.concatenate([bd(bn * e), bd(kn * e)], axis=0).astype(BF16), rhs2)
               for bn, kn, e, rhs2 in zip(bns, kns, e_ends, rhs2s)]
        lhs4s = [jnp.concatenate([fold(rt + ry[:, :ln]), fold(gh[:, :ln] + eye * jnp.exp(tot))],
                                 axis=0).astype(BF16)
                 for rt, ry, gh, tot in zip(rts, rys, ghs, tots)]
        outs = [_dot(lhs4, bd(st_ref[:, sl]).astype(BF16)) for lhs4, sl in zip(lhs4s, sls)]
        for sl, out, ry, gh, r, kn, v in zip(sls, outs, rys, ghs, rs, kns, vs):
            y = out[:hn] + fold(ry[:, ln:])
            st_ref[:, sl] = out[hn:] + fold(gh[:, ln:])
            mean = headsum(y) * (1.0 / hn)
            yc = y - mean
            var = headsum(yc * yc) * (1.0 / hn)
            yn = yc * lax.rsqrt(var + RW_LN_EPS) * lnw_ref[:, sl] + lnb_ref[:, sl]
            bonus = headsum(r * kn * rk_ref[:, sl]) * v
            o_ref[:, sl] = ((yn + bonus) * gt_ref[:, sl]).astype(o_ref.dtype)

    def step(cur, nxt):
        side_paths(zln_ref[...], nxt)
        pairs([u * ln for u in range(rw // ln)], cur)

    for parity in range(2):
        @pl.when(g % 2 == parity)
        def _():
            step(sides[parity], sides[1 - parity])


def _rwkv(zr, zl, omix, w0, w2, a0, a2, g2, k_k, k_a, r_k, ln_w, ln_b, *, batch, seq, rw, hn):
    m, zc = zr.shape
    lc = zl.shape[1]
    assert hn * 2 == V7X_LANES and RW_CHUNK == hn and seq % RW_CHUNK == 0 and zc == 3 * rw
    lt = RW_CHUNK
    nc = seq // lt
    nsteps = batch * nc
    ocol = omix.shape[1] // rw - 1
    assert (ocol + 1) * rw == omix.shape[1]
    row = lambda n: pl.BlockSpec((1, n), lambda b, c: (0, 0))
    full = lambda a: pl.BlockSpec(a.shape, lambda b, c: (0, 0))
    return pl.pallas_call(
        functools.partial(_rwkv_kernel, rw=rw, hn=hn, nc=nc),
        out_shape=jax.ShapeDtypeStruct(omix.shape, omix.dtype),
        grid=(batch, nc),
        in_specs=[pl.BlockSpec((lt, zc), lambda b, c: (b * nc + c, 0)),
                  pl.BlockSpec((lt, lc), lambda b, c: (b * nc + c, 0)),
                  pl.BlockSpec((lt, lc), lambda b, c: (jnp.minimum(b * nc + c + 1, nsteps - 1), 0)),
                  row(rw), full(w2), row(rw), full(a2), full(g2),
                  row(rw), row(rw), row(rw), row(rw), row(rw),
                  pl.BlockSpec(memory_space=pl.ANY)],
        out_specs=pl.BlockSpec((lt, rw), lambda b, c: (b * nc + c, ocol)),
        scratch_shapes=[pltpu.VMEM((hn, rw), F32)] + [pltpu.VMEM((lt, rw), F32)] * 8,
        input_output_aliases={13: 0},
        compiler_params=pltpu.CompilerParams(
            dimension_semantics=("arbitrary", "arbitrary"),
            vmem_limit_bytes=_vmem_limit(lt * (zc + 2 * lc) * 4 + (w2.size + a2.size + g2.size) * 2 + lt * rw * 2,
                                         9 * lt * rw * 4 + (8 << 20))),
        name="rwkv7_chunked",
    )(zr, zl, zl, w0, w2, a0, a2, g2, k_k, k_a, r_k, ln_w, ln_b, omix)


def _pad_to(a, axis, n):
    pad = n - a.shape[axis]
    if pad == 0:
        return a
    widths = [(0, 0)] * a.ndim
    widths[axis] = (0, pad)
    return jnp.pad(a, widths)


def _ffn(h, x, w_gate, w_up, w_down, gpost, gnext, later_weight):
    act, (wd, later) = _gateup(h, w_gate, w_up, [w_down, later_weight])
    xn, hn = _down(act, wd, x, gpost, gnext, 0.5)
    return xn, hn, later


def kernel(x, p, ffn1_pre_g, ffn1_w_gate, ffn1_w_up, ffn1_w_down, ffn1_post_g, mix_pre_g, w_in, diff_lambda_q1, diff_lambda_k1, diff_lambda_q2, diff_lambda_k2, diff_subln_g, rwkv_mu, rwkv_w0, rwkv_w2, rwkv_a0, rwkv_a2, rwkv_g2, rwkv_k_k, rwkv_k_a, rwkv_r_k, rwkv_ln_w, rwkv_ln_b, w_out, mix_post_g, ffn2_pre_g, ffn2_w_gate, ffn2_w_up, ffn2_w_down, ffn2_post_g, ple_pre_g, ple_w_gate, ple_w_proj, ple_post_g):
    batch, seq, d = x.shape
    depth = p.shape[0]
    m = batch * seq
    hd = diff_subln_g.shape[-1] // 2
    rw = rwkv_w2.shape[-1]
    da = w_out.shape[1] - rw
    heads = da // (2 * hd)
    qk = heads * 2 * hd
    rheads, hn = rwkv_r_k.shape[1], rwkv_r_k.shape[2]
    dl, al, gl = rwkv_w2.shape[1], rwkv_a2.shape[1], rwkv_g2.shape[1]
    dlp, alp, glp = (_round_up(n, V7X_LANES) for n in (dl, al, gl))
    rot = hd // 4
    assert rheads * hn == rw and da == rw and w_in.shape[2] == 2 * qk + da + 3 * rw + dl + al + gl

    inv = ROPE_THETA ** (-jnp.arange(0, rot, 2, dtype=F32) / rot)
    ang = jnp.arange(seq).astype(F32)[:, None] * inv[None, :]
    half = rot // 2
    ones = jnp.ones((seq, hd - rot), F32)
    zeros_h = jnp.zeros((seq, half), F32)
    zeros_r = jnp.zeros((seq, hd - rot), F32)
    cos_t = jnp.concatenate([jnp.cos(ang), jnp.cos(ang), ones], axis=1)
    sa_t = jnp.concatenate([-jnp.sin(ang), zeros_h, zeros_r], axis=1)
    sb_t = jnp.concatenate([zeros_h, jnp.sin(ang), zeros_r], axis=1)
    assert half == 16 and hd == V7X_LANES

    row = lambda a: a.reshape(1, -1).astype(F32)
    xf = x.reshape(m, d)
    for i in range(depth):
        lambda_init = 0.8 - 0.6 * math.exp(-0.3 * i)

        h = _prenorm(xf, row(ffn1_pre_g[i]))
        xf, h, w_o = _ffn(h, xf, ffn1_w_gate[i], ffn1_w_up[i], ffn1_w_down[i],
                          row(ffn1_post_g[i]), row(mix_pre_g[i]), w_out[i])

        wi = w_in[i]
        c0 = 2 * qk + da + 3 * rw
        w_l = jnp.concatenate([_pad_to(wi[:, c0:c0 + dl], 1, dlp),
                               _pad_to(wi[:, c0 + dl:c0 + dl + al], 1, alp),
                               _pad_to(wi[:, c0 + dl + al:], 1, glp)], axis=1).astype(BF16)
        mu = rwkv_mu[i]
        mu_l = jnp.concatenate([_pad_to(mu[3 * rw:3 * rw + dl], 0, dlp),
                                _pad_to(mu[3 * rw + dl:3 * rw + dl + al], 0, alp),
                                _pad_to(mu[3 * rw + dl + al:], 0, glp)])
        zqk = _qk_proj(h, wi[:, :2 * qk].astype(BF16), cos_t, sa_t, sb_t, seq=seq, qk_cols=qk,
                       qscale=hd ** -0.5 * math.log2(math.e))
        vt = _vt_proj(h, wi[:, 2 * qk:2 * qk + da].astype(BF16).T)
        zr = _shift_proj(h, wi[:, 2 * qk + da:c0].astype(BF16), row(mu[:3 * rw]), seq=seq)
        zl = _shift_proj(h, w_l, row(mu_l), seq=seq)

        omix = _diff_attention(zqk, vt, row(diff_lambda_q1[i]), row(diff_lambda_k1[i]),
                               row(diff_lambda_q2[i]), row(diff_lambda_k2[i]),
                               diff_subln_g[i].reshape(-1, 1).astype(F32),
                               batch=batch, seq=seq, heads=heads, hd=hd, out_cols=da + rw,
                               lambda_init=lambda_init)
        omix = _rwkv(zr, zl, omix, row(rwkv_w0[i]),
                     _pad_to(rwkv_w2[i].astype(BF16), 0, dlp), row(rwkv_a0[i]),
                     _pad_to(rwkv_a2[i].astype(BF16), 0, alp), _pad_to(rwkv_g2[i].astype(BF16), 0, glp),
                     row(rwkv_k_k[i]), row(rwkv_k_a[i]), row(rwkv_r_k[i]),
                     row(rwkv_ln_w[i]), row(rwkv_ln_b[i]), batch=batch, seq=seq, rw=rw, hn=hn)

        xf, h = _down(omix, w_o, xf, row(mix_post_g[i]), row(ffn2_pre_g[i]), 1.0)
        xf, h, w_pg = _ffn(h, xf, ffn2_w_gate[i], ffn2_w_up[i], ffn2_w_down[i],
                           row(ffn2_post_g[i]), row(ple_pre_g[i]), ple_w_gate[i])
        xf = _ple(h, w_pg, xf, p[i].reshape(m, -1).astype(BF16),
                  ple_w_proj[i].astype(BF16), row(ple_post_g[i]))
    return xf.reshape(batch, seq, d)
```

```python
import functools
import math

import jax
import jax.numpy as jnp
from jax import lax
from jax.experimental import pallas as pl
from jax.experimental.pallas import tpu as pltpu

F32 = jnp.float32
BF16 = jnp.bfloat16

RMS_EPS = 1e-6
RW_LN_EPS = 64e-5
ATTN_CHUNK = 64
ROPE_THETA = 500000.0
RW_CHUNK = 64

V7X_LANES = 128
V7X_VMEM_CAP_BYTES = 58 * 1024 * 1024


def _tile(n, pref, align):
    t = (min(n, pref) // align) * align
    while t >= align:
        if n % t == 0:
            return t
        t -= align
    return n


def _round_up(n, m):
    return (n + m - 1) // m * m


def _vmem_limit(pipelined_bytes, resident_bytes):
    need = 2 * pipelined_bytes + resident_bytes + (4 << 20)
    return int(min(V7X_VMEM_CAP_BYTES, max(need, 16 << 20)))


def _rms(x, g, eps=RMS_EPS):
    return x * lax.rsqrt(jnp.mean(x * x, axis=-1, keepdims=True) + eps) * g


def _dot(a, b):
    return jnp.dot(a, b, preferred_element_type=F32)


def _dot_nt(a, b):
    return lax.dot_general(a, b, (((1,), (1,)), ((), ())), preferred_element_type=F32)


def _dot_tn(a, b):
    return lax.dot_general(a, b, (((0,), (0,)), ((), ())), preferred_element_type=F32)


def _prenorm_kernel(x_ref, g_ref, h_ref):
    h_ref[...] = _rms(x_ref[...], g_ref[...]).astype(h_ref.dtype)


def _prenorm(x, g):
    m, d = x.shape
    tm = _tile(m, 256, 8)
    return pl.pallas_call(
        _prenorm_kernel,
        out_shape=jax.ShapeDtypeStruct((m, d), BF16),
        grid=(m // tm,),
        in_specs=[pl.BlockSpec((tm, d), lambda i: (i, 0)),
                  pl.BlockSpec((1, d), lambda i: (0, 0))],
        out_specs=pl.BlockSpec((tm, d), lambda i: (i, 0)),
        compiler_params=pltpu.CompilerParams(
            dimension_semantics=("arbitrary",),
            vmem_limit_bytes=_vmem_limit(tm * d * 6, 3 * tm * d * 4)),
        name="prenorm",
    )(x, g)


def _gateup_kernel(h_ref, wg_ref, wu_ref, *refs):
    ncast = len(refs) // 2
    o_ref = refs[ncast]
    h = h_ref[...]
    g = _dot(h, wg_ref[...].astype(BF16))
    u = _dot(h, wu_ref[...].astype(BF16))
    o_ref[...] = (g * jax.nn.sigmoid(g) * u).astype(o_ref.dtype)
    for src_ref, dst_ref in zip(refs[:ncast], refs[ncast + 1:]):
        dst_ref[...] = src_ref[...].astype(dst_ref.dtype)


def _gateup(h, wg, wu, casts):
    m, d = h.shape
    f = wg.shape[1]
    tm = _tile(m, 2048, 16)
    tn = _tile(f, 256, V7X_LANES)
    nj = f // tn
    steps = (m // tm) * nj
    cast_specs = []
    for w in casts:
        slab = _round_up(pl.cdiv(w.shape[0], steps), 16)
        while w.shape[0] % slab:
            slab += 16
        last = w.shape[0] // slab - 1
        cast_specs.append(pl.BlockSpec((slab, w.shape[1]),
                                       lambda i, j, last=last: (jnp.minimum(i * nj + j, last), 0)))
    cast_bytes = sum(sp.block_shape[0] * sp.block_shape[1] * 6 for sp in cast_specs)
    res = pl.pallas_call(
        _gateup_kernel,
        out_shape=(jax.ShapeDtypeStruct((m, f), BF16), *(jax.ShapeDtypeStruct(w.shape, BF16) for w in casts)),
        grid=(m // tm, nj),
        in_specs=[pl.BlockSpec((tm, d), lambda i, j: (i, 0), pipeline_mode=pl.Buffered(1)),
                  pl.BlockSpec((d, tn), lambda i, j: (0, j)),
                  pl.BlockSpec((d, tn), lambda i, j: (0, j)),
                  *cast_specs],
        out_specs=(pl.BlockSpec((tm, tn), lambda i, j: (i, j)), *cast_specs),
        compiler_params=pltpu.CompilerParams(
            dimension_semantics=("arbitrary", "arbitrary"),
            vmem_limit_bytes=_vmem_limit(2 * d * tn * 4 + tm * tn * 2 + cast_bytes,
                                         tm * d * 2 + 2 * d * tn * 2 + 4 * tm * tn * 4)),
        name="swiglu_up",
    )(h, wg, wu, *casts)
    return res[0], res[1:]


def _deferred_rows(a_ref, w_ref, acc_ref, *, ntiles, nk, last, nchunks, epilogue):
    i = pl.program_id(0)
    k = pl.program_id(1)
    tk = a_ref.shape[1]
    cur = acc_ref.at[i % 2]
    prev = acc_ref.at[(i + 1) % 2]
    rc = acc_ref.shape[1] // nchunks
    full_end = nk - (last < tk)
    assert nchunks <= full_end
    mm = i < ntiles
    epi = (i >= 1) & (k < nchunks)

    def run_epilogue():
        epilogue(prev[pl.ds(pl.multiple_of(k * rc, rc), rc), :])

    @pl.when(mm & (k == 0) & (i == 0))
    def _():
        cur[...] = _dot(a_ref[...], w_ref[...])

    @pl.when(mm & (k == 0) & (i >= 1))
    def _():
        run_epilogue()
        cur[...] = _dot(a_ref[...], w_ref[...])

    @pl.when(mm & (k > 0) & (k < full_end) & epi)
    def _():
        run_epilogue()
        cur[...] += _dot(a_ref[...], w_ref[...])

    @pl.when(mm & (k > 0) & (k < full_end) & jnp.logical_not(epi))
    def _():
        cur[...] += _dot(a_ref[...], w_ref[...])

    if last < tk:
        @pl.when(mm & (k == nk - 1))
        def _():
            cur[...] += _dot(a_ref[:, :last], w_ref[:last, :])

    @pl.when(jnp.logical_not(mm) & (k < nchunks))
    def _():
        run_epilogue()


def _deferred_specs(m, kdim, d):
    tm = _tile(m, 1024, 16)
    tk = 512
    ntiles = m // tm
    nk = pl.cdiv(kdim, tk)
    last = kdim - (nk - 1) * tk
    assert last % V7X_LANES == 0 and nk >= 2
    nchunks = 1 << ((nk - (last < tk)).bit_length() - 1)
    nchunks = min(nchunks, tm // 8)
    rc = tm // nchunks
    a_spec = pl.BlockSpec((tm, tk), lambda i, k: (jnp.minimum(i, ntiles - 1), jnp.where(i < ntiles, k, nk - 1)))
    w_spec = pl.BlockSpec((tk, d), lambda i, k: (jnp.where(i < ntiles, k, nk - 1), 0))
    chunk = lambda i, k: (jnp.where(i >= 1, (i - 1) * nchunks + jnp.minimum(k, nchunks - 1), 0), 0)
    return tm, tk, ntiles, nk, last, nchunks, rc, a_spec, w_spec, chunk


def _down_kernel(a_ref, w_ref, x_ref, gpost_ref, gnext_ref, o_ref, h_ref, acc_ref, *, scale, **tiling):
    def epilogue(f):
        xn = x_ref[...] + scale * _rms(f, gpost_ref[...])
        o_ref[...] = xn
        h_ref[...] = _rms(xn, gnext_ref[...]).astype(h_ref.dtype)

    _deferred_rows(a_ref, w_ref, acc_ref, epilogue=epilogue, **tiling)


def _down(a, w, x, gpost, gnext, scale):
    m, kdim = a.shape
    d = w.shape[1]
    tm, tk, ntiles, nk, last, nchunks, rc, a_spec, w_spec, chunk = _deferred_specs(m, kdim, d)
    return pl.pallas_call(
        functools.partial(_down_kernel, scale=scale, ntiles=ntiles, nk=nk, last=last, nchunks=nchunks),
        out_shape=(jax.ShapeDtypeStruct((m, d), F32), jax.ShapeDtypeStruct((m, d), BF16)),
        grid=(ntiles + 1, nk),
        in_specs=[a_spec, w_spec,
                  pl.BlockSpec((rc, d), chunk),
                  pl.BlockSpec((1, d), lambda i, k: (0, 0)),
                  pl.BlockSpec((1, d), lambda i, k: (0, 0))],
        out_specs=(pl.BlockSpec((rc, d), chunk), pl.BlockSpec((rc, d), chunk)),
        scratch_shapes=[pltpu.VMEM((2, tm, d), F32)],
        compiler_params=pltpu.CompilerParams(
            dimension_semantics=("arbitrary", "arbitrary"),
            vmem_limit_bytes=_vmem_limit(tm * tk * 2 + tk * d * 2 + rc * d * 10, 2 * tm * d * 4 + 6 * rc * d * 4)),
        name="down_norm_residual",
    )(a, w, x, gpost, gnext)


def _ple_kernel(a_ref, w_ref, x_ref, p_ref, wp_ref, gpost_ref, o_ref, acc_ref, **tiling):
    def epilogue(f):
        proj = _dot(p_ref[...], wp_ref[...])
        o_ref[...] = x_ref[...] + _rms(proj * jax.nn.sigmoid(f), gpost_ref[...])

    _deferred_rows(a_ref, w_ref, acc_ref, epilogue=epilogue, **tiling)


def _ple(a, w, x, p, wp, gpost):
    m, kdim = a.shape
    d = w.shape[1]
    pd = p.shape[1]
    tm, tk, ntiles, nk, last, nchunks, rc, a_spec, w_spec, chunk = _deferred_specs(m, kdim, d)
    return pl.pallas_call(
        functools.partial(_ple_kernel, ntiles=ntiles, nk=nk, last=last, nchunks=nchunks),
        out_shape=jax.ShapeDtypeStruct((m, d), F32),
        grid=(ntiles + 1, nk),
        in_specs=[a_spec, w_spec,
                  pl.BlockSpec((rc, d), chunk),
                  pl.BlockSpec((rc, pd), chunk),
                  pl.BlockSpec((pd, d), lambda i, k: (0, 0)),
                  pl.BlockSpec((1, d), lambda i, k: (0, 0))],
        out_specs=pl.BlockSpec((rc, d), chunk),
        scratch_shapes=[pltpu.VMEM((2, tm, d), F32)],
        compiler_params=pltpu.CompilerParams(
            dimension_semantics=("arbitrary", "arbitrary"),
            vmem_limit_bytes=_vmem_limit(tm * tk * 2 + tk * d * 2 + rc * d * 8 + rc * pd * 2 + pd * d * 2,
                                         2 * tm * d * 4 + 6 * rc * d * 4)),
        name="ple_gate_norm_residual",
    )(a, w, x, p, wp, gpost)


def _qk_kernel(h_ref, w_ref, cos_ref, sa_ref, sb_ref, o_ref, *, q_tiles, qscale):
    j = pl.program_id(1)
    w = w_ref[...].astype(BF16)
    sc = jnp.where(j < q_tiles, qscale, 1.0).astype(F32)
    parts = 4
    part = h_ref.shape[0] // parts
    for r in range(parts):
        rows = pl.ds(r * part, part)
        z = _dot(h_ref[rows, :], w)
        c, sa, sb = cos_ref[rows, :], sa_ref[rows, :], sb_ref[rows, :]
        outs = []
        for g in range(z.shape[1] // V7X_LANES):
            zg = z[:, g * V7X_LANES:(g + 1) * V7X_LANES]
            rg = zg * c + pltpu.roll(zg, V7X_LANES - 16, 1) * sa + pltpu.roll(zg, 16, 1) * sb
            outs.append(rg * sc)
        o_ref[rows, :] = jnp.concatenate(outs, axis=1).astype(o_ref.dtype)


def _qk_proj(h, w, cos_t, sa_t, sb_t, *, seq, qk_cols, qscale):
    m, d = h.shape
    n = 2 * qk_cols
    tm = _tile(seq, 1024, 16)
    tn = _tile(qk_cols, 512, V7X_LANES)
    tpb = seq // tm
    return pl.pallas_call(
        functools.partial(_qk_kernel, q_tiles=qk_cols // tn, qscale=qscale),
        out_shape=jax.ShapeDtypeStruct((m, n), BF16),
        grid=(m // tm, n // tn),
        in_specs=[pl.BlockSpec((tm, d), lambda i, j: (i, 0)),
                  pl.BlockSpec((d, tn), lambda i, j: (0, j)),
                  pl.BlockSpec((tm, V7X_LANES), lambda i, j: (i % tpb, 0)),
                  pl.BlockSpec((tm, V7X_LANES), lambda i, j: (i % tpb, 0)),
                  pl.BlockSpec((tm, V7X_LANES), lambda i, j: (i % tpb, 0))],
        out_specs=pl.BlockSpec((tm, tn), lambda i, j: (i, j)),
        compiler_params=pltpu.CompilerParams(
            dimension_semantics=("arbitrary", "arbitrary"),
            vmem_limit_bytes=_vmem_limit(tm * d * 2 + d * tn * 4 + tm * tn * 2 + 3 * tm * V7X_LANES * 4,
                                         d * tn * 2 + 5 * tm * tn * 4)),
        name="qk_proj_rotary",
    )(h, w, cos_t, sa_t, sb_t)


def _vt_proj_kernel(wt_ref, h_ref, o_ref):
    o_ref[...] = _dot_nt(wt_ref[...], h_ref[...]).astype(o_ref.dtype)


def _vt_proj(h, wt):
    m, d = h.shape
    n = wt.shape[0]
    tm = _tile(m, 1024, V7X_LANES)
    tn = _tile(n, 512, 16)
    return pl.pallas_call(
        _vt_proj_kernel,
        out_shape=jax.ShapeDtypeStruct((n, m), BF16),
        grid=(m // tm, n // tn),
        in_specs=[pl.BlockSpec((tn, d), lambda i, j: (j, 0)),
                  pl.BlockSpec((tm, d), lambda i, j: (i, 0))],
        out_specs=pl.BlockSpec((tn, tm), lambda i, j: (j, i)),
        compiler_params=pltpu.CompilerParams(
            dimension_semantics=("arbitrary", "arbitrary"),
            vmem_limit_bytes=_vmem_limit(tm * d * 2 + d * tn * 2 + tm * tn * 2, 2 * tm * tn * 4)),
        name="v_proj_transposed",
    )(wt, h)


def _shift_proj_kernel(h_ref, w_ref, mu_ref, o_ref, carry_ref, *, tiles_per_seq):
    i = pl.program_id(0)
    j = pl.program_id(1)

    @pl.when(i == 0)
    def _():
        carry_ref[j] = jnp.zeros(carry_ref.shape[1:], F32)

    z = _dot(h_ref[...], w_ref[...])
    prev_row = jnp.where(i % tiles_per_seq == 0, 0.0, carry_ref[j, 0:1, :])
    row = lax.broadcasted_iota(jnp.int32, z.shape, 0)
    zp = jnp.where(row == 0, prev_row, pltpu.roll(z, 1, 0))
    carry_ref[j, 0:1, :] = z[z.shape[0] - 1:, :]
    o_ref[...] = z + (zp - z) * mu_ref[...]


def _shift_proj(h, w, mu, *, seq):
    m, d = h.shape
    n = w.shape[1]
    tm = _tile(seq, 1024, 16)
    tn = _tile(n, 512, V7X_LANES)
    return pl.pallas_call(
        functools.partial(_shift_proj_kernel, tiles_per_seq=seq // tm),
        out_shape=jax.ShapeDtypeStruct((m, n), F32),
        grid=(m // tm, n // tn),
        in_specs=[pl.BlockSpec((tm, d), lambda i, j: (i, 0)),
                  pl.BlockSpec((d, tn), lambda i, j: (0, j)),
                  pl.BlockSpec((1, tn), lambda i, j: (0, j))],
        out_specs=pl.BlockSpec((tm, tn), lambda i, j: (i, j)),
        scratch_shapes=[pltpu.VMEM((n // tn, 8, tn), F32)],
        compiler_params=pltpu.CompilerParams(
            dimension_semantics=("arbitrary", "arbitrary"),
            vmem_limit_bytes=_vmem_limit(tm * d * 2 + d * tn * 2 + tm * tn * 4, 8 * n * 4 + 4 * tm * tn * 4)),
        name="rwkv_proj_token_shift",
    )(h, w, mu)


def _attn_kernel(q_ref, k_ref, vt_ref, lq1_ref, lk1_ref, lq2_ref, lk2_ref, sg_ref, o_ref,
                 m_ref, l_ref, acc_ref, *, tq, hd, lambda_init):
    qi = pl.program_id(2)
    qs = [q_ref[h * tq:(h + 1) * tq, :] for h in range(2)]
    m_ref[...] = jnp.full(m_ref.shape, -jnp.inf, F32)
    l_ref[...] = jnp.zeros(l_ref.shape, F32)
    acc_ref[...] = jnp.zeros(acc_ref.shape, F32)

    def tile(start, work):
        kt = k_ref[pl.ds(start, tq), :]
        vt = vt_ref[:, pl.ds(start, tq)]
        chains = [(h, c, diag) for h, diag in work for c in range(2)]
        ss = [_dot_nt(kt[:, c * hd:(c + 1) * hd], qs[h][:, c * hd:(c + 1) * hd]) for h, c, _ in chains]
        if any(diag for _, _, diag in chains):
            sh = ATTN_CHUNK.bit_length() - 1
            ck = lax.shift_right_logical(lax.broadcasted_iota(jnp.int32, ss[0].shape, 0), sh)
            rq = lax.shift_right_logical(lax.broadcasted_iota(jnp.int32, ss[0].shape, 1), sh)
            ss = [jnp.where(ck <= rq, s, -jnp.inf) if diag else s for s, (_, _, diag) in zip(ss, chains)]
        m_olds = [m_ref[h, c] for h, c, _ in chains]
        m_news = [jnp.maximum(m_old, jnp.max(s, axis=0, keepdims=True)) for m_old, s in zip(m_olds, ss)]
        ps = [jnp.exp2(s - m_new) for s, m_new in zip(ss, m_news)]
        alphas = [jnp.exp2(m_old - m_new) for m_old, m_new in zip(m_olds, m_news)]
        vt1 = jnp.concatenate([vt, jnp.ones((16, vt.shape[1]), BF16)], axis=0)
        nv = vt.shape[0]
        pvs = [_dot(vt1, p.astype(BF16)) for p in ps]
        for (h, c, _), alpha, pv, m_new in zip(chains, alphas, pvs, m_news):
            l_ref[h, c] = alpha * l_ref[h, c] + pv[nv:nv + 1]
            acc_ref[h, c] = alpha * acc_ref[h, c] + pv[:nv]
            m_ref[h, c] = m_new

    def body(j, carry):
        tile(pl.multiple_of(j * tq, tq), [(0, False), (1, False)])
        return carry

    lax.fori_loop(0, 2 * qi, body, 0)
    tile(pl.multiple_of(2 * qi * tq, tq), [(0, True), (1, False)])
    tile(pl.multiple_of((2 * qi + 1) * tq, tq), [(1, True)])

    lam = (jnp.exp(jnp.sum(lq1_ref[...] * lk1_ref[...], axis=-1, keepdims=True))
           - jnp.exp(jnp.sum(lq2_ref[...] * lk2_ref[...], axis=-1, keepdims=True)) + lambda_init)
    for h in range(2):
        o = acc_ref[h, 0] / l_ref[h, 0] - lam * (acc_ref[h, 1] / l_ref[h, 1])
        o = o * lax.rsqrt(jnp.mean(o * o, axis=0, keepdims=True) + RMS_EPS) * sg_ref[...] * (1.0 - lambda_init)
        o_ref[h * tq:(h + 1) * tq, :] = o.T.astype(o_ref.dtype)


def _diff_attention(zqk, vt, lq1, lk1, lq2, lk2, subln_g, *, batch, seq, heads, hd, out_cols, lambda_init):
    m = zqk.shape[0]
    vd = 2 * hd
    tq = _tile(seq // 2, 512, V7X_LANES)
    nq = seq // (2 * tq)
    vec = pl.BlockSpec((1, hd), lambda b, h, i: (0, 0))
    return pl.pallas_call(
        functools.partial(_attn_kernel, tq=tq, hd=hd, lambda_init=lambda_init),
        out_shape=jax.ShapeDtypeStruct((m, out_cols), BF16),
        grid=(batch, heads, nq),
        in_specs=[pl.BlockSpec((2 * tq, vd), lambda b, h, i: (b * nq + i, h)),
                  pl.BlockSpec((seq, vd), lambda b, h, i: (b, heads + h)),
                  pl.BlockSpec((vd, seq), lambda b, h, i: (h, b)),
                  vec, vec, vec, vec,
                  pl.BlockSpec((vd, 1), lambda b, h, i: (0, 0))],
        out_specs=pl.BlockSpec((2 * tq, vd), lambda b, h, i: (b * nq + i, h)),
        scratch_shapes=[pltpu.VMEM((2, 2, 1, tq), F32), pltpu.VMEM((2, 2, 1, tq), F32),
                        pltpu.VMEM((2, 2, vd, tq), F32)],
        compiler_params=pltpu.CompilerParams(
            dimension_semantics=("arbitrary", "arbitrary", "arbitrary"),
            vmem_limit_bytes=_vmem_limit(4 * tq * vd * 2 + 2 * seq * vd * 2,
                                         4 * tq * vd * 4 + 20 * tq * tq * 4)),
        name="diff_attention",
    )(zqk, zqk, vt, lq1, lk1, lq2, lk2, subln_g)


def _split3_bf16(x):
    h1 = x.astype(BF16)
    r1 = x - h1.astype(F32)
    h2 = r1.astype(BF16)
    h3 = (r1 - h2.astype(F32)).astype(BF16)
    return h1, h2, h3


def _rwkv_kernel(z_ref, zlc_ref, zln_ref, w0_ref, w2_ref, a0_ref, a2_ref, g2_ref,
                 kk_ref, ka_ref, rk_ref, lnw_ref, lnb_ref, alias_ref, o_ref,
                 st_ref, *side_refs, rw, hn, nc):
    del alias_ref
    c = pl.program_id(1)
    g = pl.program_id(0) * nc + c
    lt = z_ref.shape[0]
    ln = V7X_LANES
    sides = (side_refs[:4], side_refs[4:])

    @pl.when(c == 0)
    def _():
        st_ref[...] = jnp.zeros(st_ref.shape, F32)

    def side_paths(zlm, dst):
        lw_ref, cum_ref, as_ref, gt_ref = dst
        dl = w2_ref.shape[0]
        al = a2_ref.shape[0]
        wpre = w0_ref[...] + _dot(jnp.tanh(zlm[:, :dl]).astype(BF16), w2_ref[...])
        softplus = jnp.maximum(-wpre, 0.0) + jnp.log(1.0 + jnp.exp(-jnp.abs(wpre)))
        lw = -jnp.exp(-softplus - 0.5)
        lw_ref[...] = lw
        tri = (lax.broadcasted_iota(jnp.int32, (lt, lt), 0)
               >= lax.broadcasted_iota(jnp.int32, (lt, lt), 1)).astype(BF16)
        h1, h2, h3 = _split3_bf16(lw)
        cum_ref[...] = _dot(tri, h1) + _dot(tri, h2) + _dot(tri, h3)
        as_ref[...] = jax.nn.sigmoid(a0_ref[...] + _dot(zlm[:, dl:dl + al].astype(BF16), a2_ref[...]))
        gt_ref[...] = _dot(jax.nn.sigmoid(zlm[:, dl + al:]).astype(BF16), g2_ref[...])

    @pl.when(g == 0)
    def _():
        side_paths(zlc_ref[...], sides[0])

    lane = lax.broadcasted_iota(jnp.int32, (lt, ln), 1)
    first = lane < hn
    r2 = lax.broadcasted_iota(jnp.int32, (2 * hn, 2 * hn), 0)
    c2 = lax.broadcasted_iota(jnp.int32, (2 * hn, 2 * hn), 1)
    hsh = hn.bit_length() - 1
    same = lax.shift_right_logical(r2, hsh) == lax.shift_right_logical(c2, hsh)
    strict = same & (r2 > c2)
    incl = same & (r2 >= c2)
    eye = (r2 == c2).astype(F32)
    zeros = jnp.zeros((2 * hn, ln), F32)

    def headsum(x):
        s0 = jnp.sum(jnp.where(first, x, 0.0), axis=-1, keepdims=True)
        s1 = jnp.sum(jnp.where(first, 0.0, x), axis=-1, keepdims=True)
        return jnp.where(first, s0, s1)

    def bd(x):
        return jnp.concatenate([jnp.where(first, x, 0.0), jnp.where(first, 0.0, x)], axis=0)

    def fold(x):
        return x[:hn] + x[hn:]

    def pairs(offs, src):
        lw_ref, cum_ref, as_ref, gt_ref = src
        n2 = 2 * hn
        sls = [pl.ds(off, ln) for off in offs]
        rs = [z_ref[:, pl.ds(off, ln)] for off in offs]
        ks = [z_ref[:, pl.ds(rw + off, ln)] for off in offs]
        vs = [z_ref[:, pl.ds(2 * rw + off, ln)] for off in offs]
        asgs = [as_ref[:, sl] for sl in sls]
        kks = [k * kk_ref[:, sl] for k, sl in zip(ks, sls)]
        kks = [kk / jnp.maximum(jnp.sqrt(headsum(kk * kk)), 1e-12) for kk in kks]
        kns = [k * (1.0 + (asg - 1.0) * ka_ref[:, sl]) for k, asg, sl in zip(ks, asgs, sls)]
        bns = [kk * asg for kk, asg in zip(kks, asgs)]

        lws = [lw_ref[:, sl] for sl in sls]
        cums = [cum_ref[:, sl] for sl in sls]
        tots = [cum[lt - 1:lt, :] for cum in cums]
        e_invs = [jnp.exp(-cum) for cum in cums]
        ats = [bd(-kk * jnp.exp(cum - lw)) for kk, cum, lw in zip(kks, cums, lws)]
        rts = [bd(r * jnp.exp(cum)) for r, cum in zip(rs, cums)]
        vbs = [bd(v) for v in vs]

        fulls = [_dot_nt(jnp.concatenate([at, rt], axis=0).astype(BF16),
                         jnp.concatenate([bd(bn * e), bd(kn * e)], axis=0).astype(BF16))
                 for at, rt, bn, kn, e in zip(ats, rts, bns, kns, e_invs)]
        a_abs = [jnp.where(strict, f[:n2, :n2], 0.0) for f in fulls]
        a_aks = [jnp.where(strict, f[:n2, n2:], 0.0) for f in fulls]
        a_rs = [jnp.concatenate([jnp.where(incl, f[n2:, :n2], 0.0), jnp.where(incl, f[n2:, n2:], 0.0)],
                                axis=1).astype(BF16) for f in fulls]

        akvs = [_dot(a_ak.astype(BF16), vb.astype(BF16)) for a_ak, vb in zip(a_aks, vbs)]
        zqs = [jnp.concatenate([at, akv], axis=1) for at, akv in zip(ats, akvs)]
        apows = a_abs
        steps = max(1, (lt - 1).bit_length())
        for it in range(steps):
            if it + 1 < steps:
                aws = [_dot(ap.astype(BF16), jnp.concatenate([zq, ap], axis=1).astype(BF16))
                       for ap, zq in zip(apows, zqs)]
                zqs = [zq + aw[:, :2 * ln] for zq, aw in zip(zqs, aws)]
                apows = [aw[:, 2 * ln:] for aw in aws]
            else:
                zqs = [zq + _dot(ap.astype(BF16), zq.astype(BF16)) for ap, zq in zip(apows, zqs)]

        rhs2s = [jnp.concatenate([zq, jnp.concatenate([zeros, vb], axis=1)], axis=0).astype(BF16)
                 for zq, vb in zip(zqs, vbs)]
        rys = [_dot(a_r, rhs2) for a_r, rhs2 in zip(a_rs, rhs2s)]
        e_ends = [jnp.exp(tot - cum) for tot, cum in zip(tots, cums)]
        ghs = [_dot_tn(jnp.concatenate([bd(bn * e), bd(kn * e)], axis=0).astype(BF16), rhs2)
               for bn, kn, e, rhs2 in zip(bns, kns, e_ends, rhs2s)]
        lhs4s = [jnp.concatenate([fold(rt + ry[:, :ln]), fold(gh[:, :ln] + eye * jnp.exp(tot))],
                                 axis=0).astype(BF16)
                 for rt, ry, gh, tot in zip(rts, rys, ghs, tots)]
        outs = [_dot(lhs4, bd(st_ref[:, sl]).astype(BF16)) for lhs4, sl in zip(lhs4s, sls)]
        for sl, out, ry, gh, r, kn, v in zip(sls, outs, rys, ghs, rs, kns, vs):
            y = out[:hn] + fold(ry[:, ln:])
            st_ref[:, sl] = out[hn:] + fold(gh[:, ln:])
            mean = headsum(y) * (1.0 / hn)
            yc = y - mean
            var = headsum(yc * yc) * (1.0 / hn)
            yn = yc * lax.rsqrt(var + RW_LN_EPS) * lnw_ref[:, sl] + lnb_ref[:, sl]
            bonus = headsum(r * kn * rk_ref[:, sl]) * v
            o_ref[:, sl] = ((yn + bonus) * gt_ref[:, sl]).astype(o_ref.dtype)

    def step(cur, nxt):
        side_paths(zln_ref[...], nxt)
        pairs([u * ln for u in range(rw // ln)], cur)

    for parity in range(2):
        @pl.when(g % 2 == parity)
        def _():
            step(sides[parity], sides[1 - parity])


def _rwkv(zr, zl, omix, w0, w2, a0, a2, g2, k_k, k_a, r_k, ln_w, ln_b, *, batch, seq, rw, hn):
    m, zc = zr.shape
    lc = zl.shape[1]
    assert hn * 2 == V7X_LANES and RW_CHUNK == hn and seq % RW_CHUNK == 0 and zc == 3 * rw
    lt = RW_CHUNK
    nc = seq // lt
    nsteps = batch * nc
    ocol = omix.shape[1] // rw - 1
    assert (ocol + 1) * rw == omix.shape[1]
    row = lambda n: pl.BlockSpec((1, n), lambda b, c: (0, 0))
    full = lambda a: pl.BlockSpec(a.shape, lambda b, c: (0, 0))
    return pl.pallas_call(
        functools.partial(_rwkv_kernel, rw=rw, hn=hn, nc=nc),
        out_shape=jax.ShapeDtypeStruct(omix.shape, omix.dtype),
        grid=(batch, nc),
        in_specs=[pl.BlockSpec((lt, zc), lambda b, c: (b * nc + c, 0)),
                  pl.BlockSpec((lt, lc), lambda b, c: (b * nc + c, 0)),
                  pl.BlockSpec((lt, lc), lambda b, c: (jnp.minimum(b * nc + c + 1, nsteps - 1), 0)),
                  row(rw), full(w2), row(rw), full(a2), full(g2),
                  row(rw), row(rw), row(rw), row(rw), row(rw),
                  pl.BlockSpec(memory_space=pl.ANY)],
        out_specs=pl.BlockSpec((lt, rw), lambda b, c: (b * nc + c, ocol)),
        scratch_shapes=[pltpu.VMEM((hn, rw), F32)] + [pltpu.VMEM((lt, rw), F32)] * 8,
        input_output_aliases={13: 0},
        compiler_params=pltpu.CompilerParams(
            dimension_semantics=("arbitrary", "arbitrary"),
            vmem_limit_bytes=_vmem_limit(lt * (zc + 2 * lc) * 4 + (w2.size + a2.size + g2.size) * 2 + lt * rw * 2,
                                         9 * lt * rw * 4 + (8 << 20))),
        name="rwkv7_chunked",
    )(zr, zl, zl, w0, w2, a0, a2, g2, k_k, k_a, r_k, ln_w, ln_b, omix)


def _pad_to(a, axis, n):
    pad = n - a.shape[axis]
    if pad == 0:
        return a
    widths = [(0, 0)] * a.ndim
    widths[axis] = (0, pad)
    return jnp.pad(a, widths)


def _ffn(h, x, w_gate, w_up, w_down, gpost, gnext, later_weight):
    act, (wd, later) = _gateup(h, w_gate, w_up, [w_down, later_weight])
    xn, hn = _down(act, wd, x, gpost, gnext, 0.5)
    return xn, hn, later


def kernel(x, p, ffn1_pre_g, ffn1_w_gate, ffn1_w_up, ffn1_w_down, ffn1_post_g, mix_pre_g, w_in, diff_lambda_q1, diff_lambda_k1, diff_lambda_q2, diff_lambda_k2, diff_subln_g, rwkv_mu, rwkv_w0, rwkv_w2, rwkv_a0, rwkv_a2, rwkv_g2, rwkv_k_k, rwkv_k_a, rwkv_r_k, rwkv_ln_w, rwkv_ln_b, w_out, mix_post_g, ffn2_pre_g, ffn2_w_gate, ffn2_w_up, ffn2_w_down, ffn2_post_g, ple_pre_g, ple_w_gate, ple_w_proj, ple_post_g):
    batch, seq, d = x.shape
    depth = p.shape[0]
    m = batch * seq
    hd = diff_subln_g.shape[-1] // 2
    rw = rwkv_w2.shape[-1]
    da = w_out.shape[1] - rw
    heads = da // (2 * hd)
    qk = heads * 2 * hd
    rheads, hn = rwkv_r_k.shape[1], rwkv_r_k.shape[2]
    dl, al, gl = rwkv_w2.shape[1], rwkv_a2.shape[1], rwkv_g2.shape[1]
    dlp, alp, glp = (_round_up(n, V7X_LANES) for n in (dl, al, gl))
    rot = hd // 4
    assert rheads * hn == rw and da == rw and w_in.shape[2] == 2 * qk + da + 3 * rw + dl + al + gl

    inv = ROPE_THETA ** (-jnp.arange(0, rot, 2, dtype=F32) / rot)
    ang = jnp.arange(seq).astype(F32)[:, None] * inv[None, :]
    half = rot // 2
    ones = jnp.ones((seq, hd - rot), F32)
    zeros_h = jnp.zeros((seq, half), F32)
    zeros_r = jnp.zeros((seq, hd - rot), F32)
    cos_t = jnp.concatenate([jnp.cos(ang), jnp.cos(ang), ones], axis=1)
    sa_t = jnp.concatenate([-jnp.sin(ang), zeros_h, zeros_r], axis=1)
    sb_t = jnp.concatenate([zeros_h, jnp.sin(ang), zeros_r], axis=1)
    assert half == 16 and hd == V7X_LANES

    row = lambda a: a.reshape(1, -1).astype(F32)
    xf = x.reshape(m, d)
    for i in range(depth):
        lambda_init = 0.8 - 0.6 * math.exp(-0.3 * i)

        h = _prenorm(xf, row(ffn1_pre_g[i]))
        xf, h, w_o = _ffn(h, xf, ffn1_w_gate[i], ffn1_w_up[i], ffn1_w_down[i],
                          row(ffn1_post_g[i]), row(mix_pre_g[i]), w_out[i])

        wi = w_in[i]
        c0 = 2 * qk + da + 3 * rw
        w_l = jnp.concatenate([_pad_to(wi[:, c0:c0 + dl], 1, dlp),
                               _pad_to(wi[:, c0 + dl:c0 + dl + al], 1, alp),
                               _pad_to(wi[:, c0 + dl + al:], 1, glp)], axis=1).astype(BF16)
        mu = rwkv_mu[i]
        mu_l = jnp.concatenate([_pad_to(mu[3 * rw:3 * rw + dl], 0, dlp),
                                _pad_to(mu[3 * rw + dl:3 * rw + dl + al], 0, alp),
                                _pad_to(mu[3 * rw + dl + al:], 0, glp)])
        zqk = _qk_proj(h, wi[:, :2 * qk].astype(BF16), cos_t, sa_t, sb_t, seq=seq, qk_cols=qk,
                       qscale=hd ** -0.5 * math.log2(math.e))
        vt = _vt_proj(h, wi[:, 2 * qk:2 * qk + da].astype(BF16).T)
        zr = _shift_proj(h, wi[:, 2 * qk + da:c0].astype(BF16), row(mu[:3 * rw]), seq=seq)
        zl = _shift_proj(h, w_l, row(mu_l), seq=seq)

        omix = _diff_attention(zqk, vt, row(diff_lambda_q1[i]), row(diff_lambda_k1[i]),
                               row(diff_lambda_q2[i]), row(diff_lambda_k2[i]),
                               diff_subln_g[i].reshape(-1, 1).astype(F32),
                               batch=batch, seq=seq, heads=heads, hd=hd, out_cols=da + rw,
                               lambda_init=lambda_init)
        omix = _rwkv(zr, zl, omix, row(rwkv_w0[i]),
                     _pad_to(rwkv_w2[i].astype(BF16), 0, dlp), row(rwkv_a0[i]),
                     _pad_to(rwkv_a2[i].astype(BF16), 0, alp), _pad_to(rwkv_g2[i].astype(BF16), 0, glp),
                     row(rwkv_k_k[i]), row(rwkv_k_a[i]), row(rwkv_r_k[i]),
                     row(rwkv_ln_w[i]), row(rwkv_ln_b[i]), batch=batch, seq=seq, rw=rw, hn=hn)

        xf, h = _down(omix, w_o, xf, row(mix_post_g[i]), row(ffn2_pre_g[i]), 1.0)
        xf, h, w_pg = _ffn(h, xf, ffn2_w_gate[i], ffn2_w_up[i], ffn2_w_down[i],
                           row(ffn2_post_g[i]), row(ple_pre_g[i]), ple_w_gate[i])
        xf = _ple(h, w_pg, xf, p[i].reshape(m, -1).astype(BF16),
                  ple_w_proj[i].astype(BF16), row(ple_post_g[i]))
    return xf.reshape(batch, seq, d)
```

```python
import functools
import math

import jax
import jax.numpy as jnp
from jax import lax
from jax.experimental import pallas as pl
from jax.experimental.pallas import tpu as pltpu

F32 = jnp.float32
BF16 = jnp.bfloat16

RMS_EPS = 1e-6
RW_LN_EPS = 64e-5
ATTN_CHUNK = 64
ROPE_THETA = 500000.0
RW_CHUNK = 64

V7X_LANES = 128
V7X_SUBLANES_F32 = 8
V7X_SUBLANES_BF16 = 16
V7X_VMEM_CAP_BYTES = 58 * 1024 * 1024


def _tile(n, pref, align):
    t = (min(n, pref) // align) * align
    while t >= align:
        if n % t == 0:
            return t
        t -= align
    return n


def _round_up(n, m):
    return (n + m - 1) // m * m


def _vmem_limit(pipelined_bytes, resident_bytes):
    need = 2 * pipelined_bytes + resident_bytes + (4 << 20)
    return int(min(V7X_VMEM_CAP_BYTES, max(need, 16 << 20)))


def _rms(x, g, eps=RMS_EPS):
    return x * lax.rsqrt(jnp.mean(x * x, axis=-1, keepdims=True) + eps) * g


def _dot(a, b):
    return jnp.dot(a, b, preferred_element_type=F32)


def _dot_nt(a, b):
    return lax.dot_general(a, b, (((1,), (1,)), ((), ())), preferred_element_type=F32)


def _dot_tn(a, b):
    return lax.dot_general(a, b, (((0,), (0,)), ((), ())), preferred_element_type=F32)


def _prenorm_kernel(x_ref, g_ref, h_ref):
    h_ref[...] = _rms(x_ref[...], g_ref[...]).astype(h_ref.dtype)


def _prenorm(x, g):
    m, d = x.shape
    tm = _tile(m, 256, V7X_SUBLANES_F32)
    return pl.pallas_call(
        _prenorm_kernel,
        out_shape=jax.ShapeDtypeStruct((m, d), BF16),
        grid=(m // tm,),
        in_specs=[pl.BlockSpec((tm, d), lambda i: (i, 0)),
                  pl.BlockSpec((1, d), lambda i: (0, 0))],
        out_specs=pl.BlockSpec((tm, d), lambda i: (i, 0)),
        compiler_params=pltpu.CompilerParams(
            dimension_semantics=("arbitrary",),
            vmem_limit_bytes=_vmem_limit(tm * d * 6, 3 * tm * d * 4)),
        name="prenorm",
    )(x, g)


def _gateup_kernel(h_ref, wg_ref, wu_ref, *refs):
    ncast = len(refs) // 2
    o_ref = refs[ncast]
    h = h_ref[...]
    g = _dot(h, wg_ref[...].astype(BF16))
    u = _dot(h, wu_ref[...].astype(BF16))
    o_ref[...] = (g * jax.nn.sigmoid(g) * u).astype(o_ref.dtype)
    for src_ref, dst_ref in zip(refs[:ncast], refs[ncast + 1:]):
        dst_ref[...] = src_ref[...].astype(dst_ref.dtype)


def _gateup(h, wg, wu, casts):
    m, d = h.shape
    f = wg.shape[1]
    tm = _tile(m, 2048, V7X_SUBLANES_BF16)
    tn = _tile(f, 256, V7X_LANES)
    nj = f // tn
    steps = (m // tm) * nj
    cast_specs = []
    for w in casts:
        slab = _round_up(pl.cdiv(w.shape[0], steps), V7X_SUBLANES_BF16)
        while w.shape[0] % slab:
            slab += V7X_SUBLANES_BF16
        last = w.shape[0] // slab - 1
        cast_specs.append(pl.BlockSpec((slab, w.shape[1]),
                                       lambda i, j, last=last: (jnp.minimum(i * nj + j, last), 0)))
    cast_bytes = sum(sp.block_shape[0] * sp.block_shape[1] * 6 for sp in cast_specs)
    res = pl.pallas_call(
        _gateup_kernel,
        out_shape=(jax.ShapeDtypeStruct((m, f), BF16), *(jax.ShapeDtypeStruct(w.shape, BF16) for w in casts)),
        grid=(m // tm, nj),
        in_specs=[pl.BlockSpec((tm, d), lambda i, j: (i, 0), pipeline_mode=pl.Buffered(1)),
                  pl.BlockSpec((d, tn), lambda i, j: (0, j)),
                  pl.BlockSpec((d, tn), lambda i, j: (0, j)),
                  *cast_specs],
        out_specs=(pl.BlockSpec((tm, tn), lambda i, j: (i, j)), *cast_specs),
        compiler_params=pltpu.CompilerParams(
            dimension_semantics=("arbitrary", "arbitrary"),
            vmem_limit_bytes=_vmem_limit(2 * d * tn * 4 + tm * tn * 2 + cast_bytes,
                                         tm * d * 2 + 2 * d * tn * 2 + 4 * tm * tn * 4)),
        name="swiglu_up",
    )(h, wg, wu, *casts)
    return res[0], res[1:]


def _deferred_rows(a_ref, w_ref, acc_ref, *, ntiles, nk, last, nchunks, epilogue):
    i = pl.program_id(0)
    k = pl.program_id(1)
    tk = a_ref.shape[1]
    cur = acc_ref.at[i % 2]
    prev = acc_ref.at[(i + 1) % 2]
    rc = acc_ref.shape[1] // nchunks
    full_end = nk - (last < tk)
    assert nchunks <= full_end
    mm = i < ntiles
    epi = (i >= 1) & (k < nchunks)

    def run_epilogue():
        epilogue(prev[pl.ds(pl.multiple_of(k * rc, rc), rc), :])

    @pl.when(mm & (k == 0) & (i == 0))
    def _():
        cur[...] = _dot(a_ref[...], w_ref[...])

    @pl.when(mm & (k == 0) & (i >= 1))
    def _():
        run_epilogue()
        cur[...] = _dot(a_ref[...], w_ref[...])

    @pl.when(mm & (k > 0) & (k < full_end) & epi)
    def _():
        run_epilogue()
        cur[...] += _dot(a_ref[...], w_ref[...])

    @pl.when(mm & (k > 0) & (k < full_end) & jnp.logical_not(epi))
    def _():
        cur[...] += _dot(a_ref[...], w_ref[...])

    if last < tk:
        @pl.when(mm & (k == nk - 1))
        def _():
            cur[...] += _dot(a_ref[:, :last], w_ref[:last, :])

    @pl.when(jnp.logical_not(mm) & (k < nchunks))
    def _():
        run_epilogue()


def _deferred_specs(m, kdim, d):
    tm = _tile(m, 1024, V7X_SUBLANES_BF16)
    tk = 512
    ntiles = m // tm
    nk = pl.cdiv(kdim, tk)
    last = kdim - (nk - 1) * tk
    assert last % V7X_LANES == 0 and nk >= 2
    nchunks = 1 << ((nk - (last < tk)).bit_length() - 1)
    nchunks = min(nchunks, tm // V7X_SUBLANES_F32)
    rc = tm // nchunks
    a_spec = pl.BlockSpec((tm, tk), lambda i, k: (jnp.minimum(i, ntiles - 1), jnp.where(i < ntiles, k, nk - 1)))
    w_spec = pl.BlockSpec((tk, d), lambda i, k: (jnp.where(i < ntiles, k, nk - 1), 0))
    chunk = lambda i, k: (jnp.where(i >= 1, (i - 1) * nchunks + jnp.minimum(k, nchunks - 1), 0), 0)
    return tm, tk, ntiles, nk, last, nchunks, rc, a_spec, w_spec, chunk


def _down_kernel(a_ref, w_ref, x_ref, gpost_ref, gnext_ref, o_ref, h_ref, acc_ref, *, scale, **tiling):
    def epilogue(f):
        xn = x_ref[...] + scale * _rms(f, gpost_ref[...])
        o_ref[...] = xn
        h_ref[...] = _rms(xn, gnext_ref[...]).astype(h_ref.dtype)

    _deferred_rows(a_ref, w_ref, acc_ref, epilogue=epilogue, **tiling)


def _down(a, w, x, gpost, gnext, scale):
    m, kdim = a.shape
    d = w.shape[1]
    tm, tk, ntiles, nk, last, nchunks, rc, a_spec, w_spec, chunk = _deferred_specs(m, kdim, d)
    return pl.pallas_call(
        functools.partial(_down_kernel, scale=scale, ntiles=ntiles, nk=nk, last=last, nchunks=nchunks),
        out_shape=(jax.ShapeDtypeStruct((m, d), F32), jax.ShapeDtypeStruct((m, d), BF16)),
        grid=(ntiles + 1, nk),
        in_specs=[a_spec, w_spec,
                  pl.BlockSpec((rc, d), chunk),
                  pl.BlockSpec((1, d), lambda i, k: (0, 0)),
                  pl.BlockSpec((1, d), lambda i, k: (0, 0))],
        out_specs=(pl.BlockSpec((rc, d), chunk), pl.BlockSpec((rc, d), chunk)),
        scratch_shapes=[pltpu.VMEM((2, tm, d), F32)],
        compiler_params=pltpu.CompilerParams(
            dimension_semantics=("arbitrary", "arbitrary"),
            vmem_limit_bytes=_vmem_limit(tm * tk * 2 + tk * d * 2 + rc * d * 10, 2 * tm * d * 4 + 6 * rc * d * 4)),
        name="down_norm_residual",
    )(a, w, x, gpost, gnext)


def _ple_kernel(a_ref, w_ref, x_ref, p_ref, wp_ref, gpost_ref, o_ref, acc_ref, **tiling):
    def epilogue(f):
        proj = _dot(p_ref[...], wp_ref[...])
        o_ref[...] = x_ref[...] + _rms(proj * jax.nn.sigmoid(f), gpost_ref[...])

    _deferred_rows(a_ref, w_ref, acc_ref, epilogue=epilogue, **tiling)


def _ple(a, w, x, p, wp, gpost):
    m, kdim = a.shape
    d = w.shape[1]
    pd = p.shape[1]
    tm, tk, ntiles, nk, last, nchunks, rc, a_spec, w_spec, chunk = _deferred_specs(m, kdim, d)
    return pl.pallas_call(
        functools.partial(_ple_kernel, ntiles=ntiles, nk=nk, last=last, nchunks=nchunks),
        out_shape=jax.ShapeDtypeStruct((m, d), F32),
        grid=(ntiles + 1, nk),
        in_specs=[a_spec, w_spec,
                  pl.BlockSpec((rc, d), chunk),
                  pl.BlockSpec((rc, pd), chunk),
                  pl.BlockSpec((pd, d), lambda i, k: (0, 0)),
                  pl.BlockSpec((1, d), lambda i, k: (0, 0))],
        out_specs=pl.BlockSpec((rc, d), chunk),
        scratch_shapes=[pltpu.VMEM((2, tm, d), F32)],
        compiler_params=pltpu.CompilerParams(
            dimension_semantics=("arbitrary", "arbitrary"),
            vmem_limit_bytes=_vmem_limit(tm * tk * 2 + tk * d * 2 + rc * d * 8 + rc * pd * 2 + pd * d * 2,
                                         2 * tm * d * 4 + 6 * rc * d * 4)),
        name="ple_gate_norm_residual",
    )(a, w, x, p, wp, gpost)


def _qk_kernel(h_ref, w_ref, cos_ref, sa_ref, sb_ref, o_ref, *, q_tiles, qscale):
    j = pl.program_id(1)
    w = w_ref[...].astype(BF16)
    sc = jnp.where(j < q_tiles, qscale, 1.0).astype(F32)
    parts = 4
    part = h_ref.shape[0] // parts
    for r in range(parts):
        rows = pl.ds(r * part, part)
        z = _dot(h_ref[rows, :], w)
        c, sa, sb = cos_ref[rows, :], sa_ref[rows, :], sb_ref[rows, :]
        outs = []
        for g in range(z.shape[1] // V7X_LANES):
            zg = z[:, g * V7X_LANES:(g + 1) * V7X_LANES]
            rg = zg * c + pltpu.roll(zg, V7X_LANES - 16, 1) * sa + pltpu.roll(zg, 16, 1) * sb
            outs.append(rg * sc)
        o_ref[rows, :] = jnp.concatenate(outs, axis=1).astype(o_ref.dtype)


def _qk_proj(h, w, cos_t, sa_t, sb_t, *, seq, qk_cols, qscale):
    m, d = h.shape
    n = 2 * qk_cols
    tm = _tile(seq, 1024, V7X_SUBLANES_BF16)
    tn = _tile(qk_cols, 1024, V7X_LANES)
    tpb = seq // tm
    return pl.pallas_call(
        functools.partial(_qk_kernel, q_tiles=qk_cols // tn, qscale=qscale),
        out_shape=jax.ShapeDtypeStruct((m, n), BF16),
        grid=(m // tm, n // tn),
        in_specs=[pl.BlockSpec((tm, d), lambda i, j: (i, 0)),
                  pl.BlockSpec((d, tn), lambda i, j: (0, j)),
                  pl.BlockSpec((tm, V7X_LANES), lambda i, j: (i % tpb, 0)),
                  pl.BlockSpec((tm, V7X_LANES), lambda i, j: (i % tpb, 0)),
                  pl.BlockSpec((tm, V7X_LANES), lambda i, j: (i % tpb, 0))],
        out_specs=pl.BlockSpec((tm, tn), lambda i, j: (i, j)),
        compiler_params=pltpu.CompilerParams(
            dimension_semantics=("arbitrary", "arbitrary"),
            vmem_limit_bytes=_vmem_limit(tm * d * 2 + d * tn * 4 + tm * tn * 2 + 3 * tm * V7X_LANES * 4,
                                         d * tn * 2 + 5 * tm * tn * 4)),
        name="qk_proj_rotary",
    )(h, w, cos_t, sa_t, sb_t)


def _vt_proj_kernel(wt_ref, h_ref, o_ref):
    o_ref[...] = _dot_nt(wt_ref[...], h_ref[...]).astype(o_ref.dtype)


def _vt_proj(h, wt):
    m, d = h.shape
    n = wt.shape[0]
    tm = _tile(m, 1024, V7X_LANES)
    tn = _tile(n, 1024, V7X_SUBLANES_BF16)
    return pl.pallas_call(
        _vt_proj_kernel,
        out_shape=jax.ShapeDtypeStruct((n, m), BF16),
        grid=(m // tm, n // tn),
        in_specs=[pl.BlockSpec((tn, d), lambda i, j: (j, 0)),
                  pl.BlockSpec((tm, d), lambda i, j: (i, 0))],
        out_specs=pl.BlockSpec((tn, tm), lambda i, j: (j, i)),
        compiler_params=pltpu.CompilerParams(
            dimension_semantics=("arbitrary", "arbitrary"),
            vmem_limit_bytes=_vmem_limit(tm * d * 2 + d * tn * 2 + tm * tn * 2, 2 * tm * tn * 4)),
        name="v_proj_transposed",
    )(wt, h)


def _shift_proj_kernel(h_ref, w_ref, mu_ref, o_ref, carry_ref, *, tiles_per_seq):
    i = pl.program_id(0)
    j = pl.program_id(1)

    @pl.when(i == 0)
    def _():
        carry_ref[j] = jnp.zeros(carry_ref.shape[1:], F32)

    z = _dot(h_ref[...], w_ref[...])
    prev_row = jnp.where(i % tiles_per_seq == 0, 0.0, carry_ref[j, 0:1, :])
    row = lax.broadcasted_iota(jnp.int32, z.shape, 0)
    zp = jnp.where(row == 0, prev_row, pltpu.roll(z, 1, 0))
    carry_ref[j, 0:1, :] = z[z.shape[0] - 1:, :]
    o_ref[...] = z + (zp - z) * mu_ref[...]


def _shift_proj(h, w, mu, *, seq):
    m, d = h.shape
    n = w.shape[1]
    tm = _tile(seq, 1024, V7X_SUBLANES_BF16)
    tn = _tile(n, 1024, V7X_LANES)
    return pl.pallas_call(
        functools.partial(_shift_proj_kernel, tiles_per_seq=seq // tm),
        out_shape=jax.ShapeDtypeStruct((m, n), F32),
        grid=(m // tm, n // tn),
        in_specs=[pl.BlockSpec((tm, d), lambda i, j: (i, 0)),
                  pl.BlockSpec((d, tn), lambda i, j: (0, j)),
                  pl.BlockSpec((1, tn), lambda i, j: (0, j))],
        out_specs=pl.BlockSpec((tm, tn), lambda i, j: (i, j)),
        scratch_shapes=[pltpu.VMEM((n // tn, V7X_SUBLANES_F32, tn), F32)],
        compiler_params=pltpu.CompilerParams(
            dimension_semantics=("arbitrary", "arbitrary"),
            vmem_limit_bytes=_vmem_limit(tm * d * 2 + d * tn * 2 + tm * tn * 4, 8 * n * 4 + 4 * tm * tn * 4)),
        name="rwkv_proj_token_shift",
    )(h, w, mu)


def _attn_kernel(q_ref, k_ref, vt_ref, lq1_ref, lk1_ref, lq2_ref, lk2_ref, sg_ref, o_ref,
                 m_ref, l_ref, acc_ref, *, tq, hd, lambda_init):
    qi = pl.program_id(2)
    qs = [q_ref[h * tq:(h + 1) * tq, :] for h in range(2)]
    m_ref[...] = jnp.full(m_ref.shape, -jnp.inf, F32)
    l_ref[...] = jnp.zeros(l_ref.shape, F32)
    acc_ref[...] = jnp.zeros(acc_ref.shape, F32)

    def tile(start, work):
        kt = k_ref[pl.ds(start, tq), :]
        vt = vt_ref[:, pl.ds(start, tq)]
        chains = [(h, c, diag) for h, diag in work for c in range(2)]
        ss = [_dot_nt(kt[:, c * hd:(c + 1) * hd], qs[h][:, c * hd:(c + 1) * hd]) for h, c, _ in chains]
        if any(diag for _, _, diag in chains):
            sh = ATTN_CHUNK.bit_length() - 1
            ck = lax.shift_right_logical(lax.broadcasted_iota(jnp.int32, ss[0].shape, 0), sh)
            rq = lax.shift_right_logical(lax.broadcasted_iota(jnp.int32, ss[0].shape, 1), sh)
            ss = [jnp.where(ck <= rq, s, -jnp.inf) if diag else s for s, (_, _, diag) in zip(ss, chains)]
        m_olds = [m_ref[h, c] for h, c, _ in chains]
        m_news = [jnp.maximum(m_old, jnp.max(s, axis=0, keepdims=True)) for m_old, s in zip(m_olds, ss)]
        ps = [jnp.exp2(s - m_new) for s, m_new in zip(ss, m_news)]
        alphas = [jnp.exp2(m_old - m_new) for m_old, m_new in zip(m_olds, m_news)]
        vt1 = jnp.concatenate([vt, jnp.ones((V7X_SUBLANES_BF16, vt.shape[1]), BF16)], axis=0)
        nv = vt.shape[0]
        pvs = [_dot(vt1, p.astype(BF16)) for p in ps]
        for (h, c, _), alpha, pv, m_new in zip(chains, alphas, pvs, m_news):
            l_ref[h, c] = alpha * l_ref[h, c] + pv[nv:nv + 1]
            acc_ref[h, c] = alpha * acc_ref[h, c] + pv[:nv]
            m_ref[h, c] = m_new

    def body(j, carry):
        tile(pl.multiple_of(j * tq, tq), [(0, False), (1, False)])
        return carry

    lax.fori_loop(0, 2 * qi, body, 0)
    tile(pl.multiple_of(2 * qi * tq, tq), [(0, True), (1, False)])
    tile(pl.multiple_of((2 * qi + 1) * tq, tq), [(1, True)])

    lam = (jnp.exp(jnp.sum(lq1_ref[...] * lk1_ref[...], axis=-1, keepdims=True))
           - jnp.exp(jnp.sum(lq2_ref[...] * lk2_ref[...], axis=-1, keepdims=True)) + lambda_init)
    for h in range(2):
        o = acc_ref[h, 0] / l_ref[h, 0] - lam * (acc_ref[h, 1] / l_ref[h, 1])
        o = o * lax.rsqrt(jnp.mean(o * o, axis=0, keepdims=True) + RMS_EPS) * sg_ref[...] * (1.0 - lambda_init)
        o_ref[h * tq:(h + 1) * tq, :] = o.T.astype(o_ref.dtype)


def _diff_attention(zqk, vt, lq1, lk1, lq2, lk2, subln_g, *, batch, seq, heads, hd, out_cols, lambda_init):
    m = zqk.shape[0]
    vd = 2 * hd
    tq = _tile(seq // 2, 512, V7X_LANES)
    nq = seq // (2 * tq)
    vec = pl.BlockSpec((1, hd), lambda b, h, i: (0, 0))
    return pl.pallas_call(
        functools.partial(_attn_kernel, tq=tq, hd=hd, lambda_init=lambda_init),
        out_shape=jax.ShapeDtypeStruct((m, out_cols), BF16),
        grid=(batch, heads, nq),
        in_specs=[pl.BlockSpec((2 * tq, vd), lambda b, h, i: (b * nq + i, h)),
                  pl.BlockSpec((seq, vd), lambda b, h, i: (b, heads + h)),
                  pl.BlockSpec((vd, seq), lambda b, h, i: (h, b)),
                  vec, vec, vec, vec,
                  pl.BlockSpec((vd, 1), lambda b, h, i: (0, 0))],
        out_specs=pl.BlockSpec((2 * tq, vd), lambda b, h, i: (b * nq + i, h)),
        scratch_shapes=[pltpu.VMEM((2, 2, 1, tq), F32), pltpu.VMEM((2, 2, 1, tq), F32),
                        pltpu.VMEM((2, 2, vd, tq), F32)],
        compiler_params=pltpu.CompilerParams(
            dimension_semantics=("arbitrary", "arbitrary", "arbitrary"),
            vmem_limit_bytes=_vmem_limit(4 * tq * vd * 2 + 2 * seq * vd * 2,
                                         4 * tq * vd * 4 + 20 * tq * tq * 4)),
        name="diff_attention",
    )(zqk, zqk, vt, lq1, lk1, lq2, lk2, subln_g)


def _split3_bf16(x):
    h1 = x.astype(BF16)
    r1 = x - h1.astype(F32)
    h2 = r1.astype(BF16)
    h3 = (r1 - h2.astype(F32)).astype(BF16)
    return h1, h2, h3


def _rwkv_kernel(z_ref, zlc_ref, zln_ref, w0_ref, w2_ref, a0_ref, a2_ref, g2_ref,
                 kk_ref, ka_ref, rk_ref, lnw_ref, lnb_ref, alias_ref, o_ref,
                 st_ref, *side_refs, rw, hn, nc):
    del alias_ref
    c = pl.program_id(1)
    g = pl.program_id(0) * nc + c
    lt = z_ref.shape[0]
    ln = V7X_LANES
    sides = (side_refs[:4], side_refs[4:])

    @pl.when(c == 0)
    def _():
        st_ref[...] = jnp.zeros(st_ref.shape, F32)

    def side_paths(zlm, dst):
        lw_ref, cum_ref, as_ref, gt_ref = dst
        dl = w2_ref.shape[0]
        al = a2_ref.shape[0]
        wpre = w0_ref[...] + _dot(jnp.tanh(zlm[:, :dl]).astype(BF16), w2_ref[...])
        softplus = jnp.maximum(-wpre, 0.0) + jnp.log(1.0 + jnp.exp(-jnp.abs(wpre)))
        lw = -jnp.exp(-softplus - 0.5)
        lw_ref[...] = lw
        tri = (lax.broadcasted_iota(jnp.int32, (lt, lt), 0)
               >= lax.broadcasted_iota(jnp.int32, (lt, lt), 1)).astype(BF16)
        h1, h2, h3 = _split3_bf16(lw)
        cum_ref[...] = _dot(tri, h1) + _dot(tri, h2) + _dot(tri, h3)
        as_ref[...] = jax.nn.sigmoid(a0_ref[...] + _dot(zlm[:, dl:dl + al].astype(BF16), a2_ref[...]))
        gt_ref[...] = _dot(jax.nn.sigmoid(zlm[:, dl + al:]).astype(BF16), g2_ref[...])

    @pl.when(g == 0)
    def _():
        side_paths(zlc_ref[...], sides[0])

    lane = lax.broadcasted_iota(jnp.int32, (lt, ln), 1)
    first = lane < hn
    r2 = lax.broadcasted_iota(jnp.int32, (2 * hn, 2 * hn), 0)
    c2 = lax.broadcasted_iota(jnp.int32, (2 * hn, 2 * hn), 1)
    hsh = hn.bit_length() - 1
    same = lax.shift_right_logical(r2, hsh) == lax.shift_right_logical(c2, hsh)
    strict = same & (r2 > c2)
    incl = same & (r2 >= c2)
    eye = (r2 == c2).astype(F32)
    zeros = jnp.zeros((2 * hn, ln), F32)

    def headsum(x):
        s0 = jnp.sum(jnp.where(first, x, 0.0), axis=-1, keepdims=True)
        s1 = jnp.sum(jnp.where(first, 0.0, x), axis=-1, keepdims=True)
        return jnp.where(first, s0, s1)

    def bd(x):
        return jnp.concatenate([jnp.where(first, x, 0.0), jnp.where(first, 0.0, x)], axis=0)

    def fold(x):
        return x[:hn] + x[hn:]

    def pairs(offs, src):
        lw_ref, cum_ref, as_ref, gt_ref = src
        n2 = 2 * hn
        sls = [pl.ds(off, ln) for off in offs]
        rs = [z_ref[:, pl.ds(off, ln)] for off in offs]
        ks = [z_ref[:, pl.ds(rw + off, ln)] for off in offs]
        vs = [z_ref[:, pl.ds(2 * rw + off, ln)] for off in offs]
        asgs = [as_ref[:, sl] for sl in sls]
        kks = [k * kk_ref[:, sl] for k, sl in zip(ks, sls)]
        kks = [kk / jnp.maximum(jnp.sqrt(headsum(kk * kk)), 1e-12) for kk in kks]
        kns = [k * (1.0 + (asg - 1.0) * ka_ref[:, sl]) for k, asg, sl in zip(ks, asgs, sls)]
        bns = [kk * asg for kk, asg in zip(kks, asgs)]

        lws = [lw_ref[:, sl] for sl in sls]
        cums = [cum_ref[:, sl] for sl in sls]
        tots = [cum[lt - 1:lt, :] for cum in cums]
        e_invs = [jnp.exp(-cum) for cum in cums]
        ats = [bd(-kk * jnp.exp(cum - lw)) for kk, cum, lw in zip(kks, cums, lws)]
        rts = [bd(r * jnp.exp(cum)) for r, cum in zip(rs, cums)]
        vbs = [bd(v) for v in vs]

        fulls = [_dot_nt(jnp.concatenate([at, rt], axis=0).astype(BF16),
                         jnp.concatenate([bd(bn * e), bd(kn * e)], axis=0).astype(BF16))
                 for at, rt, bn, kn, e in zip(ats, rts, bns, kns, e_invs)]
        a_abs = [jnp.where(strict, f[:n2, :n2], 0.0) for f in fulls]
        a_aks = [jnp.where(strict, f[:n2, n2:], 0.0) for f in fulls]
        a_rs = [jnp.concatenate([jnp.where(incl, f[n2:, :n2], 0.0), jnp.where(incl, f[n2:, n2:], 0.0)],
                                axis=1).astype(BF16) for f in fulls]

        akvs = [_dot(a_ak.astype(BF16), vb.astype(BF16)) for a_ak, vb in zip(a_aks, vbs)]
        zqs = [jnp.concatenate([at, akv], axis=1) for at, akv in zip(ats, akvs)]
        apows = a_abs
        steps = max(1, (lt - 1).bit_length())
        for it in range(steps):
            if it + 1 < steps:
                aws = [_dot(ap.astype(BF16), jnp.concatenate([zq, ap], axis=1).astype(BF16))
                       for ap, zq in zip(apows, zqs)]
                zqs = [zq + aw[:, :2 * ln] for zq, aw in zip(zqs, aws)]
                apows = [aw[:, 2 * ln:] for aw in aws]
            else:
                zqs = [zq + _dot(ap.astype(BF16), zq.astype(BF16)) for ap, zq in zip(apows, zqs)]

        rhs2s = [jnp.concatenate([zq, jnp.concatenate([zeros, vb], axis=1)], axis=0).astype(BF16)
                 for zq, vb in zip(zqs, vbs)]
        rys = [_dot(a_r, rhs2) for a_r, rhs2 in zip(a_rs, rhs2s)]
        e_ends = [jnp.exp(tot - cum) for tot, cum in zip(tots, cums)]
        ghs = [_dot_tn(jnp.concatenate([bd(bn * e), bd(kn * e)], axis=0).astype(BF16), rhs2)
               for bn, kn, e, rhs2 in zip(bns, kns, e_ends, rhs2s)]
        lhs4s = [jnp.concatenate([fold(rt + ry[:, :ln]), fold(gh[:, :ln] + eye * jnp.exp(tot))],
                                 axis=0).astype(BF16)
                 for rt, ry, gh, tot in zip(rts, rys, ghs, tots)]
        outs = [_dot(lhs4, bd(st_ref[:, sl]).astype(BF16)) for lhs4, sl in zip(lhs4s, sls)]
        for sl, out, ry, gh, r, kn, v in zip(sls, outs, rys, ghs, rs, kns, vs):
            y = out[:hn] + fold(ry[:, ln:])
            st_ref[:, sl] = out[hn:] + fold(gh[:, ln:])
            mean = headsum(y) * (1.0 / hn)
            yc = y - mean
            var = headsum(yc * yc) * (1.0 / hn)
            yn = yc * lax.rsqrt(var + RW_LN_EPS) * lnw_ref[:, sl] + lnb_ref[:, sl]
            bonus = headsum(r * kn * rk_ref[:, sl]) * v
            o_ref[:, sl] = ((yn + bonus) * gt_ref[:, sl]).astype(o_ref.dtype)

    def step(cur, nxt):
        side_paths(zln_ref[...], nxt)
        pairs([u * ln for u in range(rw // ln)], cur)

    for parity in range(2):
        @pl.when(g % 2 == parity)
        def _():
            step(sides[parity], sides[1 - parity])


def _rwkv(zr, zl, omix, w0, w2, a0, a2, g2, k_k, k_a, r_k, ln_w, ln_b, *, batch, seq, rw, hn):
    m, zc = zr.shape
    lc = zl.shape[1]
    assert hn * 2 == V7X_LANES and RW_CHUNK == hn and seq % RW_CHUNK == 0 and zc == 3 * rw
    lt = RW_CHUNK
    nc = seq // lt
    nsteps = batch * nc
    ocol = omix.shape[1] // rw - 1
    assert (ocol + 1) * rw == omix.shape[1]
    row = lambda n: pl.BlockSpec((1, n), lambda b, c: (0, 0))
    full = lambda a: pl.BlockSpec(a.shape, lambda b, c: (0, 0))
    return pl.pallas_call(
        functools.partial(_rwkv_kernel, rw=rw, hn=hn, nc=nc),
        out_shape=jax.ShapeDtypeStruct(omix.shape, omix.dtype),
        grid=(batch, nc),
        in_specs=[pl.BlockSpec((lt, zc), lambda b, c: (b * nc + c, 0)),
                  pl.BlockSpec((lt, lc), lambda b, c: (b * nc + c, 0)),
                  pl.BlockSpec((lt, lc), lambda b, c: (jnp.minimum(b * nc + c + 1, nsteps - 1), 0)),
                  row(rw), full(w2), row(rw), full(a2), full(g2),
                  row(rw), row(rw), row(rw), row(rw), row(rw),
                  pl.BlockSpec(memory_space=pl.ANY)],
        out_specs=pl.BlockSpec((lt, rw), lambda b, c: (b * nc + c, ocol)),
        scratch_shapes=[pltpu.VMEM((hn, rw), F32)] + [pltpu.VMEM((lt, rw), F32)] * 8,
        input_output_aliases={13: 0},
        compiler_params=pltpu.CompilerParams(
            dimension_semantics=("arbitrary", "arbitrary"),
            vmem_limit_bytes=_vmem_limit(lt * (zc + 2 * lc) * 4 + (w2.size + a2.size + g2.size) * 2 + lt * rw * 2,
                                         9 * lt * rw * 4 + (8 << 20))),
        name="rwkv7_chunked",
    )(zr, zl, zl, w0, w2, a0, a2, g2, k_k, k_a, r_k, ln_w, ln_b, omix)


def _pad_to(a, axis, n):
    pad = n - a.shape[axis]
    if pad == 0:
        return a
    widths = [(0, 0)] * a.ndim
    widths[axis] = (0, pad)
    return jnp.pad(a, widths)


def _ffn(h, x, w_gate, w_up, w_down, gpost, gnext, later_weight):
    act, (wd, later) = _gateup(h, w_gate, w_up, [w_down, later_weight])
    xn, hn = _down(act, wd, x, gpost, gnext, 0.5)
    return xn, hn, later


def kernel(x, p, ffn1_pre_g, ffn1_w_gate, ffn1_w_up, ffn1_w_down, ffn1_post_g, mix_pre_g, w_in, diff_lambda_q1, diff_lambda_k1, diff_lambda_q2, diff_lambda_k2, diff_subln_g, rwkv_mu, rwkv_w0, rwkv_w2, rwkv_a0, rwkv_a2, rwkv_g2, rwkv_k_k, rwkv_k_a, rwkv_r_k, rwkv_ln_w, rwkv_ln_b, w_out, mix_post_g, ffn2_pre_g, ffn2_w_gate, ffn2_w_up, ffn2_w_down, ffn2_post_g, ple_pre_g, ple_w_gate, ple_w_proj, ple_post_g):
    batch, seq, d = x.shape
    depth = p.shape[0]
    m = batch * seq
    hd = diff_subln_g.shape[-1] // 2
    rw = rwkv_w2.shape[-1]
    da = w_out.shape[1] - rw
    heads = da // (2 * hd)
    qk = heads * 2 * hd
    rheads, hn = rwkv_r_k.shape[1], rwkv_r_k.shape[2]
    dl, al, gl = rwkv_w2.shape[1], rwkv_a2.shape[1], rwkv_g2.shape[1]
    dlp, alp, glp = (_round_up(n, V7X_LANES) for n in (dl, al, gl))
    rot = hd // 4
    assert rheads * hn == rw and da == rw and w_in.shape[2] == 2 * qk + da + 3 * rw + dl + al + gl

    inv = ROPE_THETA ** (-jnp.arange(0, rot, 2, dtype=F32) / rot)
    ang = jnp.arange(seq).astype(F32)[:, None] * inv[None, :]
    half = rot // 2
    ones = jnp.ones((seq, hd - rot), F32)
    zeros_h = jnp.zeros((seq, half), F32)
    zeros_r = jnp.zeros((seq, hd - rot), F32)
    cos_t = jnp.concatenate([jnp.cos(ang), jnp.cos(ang), ones], axis=1)
    sa_t = jnp.concatenate([-jnp.sin(ang), zeros_h, zeros_r], axis=1)
    sb_t = jnp.concatenate([zeros_h, jnp.sin(ang), zeros_r], axis=1)
    assert half == 16 and hd == V7X_LANES

    row = lambda a: a.reshape(1, -1).astype(F32)
    xf = x.reshape(m, d)
    for i in range(depth):
        lambda_init = 0.8 - 0.6 * math.exp(-0.3 * i)

        h = _prenorm(xf, row(ffn1_pre_g[i]))
        xf, h, w_o = _ffn(h, xf, ffn1_w_gate[i], ffn1_w_up[i], ffn1_w_down[i],
                          row(ffn1_post_g[i]), row(mix_pre_g[i]), w_out[i])

        wi = w_in[i]
        c0 = 2 * qk + da + 3 * rw
        w_l = jnp.concatenate([_pad_to(wi[:, c0:c0 + dl], 1, dlp),
                               _pad_to(wi[:, c0 + dl:c0 + dl + al], 1, alp),
                               _pad_to(wi[:, c0 + dl + al:], 1, glp)], axis=1).astype(BF16)
        mu = rwkv_mu[i]
        mu_l = jnp.concatenate([_pad_to(mu[3 * rw:3 * rw + dl], 0, dlp),
                                _pad_to(mu[3 * rw + dl:3 * rw + dl + al], 0, alp),
                                _pad_to(mu[3 * rw + dl + al:], 0, glp)])
        zqk = _qk_proj(h, wi[:, :2 * qk].astype(BF16), cos_t, sa_t, sb_t, seq=seq, qk_cols=qk,
                       qscale=hd ** -0.5 * math.log2(math.e))
        vt = _vt_proj(h, wi[:, 2 * qk:2 * qk + da].astype(BF16).T)
        zr = _shift_proj(h, wi[:, 2 * qk + da:c0].astype(BF16), row(mu[:3 * rw]), seq=seq)
        zl = _shift_proj(h, w_l, row(mu_l), seq=seq)

        omix = _diff_attention(zqk, vt, row(diff_lambda_q1[i]), row(diff_lambda_k1[i]),
                               row(diff_lambda_q2[i]), row(diff_lambda_k2[i]),
                               diff_subln_g[i].reshape(-1, 1).astype(F32),
                               batch=batch, seq=seq, heads=heads, hd=hd, out_cols=da + rw,
                               lambda_init=lambda_init)
        omix = _rwkv(zr, zl, omix, row(rwkv_w0[i]),
                     _pad_to(rwkv_w2[i].astype(BF16), 0, dlp), row(rwkv_a0[i]),
                     _pad_to(rwkv_a2[i].astype(BF16), 0, alp), _pad_to(rwkv_g2[i].astype(BF16), 0, glp),
                     row(rwkv_k_k[i]), row(rwkv_k_a[i]), row(rwkv_r_k[i]),
                     row(rwkv_ln_w[i]), row(rwkv_ln_b[i]), batch=batch, seq=seq, rw=rw, hn=hn)

        xf, h = _down(omix, w_o, xf, row(mix_post_g[i]), row(ffn2_pre_g[i]), 1.0)
        xf, h, w_pg = _ffn(h, xf, ffn2_w_gate[i], ffn2_w_up[i], ffn2_w_down[i],
                           row(ffn2_post_g[i]), row(ple_pre_g[i]), ple_w_gate[i])
        xf = _ple(h, w_pg, xf, p[i].reshape(m, -1).astype(BF16),
                  ple_w_proj[i].astype(BF16), row(ple_post_g[i]))
    return xf.reshape(batch, seq, d)
```

```python
import functools
import math

import jax
import jax.numpy as jnp
from jax import lax
from jax.experimental import pallas as pl
from jax.experimental.pallas import tpu as pltpu

F32 = jnp.float32
BF16 = jnp.bfloat16

RMS_EPS = 1e-6
RW_LN_EPS = 64e-5
ATTN_CHUNK = 64
ROPE_THETA = 500000.0
RW_CHUNK = 64

V7X_LANES = 128
V7X_SUBLANES_F32 = 8
V7X_SUBLANES_BF16 = 16
V7X_VMEM_CAP_BYTES = 58 * 1024 * 1024


def _tile(n, pref, align):
    t = (min(n, pref) // align) * align
    while t >= align:
        if n % t == 0:
            return t
        t -= align
    return n


def _round_up(n, m):
    return (n + m - 1) // m * m


def _vmem_limit(pipelined_bytes, resident_bytes):
    need = 2 * pipelined_bytes + resident_bytes + (4 << 20)
    return int(min(V7X_VMEM_CAP_BYTES, max(need, 16 << 20)))


def _rms(x, g, eps=RMS_EPS):
    return x * lax.rsqrt(jnp.mean(x * x, axis=-1, keepdims=True) + eps) * g


def _dot(a, b):
    return jnp.dot(a, b, preferred_element_type=F32)


def _dot_nt(a, b):
    return lax.dot_general(a, b, (((1,), (1,)), ((), ())), preferred_element_type=F32)


def _dot_tn(a, b):
    return lax.dot_general(a, b, (((0,), (0,)), ((), ())), preferred_element_type=F32)


def _prenorm_kernel(x_ref, g_ref, h_ref):
    h_ref[...] = _rms(x_ref[...], g_ref[...]).astype(h_ref.dtype)


def _prenorm(x, g):
    m, d = x.shape
    tm = _tile(m, 256, V7X_SUBLANES_F32)
    return pl.pallas_call(
        _prenorm_kernel,
        out_shape=jax.ShapeDtypeStruct((m, d), BF16),
        grid=(m // tm,),
        in_specs=[pl.BlockSpec((tm, d), lambda i: (i, 0)),
                  pl.BlockSpec((1, d), lambda i: (0, 0))],
        out_specs=pl.BlockSpec((tm, d), lambda i: (i, 0)),
        compiler_params=pltpu.CompilerParams(
            dimension_semantics=("arbitrary",),
            vmem_limit_bytes=_vmem_limit(tm * d * 6, 3 * tm * d * 4)),
        name="prenorm",
    )(x, g)


def _gateup_kernel(h_ref, wg_ref, wu_ref, *refs):
    ncast = len(refs) // 2
    o_ref = refs[ncast]
    h = h_ref[...]
    g = _dot(h, wg_ref[...].astype(BF16))
    u = _dot(h, wu_ref[...].astype(BF16))
    o_ref[...] = (g * jax.nn.sigmoid(g) * u).astype(o_ref.dtype)
    for src_ref, dst_ref in zip(refs[:ncast], refs[ncast + 1:]):
        dst_ref[...] = src_ref[...].astype(dst_ref.dtype)


def _gateup(h, wg, wu, casts):
    m, d = h.shape
    f = wg.shape[1]
    tm = _tile(m, 2048, V7X_SUBLANES_BF16)
    tn = _tile(f, 256, V7X_LANES)
    nj = f // tn
    steps = (m // tm) * nj
    cast_specs = []
    for w in casts:
        slab = _round_up(pl.cdiv(w.shape[0], steps), V7X_SUBLANES_BF16)
        while w.shape[0] % slab:
            slab += V7X_SUBLANES_BF16
        last = w.shape[0] // slab - 1
        cast_specs.append(pl.BlockSpec((slab, w.shape[1]),
                                       lambda i, j, last=last: (jnp.minimum(i * nj + j, last), 0)))
    cast_bytes = sum(sp.block_shape[0] * sp.block_shape[1] * 6 for sp in cast_specs)
    res = pl.pallas_call(
        _gateup_kernel,
        out_shape=(jax.ShapeDtypeStruct((m, f), BF16), *(jax.ShapeDtypeStruct(w.shape, BF16) for w in casts)),
        grid=(m // tm, nj),
        in_specs=[pl.BlockSpec((tm, d), lambda i, j: (i, 0)),
                  pl.BlockSpec((d, tn), lambda i, j: (0, j)),
                  pl.BlockSpec((d, tn), lambda i, j: (0, j)),
                  *cast_specs],
        out_specs=(pl.BlockSpec((tm, tn), lambda i, j: (i, j)), *cast_specs),
        compiler_params=pltpu.CompilerParams(
            dimension_semantics=("arbitrary", "arbitrary"),
            vmem_limit_bytes=_vmem_limit(tm * d * 2 + 2 * d * tn * 4 + tm * tn * 2 + cast_bytes,
                                         2 * d * tn * 2 + 4 * tm * tn * 4)),
        name="swiglu_up",
    )(h, wg, wu, *casts)
    return res[0], res[1:]


def _deferred_rows(a_ref, w_ref, acc_ref, *, ntiles, nk, last, nchunks, epilogue):
    i = pl.program_id(0)
    k = pl.program_id(1)
    tk = a_ref.shape[1]
    cur = acc_ref.at[i % 2]
    prev = acc_ref.at[(i + 1) % 2]
    rc = acc_ref.shape[1] // nchunks
    full_end = nk - (last < tk)
    assert nchunks <= full_end
    mm = i < ntiles
    epi = (i >= 1) & (k < nchunks)

    def run_epilogue():
        epilogue(prev[pl.ds(pl.multiple_of(k * rc, rc), rc), :])

    @pl.when(mm & (k == 0) & (i == 0))
    def _():
        cur[...] = _dot(a_ref[...], w_ref[...])

    @pl.when(mm & (k == 0) & (i >= 1))
    def _():
        run_epilogue()
        cur[...] = _dot(a_ref[...], w_ref[...])

    @pl.when(mm & (k > 0) & (k < full_end) & epi)
    def _():
        run_epilogue()
        cur[...] += _dot(a_ref[...], w_ref[...])

    @pl.when(mm & (k > 0) & (k < full_end) & jnp.logical_not(epi))
    def _():
        cur[...] += _dot(a_ref[...], w_ref[...])

    if last < tk:
        @pl.when(mm & (k == nk - 1))
        def _():
            cur[...] += _dot(a_ref[:, :last], w_ref[:last, :])

    @pl.when(jnp.logical_not(mm) & (k < nchunks))
    def _():
        run_epilogue()


def _deferred_specs(m, kdim, d):
    tm = _tile(m, 1024, V7X_SUBLANES_BF16)
    tk = 512
    ntiles = m // tm
    nk = pl.cdiv(kdim, tk)
    last = kdim - (nk - 1) * tk
    assert last % V7X_LANES == 0 and nk >= 2
    nchunks = 1 << ((nk - (last < tk)).bit_length() - 1)
    nchunks = min(nchunks, tm // V7X_SUBLANES_F32)
    rc = tm // nchunks
    a_spec = pl.BlockSpec((tm, tk), lambda i, k: (jnp.minimum(i, ntiles - 1), jnp.where(i < ntiles, k, nk - 1)))
    w_spec = pl.BlockSpec((tk, d), lambda i, k: (jnp.where(i < ntiles, k, nk - 1), 0))
    chunk = lambda i, k: (jnp.where(i >= 1, (i - 1) * nchunks + jnp.minimum(k, nchunks - 1), 0), 0)
    return tm, tk, ntiles, nk, last, nchunks, rc, a_spec, w_spec, chunk


def _down_kernel(a_ref, w_ref, x_ref, gpost_ref, gnext_ref, o_ref, h_ref, acc_ref, *, scale, **tiling):
    def epilogue(f):
        xn = x_ref[...] + scale * _rms(f, gpost_ref[...])
        o_ref[...] = xn
        h_ref[...] = _rms(xn, gnext_ref[...]).astype(h_ref.dtype)

    _deferred_rows(a_ref, w_ref, acc_ref, epilogue=epilogue, **tiling)


def _down(a, w, x, gpost, gnext, scale):
    m, kdim = a.shape
    d = w.shape[1]
    tm, tk, ntiles, nk, last, nchunks, rc, a_spec, w_spec, chunk = _deferred_specs(m, kdim, d)
    return pl.pallas_call(
        functools.partial(_down_kernel, scale=scale, ntiles=ntiles, nk=nk, last=last, nchunks=nchunks),
        out_shape=(jax.ShapeDtypeStruct((m, d), F32), jax.ShapeDtypeStruct((m, d), BF16)),
        grid=(ntiles + 1, nk),
        in_specs=[a_spec, w_spec,
                  pl.BlockSpec((rc, d), chunk),
                  pl.BlockSpec((1, d), lambda i, k: (0, 0)),
                  pl.BlockSpec((1, d), lambda i, k: (0, 0))],
        out_specs=(pl.BlockSpec((rc, d), chunk), pl.BlockSpec((rc, d), chunk)),
        scratch_shapes=[pltpu.VMEM((2, tm, d), F32)],
        compiler_params=pltpu.CompilerParams(
            dimension_semantics=("arbitrary", "arbitrary"),
            vmem_limit_bytes=_vmem_limit(tm * tk * 2 + tk * d * 2 + rc * d * 10, 2 * tm * d * 4 + 6 * rc * d * 4)),
        name="down_norm_residual",
    )(a, w, x, gpost, gnext)


def _ple_kernel(a_ref, w_ref, x_ref, p_ref, wp_ref, gpost_ref, o_ref, acc_ref, **tiling):
    def epilogue(f):
        proj = _dot(p_ref[...], wp_ref[...])
        o_ref[...] = x_ref[...] + _rms(proj * jax.nn.sigmoid(f), gpost_ref[...])

    _deferred_rows(a_ref, w_ref, acc_ref, epilogue=epilogue, **tiling)


def _ple(a, w, x, p, wp, gpost):
    m, kdim = a.shape
    d = w.shape[1]
    pd = p.shape[1]
    tm, tk, ntiles, nk, last, nchunks, rc, a_spec, w_spec, chunk = _deferred_specs(m, kdim, d)
    return pl.pallas_call(
        functools.partial(_ple_kernel, ntiles=ntiles, nk=nk, last=last, nchunks=nchunks),
        out_shape=jax.ShapeDtypeStruct((m, d), F32),
        grid=(ntiles + 1, nk),
        in_specs=[a_spec, w_spec,
                  pl.BlockSpec((rc, d), chunk),
                  pl.BlockSpec((rc, pd), chunk),
                  pl.BlockSpec((pd, d), lambda i, k: (0, 0)),
                  pl.BlockSpec((1, d), lambda i, k: (0, 0))],
        out_specs=pl.BlockSpec((rc, d), chunk),
        scratch_shapes=[pltpu.VMEM((2, tm, d), F32)],
        compiler_params=pltpu.CompilerParams(
            dimension_semantics=("arbitrary", "arbitrary"),
            vmem_limit_bytes=_vmem_limit(tm * tk * 2 + tk * d * 2 + rc * d * 8 + rc * pd * 2 + pd * d * 2,
                                         2 * tm * d * 4 + 6 * rc * d * 4)),
        name="ple_gate_norm_residual",
    )(a, w, x, p, wp, gpost)


def _qk_kernel(h_ref, w_ref, cos_ref, sa_ref, sb_ref, o_ref, *, q_tiles, qscale):
    j = pl.program_id(1)
    w = w_ref[...].astype(BF16)
    sc = jnp.where(j < q_tiles, qscale, 1.0).astype(F32)
    parts = 4
    part = h_ref.shape[0] // parts
    for r in range(parts):
        rows = pl.ds(r * part, part)
        z = _dot(h_ref[rows, :], w)
        c, sa, sb = cos_ref[rows, :], sa_ref[rows, :], sb_ref[rows, :]
        outs = []
        for g in range(z.shape[1] // V7X_LANES):
            zg = z[:, g * V7X_LANES:(g + 1) * V7X_LANES]
            rg = zg * c + pltpu.roll(zg, V7X_LANES - 16, 1) * sa + pltpu.roll(zg, 16, 1) * sb
            outs.append(rg * sc)
        o_ref[rows, :] = jnp.concatenate(outs, axis=1).astype(o_ref.dtype)


def _qk_proj(h, w, cos_t, sa_t, sb_t, *, seq, qk_cols, qscale):
    m, d = h.shape
    n = 2 * qk_cols
    tm = _tile(seq, 1024, V7X_SUBLANES_BF16)
    tn = _tile(qk_cols, 1024, V7X_LANES)
    tpb = seq // tm
    return pl.pallas_call(
        functools.partial(_qk_kernel, q_tiles=qk_cols // tn, qscale=qscale),
        out_shape=jax.ShapeDtypeStruct((m, n), BF16),
        grid=(m // tm, n // tn),
        in_specs=[pl.BlockSpec((tm, d), lambda i, j: (i, 0)),
                  pl.BlockSpec((d, tn), lambda i, j: (0, j)),
                  pl.BlockSpec((tm, V7X_LANES), lambda i, j: (i % tpb, 0)),
                  pl.BlockSpec((tm, V7X_LANES), lambda i, j: (i % tpb, 0)),
                  pl.BlockSpec((tm, V7X_LANES), lambda i, j: (i % tpb, 0))],
        out_specs=pl.BlockSpec((tm, tn), lambda i, j: (i, j)),
        compiler_params=pltpu.CompilerParams(
            dimension_semantics=("arbitrary", "arbitrary"),
            vmem_limit_bytes=_vmem_limit(tm * d * 2 + d * tn * 4 + tm * tn * 2 + 3 * tm * V7X_LANES * 4,
                                         d * tn * 2 + 5 * tm * tn * 4)),
        name="qk_proj_rotary",
    )(h, w, cos_t, sa_t, sb_t)


def _vt_proj_kernel(wt_ref, h_ref, o_ref):
    o_ref[...] = _dot_nt(wt_ref[...], h_ref[...]).astype(o_ref.dtype)


def _vt_proj(h, wt):
    m, d = h.shape
    n = wt.shape[0]
    tm = _tile(m, 1024, V7X_LANES)
    tn = _tile(n, 1024, V7X_SUBLANES_BF16)
    return pl.pallas_call(
        _vt_proj_kernel,
        out_shape=jax.ShapeDtypeStruct((n, m), BF16),
        grid=(m // tm, n // tn),
        in_specs=[pl.BlockSpec((tn, d), lambda i, j: (j, 0)),
                  pl.BlockSpec((tm, d), lambda i, j: (i, 0))],
        out_specs=pl.BlockSpec((tn, tm), lambda i, j: (j, i)),
        compiler_params=pltpu.CompilerParams(
            dimension_semantics=("arbitrary", "arbitrary"),
            vmem_limit_bytes=_vmem_limit(tm * d * 2 + d * tn * 2 + tm * tn * 2, 2 * tm * tn * 4)),
        name="v_proj_transposed",
    )(wt, h)


def _shift_proj_kernel(h_ref, w_ref, mu_ref, o_ref, carry_ref, *, tiles_per_seq):
    i = pl.program_id(0)
    j = pl.program_id(1)

    @pl.when(i == 0)
    def _():
        carry_ref[j] = jnp.zeros(carry_ref.shape[1:], F32)

    z = _dot(h_ref[...], w_ref[...])
    prev_row = jnp.where(i % tiles_per_seq == 0, 0.0, carry_ref[j, 0:1, :])
    row = lax.broadcasted_iota(jnp.int32, z.shape, 0)
    zp = jnp.where(row == 0, prev_row, pltpu.roll(z, 1, 0))
    carry_ref[j, 0:1, :] = z[z.shape[0] - 1:, :]
    o_ref[...] = z + (zp - z) * mu_ref[...]


def _shift_proj(h, w, mu, *, seq):
    m, d = h.shape
    n = w.shape[1]
    tm = _tile(seq, 1024, V7X_SUBLANES_BF16)
    tn = _tile(n, 1024, V7X_LANES)
    return pl.pallas_call(
        functools.partial(_shift_proj_kernel, tiles_per_seq=seq // tm),
        out_shape=jax.ShapeDtypeStruct((m, n), F32),
        grid=(m // tm, n // tn),
        in_specs=[pl.BlockSpec((tm, d), lambda i, j: (i, 0)),
                  pl.BlockSpec((d, tn), lambda i, j: (0, j)),
                  pl.BlockSpec((1, tn), lambda i, j: (0, j))],
        out_specs=pl.BlockSpec((tm, tn), lambda i, j: (i, j)),
        scratch_shapes=[pltpu.VMEM((n // tn, V7X_SUBLANES_F32, tn), F32)],
        compiler_params=pltpu.CompilerParams(
            dimension_semantics=("arbitrary", "arbitrary"),
            vmem_limit_bytes=_vmem_limit(tm * d * 2 + d * tn * 2 + tm * tn * 4, 8 * n * 4 + 4 * tm * tn * 4)),
        name="rwkv_proj_token_shift",
    )(h, w, mu)


def _attn_kernel(q_ref, k_ref, vt_ref, lq1_ref, lk1_ref, lq2_ref, lk2_ref, sg_ref, o_ref,
                 m_ref, l_ref, acc_ref, *, tq, hd, lambda_init):
    qi = pl.program_id(2)
    qs = [q_ref[h * tq:(h + 1) * tq, :] for h in range(2)]
    m_ref[...] = jnp.full(m_ref.shape, -jnp.inf, F32)
    l_ref[...] = jnp.zeros(l_ref.shape, F32)
    acc_ref[...] = jnp.zeros(acc_ref.shape, F32)

    def tile(start, work):
        kt = k_ref[pl.ds(start, tq), :]
        vt = vt_ref[:, pl.ds(start, tq)]
        chains = [(h, c, diag) for h, diag in work for c in range(2)]
        ss = [_dot_nt(kt[:, c * hd:(c + 1) * hd], qs[h][:, c * hd:(c + 1) * hd]) for h, c, _ in chains]
        if any(diag for _, _, diag in chains):
            sh = ATTN_CHUNK.bit_length() - 1
            ck = lax.shift_right_logical(lax.broadcasted_iota(jnp.int32, ss[0].shape, 0), sh)
            rq = lax.shift_right_logical(lax.broadcasted_iota(jnp.int32, ss[0].shape, 1), sh)
            ss = [jnp.where(ck <= rq, s, -jnp.inf) if diag else s for s, (_, _, diag) in zip(ss, chains)]
        m_olds = [m_ref[h, c] for h, c, _ in chains]
        m_news = [jnp.maximum(m_old, jnp.max(s, axis=0, keepdims=True)) for m_old, s in zip(m_olds, ss)]
        ps = [jnp.exp2(s - m_new) for s, m_new in zip(ss, m_news)]
        alphas = [jnp.exp2(m_old - m_new) for m_old, m_new in zip(m_olds, m_news)]
        vt1 = jnp.concatenate([vt, jnp.ones((V7X_SUBLANES_BF16, vt.shape[1]), BF16)], axis=0)
        nv = vt.shape[0]
        pvs = [_dot(vt1, p.astype(BF16)) for p in ps]
        for (h, c, _), alpha, pv, m_new in zip(chains, alphas, pvs, m_news):
            l_ref[h, c] = alpha * l_ref[h, c] + pv[nv:nv + 1]
            acc_ref[h, c] = alpha * acc_ref[h, c] + pv[:nv]
            m_ref[h, c] = m_new

    def body(j, carry):
        tile(pl.multiple_of(j * tq, tq), [(0, False), (1, False)])
        return carry

    lax.fori_loop(0, 2 * qi, body, 0)
    tile(pl.multiple_of(2 * qi * tq, tq), [(0, True), (1, False)])
    tile(pl.multiple_of((2 * qi + 1) * tq, tq), [(1, True)])

    lam = (jnp.exp(jnp.sum(lq1_ref[...] * lk1_ref[...], axis=-1, keepdims=True))
           - jnp.exp(jnp.sum(lq2_ref[...] * lk2_ref[...], axis=-1, keepdims=True)) + lambda_init)
    for h in range(2):
        o = acc_ref[h, 0] / l_ref[h, 0] - lam * (acc_ref[h, 1] / l_ref[h, 1])
        o = o * lax.rsqrt(jnp.mean(o * o, axis=0, keepdims=True) + RMS_EPS) * sg_ref[...] * (1.0 - lambda_init)
        o_ref[h * tq:(h + 1) * tq, :] = o.T.astype(o_ref.dtype)


def _diff_attention(zqk, vt, lq1, lk1, lq2, lk2, subln_g, *, batch, seq, heads, hd, out_cols, lambda_init):
    m = zqk.shape[0]
    vd = 2 * hd
    tq = _tile(seq // 2, 512, V7X_LANES)
    nq = seq // (2 * tq)
    vec = pl.BlockSpec((1, hd), lambda b, h, i: (0, 0))
    return pl.pallas_call(
        functools.partial(_attn_kernel, tq=tq, hd=hd, lambda_init=lambda_init),
        out_shape=jax.ShapeDtypeStruct((m, out_cols), BF16),
        grid=(batch, heads, nq),
        in_specs=[pl.BlockSpec((2 * tq, vd), lambda b, h, i: (b * nq + i, h)),
                  pl.BlockSpec((seq, vd), lambda b, h, i: (b, heads + h)),
                  pl.BlockSpec((vd, seq), lambda b, h, i: (h, b)),
                  vec, vec, vec, vec,
                  pl.BlockSpec((vd, 1), lambda b, h, i: (0, 0))],
        out_specs=pl.BlockSpec((2 * tq, vd), lambda b, h, i: (b * nq + i, h)),
        scratch_shapes=[pltpu.VMEM((2, 2, 1, tq), F32), pltpu.VMEM((2, 2, 1, tq), F32),
                        pltpu.VMEM((2, 2, vd, tq), F32)],
        compiler_params=pltpu.CompilerParams(
            dimension_semantics=("arbitrary", "arbitrary", "arbitrary"),
            vmem_limit_bytes=_vmem_limit(4 * tq * vd * 2 + 2 * seq * vd * 2,
                                         4 * tq * vd * 4 + 20 * tq * tq * 4)),
        name="diff_attention",
    )(zqk, zqk, vt, lq1, lk1, lq2, lk2, subln_g)


def _split3_bf16(x):
    h1 = x.astype(BF16)
    r1 = x - h1.astype(F32)
    h2 = r1.astype(BF16)
    h3 = (r1 - h2.astype(F32)).astype(BF16)
    return h1, h2, h3


def _rwkv_kernel(z_ref, zlc_ref, zln_ref, w0_ref, w2_ref, a0_ref, a2_ref, g2_ref,
                 kk_ref, ka_ref, rk_ref, lnw_ref, lnb_ref, alias_ref, o_ref,
                 st_ref, *side_refs, rw, hn, nc):
    del alias_ref
    c = pl.program_id(1)
    g = pl.program_id(0) * nc + c
    lt = z_ref.shape[0]
    ln = V7X_LANES
    sides = (side_refs[:4], side_refs[4:])

    @pl.when(c == 0)
    def _():
        st_ref[...] = jnp.zeros(st_ref.shape, F32)

    def side_paths(zlm, dst):
        lw_ref, cum_ref, as_ref, gt_ref = dst
        dl = w2_ref.shape[0]
        al = a2_ref.shape[0]
        wpre = w0_ref[...] + _dot(jnp.tanh(zlm[:, :dl]).astype(BF16), w2_ref[...])
        softplus = jnp.maximum(-wpre, 0.0) + jnp.log(1.0 + jnp.exp(-jnp.abs(wpre)))
        lw = -jnp.exp(-softplus - 0.5)
        lw_ref[...] = lw
        tri = (lax.broadcasted_iota(jnp.int32, (lt, lt), 0)
               >= lax.broadcasted_iota(jnp.int32, (lt, lt), 1)).astype(BF16)
        h1, h2, h3 = _split3_bf16(lw)
        cum_ref[...] = _dot(tri, h1) + _dot(tri, h2) + _dot(tri, h3)
        as_ref[...] = jax.nn.sigmoid(a0_ref[...] + _dot(zlm[:, dl:dl + al].astype(BF16), a2_ref[...]))
        gt_ref[...] = _dot(jax.nn.sigmoid(zlm[:, dl + al:]).astype(BF16), g2_ref[...])

    @pl.when(g == 0)
    def _():
        side_paths(zlc_ref[...], sides[0])

    lane = lax.broadcasted_iota(jnp.int32, (lt, ln), 1)
    first = lane < hn
    r2 = lax.broadcasted_iota(jnp.int32, (2 * hn, 2 * hn), 0)
    c2 = lax.broadcasted_iota(jnp.int32, (2 * hn, 2 * hn), 1)
    hsh = hn.bit_length() - 1
    same = lax.shift_right_logical(r2, hsh) == lax.shift_right_logical(c2, hsh)
    strict = same & (r2 > c2)
    incl = same & (r2 >= c2)
    eye = (r2 == c2).astype(F32)
    zeros = jnp.zeros((2 * hn, ln), F32)

    def headsum(x):
        s0 = jnp.sum(jnp.where(first, x, 0.0), axis=-1, keepdims=True)
        s1 = jnp.sum(jnp.where(first, 0.0, x), axis=-1, keepdims=True)
        return jnp.where(first, s0, s1)

    def bd(x):
        return jnp.concatenate([jnp.where(first, x, 0.0), jnp.where(first, 0.0, x)], axis=0)

    def fold(x):
        return x[:hn] + x[hn:]

    def pairs(offs, src):
        lw_ref, cum_ref, as_ref, gt_ref = src
        n2 = 2 * hn
        sls = [pl.ds(off, ln) for off in offs]
        rs = [z_ref[:, pl.ds(off, ln)] for off in offs]
        ks = [z_ref[:, pl.ds(rw + off, ln)] for off in offs]
        vs = [z_ref[:, pl.ds(2 * rw + off, ln)] for off in offs]
        asgs = [as_ref[:, sl] for sl in sls]
        kks = [k * kk_ref[:, sl] for k, sl in zip(ks, sls)]
        kks = [kk / jnp.maximum(jnp.sqrt(headsum(kk * kk)), 1e-12) for kk in kks]
        kns = [k * (1.0 + (asg - 1.0) * ka_ref[:, sl]) for k, asg, sl in zip(ks, asgs, sls)]
        bns = [kk * asg for kk, asg in zip(kks, asgs)]

        lws = [lw_ref[:, sl] for sl in sls]
        cums = [cum_ref[:, sl] for sl in sls]
        tots = [cum[lt - 1:lt, :] for cum in cums]
        e_invs = [jnp.exp(-cum) for cum in cums]
        ats = [bd(-kk * jnp.exp(cum - lw)) for kk, cum, lw in zip(kks, cums, lws)]
        rts = [bd(r * jnp.exp(cum)) for r, cum in zip(rs, cums)]
        vbs = [bd(v) for v in vs]

        fulls = [_dot_nt(jnp.concatenate([at, rt], axis=0).astype(BF16),
                         jnp.concatenate([bd(bn * e), bd(kn * e)], axis=0).astype(BF16))
                 for at, rt, bn, kn, e in zip(ats, rts, bns, kns, e_invs)]
        a_abs = [jnp.where(strict, f[:n2, :n2], 0.0) for f in fulls]
        a_aks = [jnp.where(strict, f[:n2, n2:], 0.0) for f in fulls]
        a_rs = [jnp.concatenate([jnp.where(incl, f[n2:, :n2], 0.0), jnp.where(incl, f[n2:, n2:], 0.0)],
                                axis=1).astype(BF16) for f in fulls]

        akvs = [_dot(a_ak.astype(BF16), vb.astype(BF16)) for a_ak, vb in zip(a_aks, vbs)]
        zqs = [jnp.concatenate([at, akv], axis=1) for at, akv in zip(ats, akvs)]
        apows = a_abs
        steps = max(1, (lt - 1).bit_length())
        for it in range(steps):
            if it + 1 < steps:
                aws = [_dot(ap.astype(BF16), jnp.concatenate([zq, ap], axis=1).astype(BF16))
                       for ap, zq in zip(apows, zqs)]
                zqs = [zq + aw[:, :2 * ln] for zq, aw in zip(zqs, aws)]
                apows = [aw[:, 2 * ln:] for aw in aws]
            else:
                zqs = [zq + _dot(ap.astype(BF16), zq.astype(BF16)) for ap, zq in zip(apows, zqs)]

        rhs2s = [jnp.concatenate([zq, jnp.concatenate([zeros, vb], axis=1)], axis=0).astype(BF16)
                 for zq, vb in zip(zqs, vbs)]
        rys = [_dot(a_r, rhs2) for a_r, rhs2 in zip(a_rs, rhs2s)]
        e_ends = [jnp.exp(tot - cum) for tot, cum in zip(tots, cums)]
        ghs = [_dot_tn(jnp.concatenate([bd(bn * e), bd(kn * e)], axis=0).astype(BF16), rhs2)
               for bn, kn, e, rhs2 in zip(bns, kns, e_ends, rhs2s)]
        lhs4s = [jnp.concatenate([fold(rt + ry[:, :ln]), fold(gh[:, :ln] + eye * jnp.exp(tot))],
                                 axis=0).astype(BF16)
                 for rt, ry, gh, tot in zip(rts, rys, ghs, tots)]
        outs = [_dot(lhs4, bd(st_ref[:, sl]).astype(BF16)) for lhs4, sl in zip(lhs4s, sls)]
        for sl, out, ry, gh, r, kn, v in zip(sls, outs, rys, ghs, rs, kns, vs):
            y = out[:hn] + fold(ry[:, ln:])
            st_ref[:, sl] = out[hn:] + fold(gh[:, ln:])
            mean = headsum(y) * (1.0 / hn)
            yc = y - mean
            var = headsum(yc * yc) * (1.0 / hn)
            yn = yc * lax.rsqrt(var + RW_LN_EPS) * lnw_ref[:, sl] + lnb_ref[:, sl]
            bonus = headsum(r * kn * rk_ref[:, sl]) * v
            o_ref[:, sl] = ((yn + bonus) * gt_ref[:, sl]).astype(o_ref.dtype)

    def step(cur, nxt):
        side_paths(zln_ref[...], nxt)
        pairs([u * ln for u in range(rw // ln)], cur)

    for parity in range(2):
        @pl.when(g % 2 == parity)
        def _():
            step(sides[parity], sides[1 - parity])


def _rwkv(zr, zl, omix, w0, w2, a0, a2, g2, k_k, k_a, r_k, ln_w, ln_b, *, batch, seq, rw, hn):
    m, zc = zr.shape
    lc = zl.shape[1]
    assert hn * 2 == V7X_LANES and RW_CHUNK == hn and seq % RW_CHUNK == 0 and zc == 3 * rw
    lt = RW_CHUNK
    nc = seq // lt
    nsteps = batch * nc
    ocol = omix.shape[1] // rw - 1
    assert (ocol + 1) * rw == omix.shape[1]
    row = lambda n: pl.BlockSpec((1, n), lambda b, c: (0, 0))
    full = lambda a: pl.BlockSpec(a.shape, lambda b, c: (0, 0))
    return pl.pallas_call(
        functools.partial(_rwkv_kernel, rw=rw, hn=hn, nc=nc),
        out_shape=jax.ShapeDtypeStruct(omix.shape, omix.dtype),
        grid=(batch, nc),
        in_specs=[pl.BlockSpec((lt, zc), lambda b, c: (b * nc + c, 0)),
                  pl.BlockSpec((lt, lc), lambda b, c: (b * nc + c, 0)),
                  pl.BlockSpec((lt, lc), lambda b, c: (jnp.minimum(b * nc + c + 1, nsteps - 1), 0)),
                  row(rw), full(w2), row(rw), full(a2), full(g2),
                  row(rw), row(rw), row(rw), row(rw), row(rw),
                  pl.BlockSpec(memory_space=pl.ANY)],
        out_specs=pl.BlockSpec((lt, rw), lambda b, c: (b * nc + c, ocol)),
        scratch_shapes=[pltpu.VMEM((hn, rw), F32)] + [pltpu.VMEM((lt, rw), F32)] * 8,
        input_output_aliases={13: 0},
        compiler_params=pltpu.CompilerParams(
            dimension_semantics=("arbitrary", "arbitrary"),
            vmem_limit_bytes=_vmem_limit(lt * (zc + 2 * lc) * 4 + (w2.size + a2.size + g2.size) * 2 + lt * rw * 2,
                                         9 * lt * rw * 4 + (8 << 20))),
        name="rwkv7_chunked",
    )(zr, zl, zl, w0, w2, a0, a2, g2, k_k, k_a, r_k, ln_w, ln_b, omix)


def _pad_to(a, axis, n):
    pad = n - a.shape[axis]
    if pad == 0:
        return a
    widths = [(0, 0)] * a.ndim
    widths[axis] = (0, pad)
    return jnp.pad(a, widths)


def _ffn(h, x, w_gate, w_up, w_down, gpost, gnext, later_weight):
    act, (wd, later) = _gateup(h, w_gate, w_up, [w_down, later_weight])
    xn, hn = _down(act, wd, x, gpost, gnext, 0.5)
    return xn, hn, later


def kernel(x, p, ffn1_pre_g, ffn1_w_gate, ffn1_w_up, ffn1_w_down, ffn1_post_g, mix_pre_g, w_in, diff_lambda_q1, diff_lambda_k1, diff_lambda_q2, diff_lambda_k2, diff_subln_g, rwkv_mu, rwkv_w0, rwkv_w2, rwkv_a0, rwkv_a2, rwkv_g2, rwkv_k_k, rwkv_k_a, rwkv_r_k, rwkv_ln_w, rwkv_ln_b, w_out, mix_post_g, ffn2_pre_g, ffn2_w_gate, ffn2_w_up, ffn2_w_down, ffn2_post_g, ple_pre_g, ple_w_gate, ple_w_proj, ple_post_g):
    batch, seq, d = x.shape
    depth = p.shape[0]
    m = batch * seq
    hd = diff_subln_g.shape[-1] // 2
    rw = rwkv_w2.shape[-1]
    da = w_out.shape[1] - rw
    heads = da // (2 * hd)
    qk = heads * 2 * hd
    rheads, hn = rwkv_r_k.shape[1], rwkv_r_k.shape[2]
    dl, al, gl = rwkv_w2.shape[1], rwkv_a2.shape[1], rwkv_g2.shape[1]
    dlp, alp, glp = (_round_up(n, V7X_LANES) for n in (dl, al, gl))
    rot = hd // 4
    assert rheads * hn == rw and da == rw and w_in.shape[2] == 2 * qk + da + 3 * rw + dl + al + gl

    inv = ROPE_THETA ** (-jnp.arange(0, rot, 2, dtype=F32) / rot)
    ang = jnp.arange(seq).astype(F32)[:, None] * inv[None, :]
    half = rot // 2
    ones = jnp.ones((seq, hd - rot), F32)
    zeros_h = jnp.zeros((seq, half), F32)
    zeros_r = jnp.zeros((seq, hd - rot), F32)
    cos_t = jnp.concatenate([jnp.cos(ang), jnp.cos(ang), ones], axis=1)
    sa_t = jnp.concatenate([-jnp.sin(ang), zeros_h, zeros_r], axis=1)
    sb_t = jnp.concatenate([zeros_h, jnp.sin(ang), zeros_r], axis=1)
    assert half == 16 and hd == V7X_LANES

    row = lambda a: a.reshape(1, -1).astype(F32)
    xf = x.reshape(m, d)
    for i in range(depth):
        lambda_init = 0.8 - 0.6 * math.exp(-0.3 * i)

        h = _prenorm(xf, row(ffn1_pre_g[i]))
        xf, h, w_o = _ffn(h, xf, ffn1_w_gate[i], ffn1_w_up[i], ffn1_w_down[i],
                          row(ffn1_post_g[i]), row(mix_pre_g[i]), w_out[i])

        wi = w_in[i]
        c0 = 2 * qk + da + 3 * rw
        w_l = jnp.concatenate([_pad_to(wi[:, c0:c0 + dl], 1, dlp),
                               _pad_to(wi[:, c0 + dl:c0 + dl + al], 1, alp),
                               _pad_to(wi[:, c0 + dl + al:], 1, glp)], axis=1).astype(BF16)
        mu = rwkv_mu[i]
        mu_l = jnp.concatenate([_pad_to(mu[3 * rw:3 * rw + dl], 0, dlp),
                                _pad_to(mu[3 * rw + dl:3 * rw + dl + al], 0, alp),
                                _pad_to(mu[3 * rw + dl + al:], 0, glp)])
        zqk = _qk_proj(h, wi[:, :2 * qk].astype(BF16), cos_t, sa_t, sb_t, seq=seq, qk_cols=qk,
                       qscale=hd ** -0.5 * math.log2(math.e))
        vt = _vt_proj(h, wi[:, 2 * qk:2 * qk + da].astype(BF16).T)
        zr = _shift_proj(h, wi[:, 2 * qk + da:c0].astype(BF16), row(mu[:3 * rw]), seq=seq)
        zl = _shift_proj(h, w_l, row(mu_l), seq=seq)

        omix = _diff_attention(zqk, vt, row(diff_lambda_q1[i]), row(diff_lambda_k1[i]),
                               row(diff_lambda_q2[i]), row(diff_lambda_k2[i]),
                               diff_subln_g[i].reshape(-1, 1).astype(F32),
                               batch=batch, seq=seq, heads=heads, hd=hd, out_cols=da + rw,
                               lambda_init=lambda_init)
        omix = _rwkv(zr, zl, omix, row(rwkv_w0[i]),
                     _pad_to(rwkv_w2[i].astype(BF16), 0, dlp), row(rwkv_a0[i]),
                     _pad_to(rwkv_a2[i].astype(BF16), 0, alp), _pad_to(rwkv_g2[i].astype(BF16), 0, glp),
                     row(rwkv_k_k[i]), row(rwkv_k_a[i]), row(rwkv_r_k[i]),
                     row(rwkv_ln_w[i]), row(rwkv_ln_b[i]), batch=batch, seq=seq, rw=rw, hn=hn)

        xf, h = _down(omix, w_o, xf, row(mix_post_g[i]), row(ffn2_pre_g[i]), 1.0)
        xf, h, w_pg = _ffn(h, xf, ffn2_w_gate[i], ffn2_w_up[i], ffn2_w_down[i],
                           row(ffn2_post_g[i]), row(ple_pre_g[i]), ple_w_gate[i])
        xf = _ple(h, w_pg, xf, p[i].reshape(m, -1).astype(BF16),
                  ple_w_proj[i].astype(BF16), row(ple_post_g[i]))
    return xf.reshape(batch, seq, d)
```

```python
import functools
import math

import jax
import jax.numpy as jnp
from jax import lax
from jax.experimental import pallas as pl
from jax.experimental.pallas import tpu as pltpu

F32 = jnp.float32
BF16 = jnp.bfloat16

RMS_EPS = 1e-6
RW_LN_EPS = 64e-5
ATTN_CHUNK = 64
ROPE_THETA = 500000.0
RW_CHUNK = 64

V7X_LANES = 128
V7X_SUBLANES_F32 = 8
V7X_SUBLANES_BF16 = 16
V7X_VMEM_CAP_BYTES = 58 * 1024 * 1024


def _tile(n, pref, align):
    t = (min(n, pref) // align) * align
    while t >= align:
        if n % t == 0:
            return t
        t -= align
    return n


def _round_up(n, m):
    return (n + m - 1) // m * m


def _vmem_limit(pipelined_bytes, resident_bytes):
    need = 2 * pipelined_bytes + resident_bytes + (4 << 20)
    return int(min(V7X_VMEM_CAP_BYTES, max(need, 16 << 20)))


def _rms(x, g, eps=RMS_EPS):
    return x * lax.rsqrt(jnp.mean(x * x, axis=-1, keepdims=True) + eps) * g


def _dot(a, b):
    return jnp.dot(a, b, preferred_element_type=F32)


def _dot_nt(a, b):
    return lax.dot_general(a, b, (((1,), (1,)), ((), ())), preferred_element_type=F32)


def _dot_tn(a, b):
    return lax.dot_general(a, b, (((0,), (0,)), ((), ())), preferred_element_type=F32)


def _prenorm_kernel(x_ref, g_ref, h_ref):
    h_ref[...] = _rms(x_ref[...], g_ref[...]).astype(h_ref.dtype)


def _prenorm(x, g):
    m, d = x.shape
    tm = _tile(m, 256, V7X_SUBLANES_F32)
    return pl.pallas_call(
        _prenorm_kernel,
        out_shape=jax.ShapeDtypeStruct((m, d), BF16),
        grid=(m // tm,),
        in_specs=[pl.BlockSpec((tm, d), lambda i: (i, 0)),
                  pl.BlockSpec((1, d), lambda i: (0, 0))],
        out_specs=pl.BlockSpec((tm, d), lambda i: (i, 0)),
        compiler_params=pltpu.CompilerParams(
            dimension_semantics=("arbitrary",),
            vmem_limit_bytes=_vmem_limit(tm * d * 6, 3 * tm * d * 4)),
        name="prenorm",
    )(x, g)


def _gateup_kernel(h_ref, wg_ref, wu_ref, *refs):
    ncast = len(refs) // 2
    o_ref = refs[ncast]
    h = h_ref[...]
    g = _dot(h, wg_ref[...].astype(BF16))
    u = _dot(h, wu_ref[...].astype(BF16))
    o_ref[...] = (g * jax.nn.sigmoid(g) * u).astype(o_ref.dtype)
    for src_ref, dst_ref in zip(refs[:ncast], refs[ncast + 1:]):
        dst_ref[...] = src_ref[...].astype(dst_ref.dtype)


def _gateup(h, wg, wu, casts):
    m, d = h.shape
    f = wg.shape[1]
    tm = _tile(m, 2048, V7X_SUBLANES_BF16)
    tn = _tile(f, 256, V7X_LANES)
    nj = f // tn
    steps = (m // tm) * nj
    cast_specs = []
    for w in casts:
        slab = _round_up(pl.cdiv(w.shape[0], steps), V7X_SUBLANES_BF16)
        while w.shape[0] % slab:
            slab += V7X_SUBLANES_BF16
        last = w.shape[0] // slab - 1
        cast_specs.append(pl.BlockSpec((slab, w.shape[1]),
                                       lambda i, j, last=last: (jnp.minimum(i * nj + j, last), 0)))
    cast_bytes = sum(sp.block_shape[0] * sp.block_shape[1] * 6 for sp in cast_specs)
    res = pl.pallas_call(
        _gateup_kernel,
        out_shape=(jax.ShapeDtypeStruct((m, f), BF16), *(jax.ShapeDtypeStruct(w.shape, BF16) for w in casts)),
        grid=(m // tm, nj),
        in_specs=[pl.BlockSpec((tm, d), lambda i, j: (i, 0)),
                  pl.BlockSpec((d, tn), lambda i, j: (0, j)),
                  pl.BlockSpec((d, tn), lambda i, j: (0, j)),
                  *cast_specs],
        out_specs=(pl.BlockSpec((tm, tn), lambda i, j: (i, j)), *cast_specs),
        compiler_params=pltpu.CompilerParams(
            dimension_semantics=("arbitrary", "arbitrary"),
            vmem_limit_bytes=_vmem_limit(tm * d * 2 + 2 * d * tn * 4 + tm * tn * 2 + cast_bytes,
                                         2 * d * tn * 2 + 4 * tm * tn * 4)),
        name="swiglu_up",
    )(h, wg, wu, *casts)
    return res[0], res[1:]


def _deferred_rows(a_refs, w_ref, acc_ref, *, ntiles, nk, last, nchunks, epilogue):
    i = pl.program_id(0)
    k = pl.program_id(1)
    tk = a_refs[0].shape[1]
    per = nk // len(a_refs)

    def a_tile():
        tile = a_refs[-1][...]
        for n in range(len(a_refs) - 2, -1, -1):
            tile = jnp.where(k < (n + 1) * per, a_refs[n][...], tile)
        return tile

    cur = acc_ref.at[i % 2]
    prev = acc_ref.at[(i + 1) % 2]
    rc = acc_ref.shape[1] // nchunks
    full_end = nk - (last < tk)
    assert nchunks <= full_end
    mm = i < ntiles
    epi = (i >= 1) & (k < nchunks)

    def run_epilogue():
        epilogue(prev[pl.ds(pl.multiple_of(k * rc, rc), rc), :])

    @pl.when(mm & (k == 0) & (i == 0))
    def _():
        cur[...] = _dot(a_tile(), w_ref[...])

    @pl.when(mm & (k == 0) & (i >= 1))
    def _():
        run_epilogue()
        cur[...] = _dot(a_tile(), w_ref[...])

    @pl.when(mm & (k > 0) & (k < full_end) & epi)
    def _():
        run_epilogue()
        cur[...] += _dot(a_tile(), w_ref[...])

    @pl.when(mm & (k > 0) & (k < full_end) & jnp.logical_not(epi))
    def _():
        cur[...] += _dot(a_tile(), w_ref[...])

    if last < tk:
        @pl.when(mm & (k == nk - 1))
        def _():
            cur[...] += _dot(a_refs[0][:, :last], w_ref[:last, :])

    @pl.when(jnp.logical_not(mm) & (k < nchunks))
    def _():
        run_epilogue()


def _deferred_specs(m, kdim, d, nsrc=1):
    tm = _tile(m, 1024, V7X_SUBLANES_BF16)
    tk = 512
    ntiles = m // tm
    nk = pl.cdiv(kdim, tk)
    last = kdim - (nk - 1) * tk
    assert last % V7X_LANES == 0 and nk >= 2
    nchunks = 1 << ((nk - (last < tk)).bit_length() - 1)
    nchunks = min(nchunks, tm // V7X_SUBLANES_F32)
    rc = tm // nchunks
    per = nk // nsrc
    assert per * nsrc == nk and (nsrc == 1 or last == tk)
    kstep = lambda i, k: jnp.where(i < ntiles, k, nk - 1)
    a_specs = [pl.BlockSpec((tm, tk), lambda i, k, n=n: (jnp.minimum(i, ntiles - 1),
                                                         jnp.clip(kstep(i, k) - n * per, 0, per - 1)))
               for n in range(nsrc)]
    w_spec = pl.BlockSpec((tk, d), lambda i, k: (kstep(i, k), 0))
    chunk = lambda i, k: (jnp.where(i >= 1, (i - 1) * nchunks + jnp.minimum(k, nchunks - 1), 0), 0)
    return tm, tk, ntiles, nk, last, nchunks, rc, a_specs, w_spec, chunk


def _down_kernel(*refs, scale, nsrc, **tiling):
    a_refs = refs[:nsrc]
    w_ref, x_ref, gpost_ref, gnext_ref, o_ref, h_ref, acc_ref = refs[nsrc:]

    def epilogue(f):
        xn = x_ref[...] + scale * _rms(f, gpost_ref[...])
        o_ref[...] = xn
        h_ref[...] = _rms(xn, gnext_ref[...]).astype(h_ref.dtype)

    _deferred_rows(a_refs, w_ref, acc_ref, epilogue=epilogue, **tiling)


def _down(a_parts, w, x, gpost, gnext, scale):
    m = a_parts[0].shape[0]
    kdim, d = w.shape
    tm, tk, ntiles, nk, last, nchunks, rc, a_specs, w_spec, chunk = _deferred_specs(m, kdim, d, len(a_parts))
    return pl.pallas_call(
        functools.partial(_down_kernel, scale=scale, nsrc=len(a_parts),
                          ntiles=ntiles, nk=nk, last=last, nchunks=nchunks),
        out_shape=(jax.ShapeDtypeStruct((m, d), F32), jax.ShapeDtypeStruct((m, d), BF16)),
        grid=(ntiles + 1, nk),
        in_specs=[*a_specs, w_spec,
                  pl.BlockSpec((rc, d), chunk),
                  pl.BlockSpec((1, d), lambda i, k: (0, 0)),
                  pl.BlockSpec((1, d), lambda i, k: (0, 0))],
        out_specs=(pl.BlockSpec((rc, d), chunk), pl.BlockSpec((rc, d), chunk)),
        scratch_shapes=[pltpu.VMEM((2, tm, d), F32)],
        compiler_params=pltpu.CompilerParams(
            dimension_semantics=("arbitrary", "arbitrary"),
            vmem_limit_bytes=_vmem_limit(len(a_parts) * tm * tk * 2 + tk * d * 2 + rc * d * 10,
                                         2 * tm * d * 4 + 6 * rc * d * 4)),
        name="down_norm_residual",
    )(*a_parts, w, x, gpost, gnext)


def _ple_kernel(a_ref, w_ref, x_ref, p_ref, wp_ref, gpost_ref, o_ref, acc_ref, **tiling):
    def epilogue(f):
        proj = _dot(p_ref[...], wp_ref[...])
        o_ref[...] = x_ref[...] + _rms(proj * jax.nn.sigmoid(f), gpost_ref[...])

    _deferred_rows([a_ref], w_ref, acc_ref, epilogue=epilogue, **tiling)


def _ple(a, w, x, p, wp, gpost):
    m, kdim = a.shape
    d = w.shape[1]
    pd = p.shape[1]
    tm, tk, ntiles, nk, last, nchunks, rc, (a_spec,), w_spec, chunk = _deferred_specs(m, kdim, d)
    return pl.pallas_call(
        functools.partial(_ple_kernel, ntiles=ntiles, nk=nk, last=last, nchunks=nchunks),
        out_shape=jax.ShapeDtypeStruct((m, d), F32),
        grid=(ntiles + 1, nk),
        in_specs=[a_spec, w_spec,
                  pl.BlockSpec((rc, d), chunk),
                  pl.BlockSpec((rc, pd), chunk),
                  pl.BlockSpec((pd, d), lambda i, k: (0, 0)),
                  pl.BlockSpec((1, d), lambda i, k: (0, 0))],
        out_specs=pl.BlockSpec((rc, d), chunk),
        scratch_shapes=[pltpu.VMEM((2, tm, d), F32)],
        compiler_params=pltpu.CompilerParams(
            dimension_semantics=("arbitrary", "arbitrary"),
            vmem_limit_bytes=_vmem_limit(tm * tk * 2 + tk * d * 2 + rc * d * 8 + rc * pd * 2 + pd * d * 2,
                                         2 * tm * d * 4 + 6 * rc * d * 4)),
        name="ple_gate_norm_residual",
    )(a, w, x, p, wp, gpost)


def _qk_kernel(h_ref, w_ref, cos_ref, sa_ref, sb_ref, o_ref, *, q_tiles, qscale):
    j = pl.program_id(1)
    w = w_ref[...].astype(BF16)
    sc = jnp.where(j < q_tiles, qscale, 1.0).astype(F32)
    parts = 4
    part = h_ref.shape[0] // parts
    for r in range(parts):
        rows = pl.ds(r * part, part)
        z = _dot(h_ref[rows, :], w)
        c, sa, sb = cos_ref[rows, :], sa_ref[rows, :], sb_ref[rows, :]
        outs = []
        for g in range(z.shape[1] // V7X_LANES):
            zg = z[:, g * V7X_LANES:(g + 1) * V7X_LANES]
            rg = zg * c + pltpu.roll(zg, V7X_LANES - 16, 1) * sa + pltpu.roll(zg, 16, 1) * sb
            outs.append(rg * sc)
        o_ref[rows, :] = jnp.concatenate(outs, axis=1).astype(o_ref.dtype)


def _qk_proj(h, w, cos_t, sa_t, sb_t, *, seq, qk_cols, qscale):
    m, d = h.shape
    n = 2 * qk_cols
    tm = _tile(seq, 1024, V7X_SUBLANES_BF16)
    tn = _tile(qk_cols, 1024, V7X_LANES)
    tpb = seq // tm
    return pl.pallas_call(
        functools.partial(_qk_kernel, q_tiles=qk_cols // tn, qscale=qscale),
        out_shape=jax.ShapeDtypeStruct((m, n), BF16),
        grid=(m // tm, n // tn),
        in_specs=[pl.BlockSpec((tm, d), lambda i, j: (i, 0)),
                  pl.BlockSpec((d, tn), lambda i, j: (0, j)),
                  pl.BlockSpec((tm, V7X_LANES), lambda i, j: (i % tpb, 0)),
                  pl.BlockSpec((tm, V7X_LANES), lambda i, j: (i % tpb, 0)),
                  pl.BlockSpec((tm, V7X_LANES), lambda i, j: (i % tpb, 0))],
        out_specs=pl.BlockSpec((tm, tn), lambda i, j: (i, j)),
        compiler_params=pltpu.CompilerParams(
            dimension_semantics=("arbitrary", "arbitrary"),
            vmem_limit_bytes=_vmem_limit(tm * d * 2 + d * tn * 4 + tm * tn * 2 + 3 * tm * V7X_LANES * 4,
                                         d * tn * 2 + 5 * tm * tn * 4)),
        name="qk_proj_rotary",
    )(h, w, cos_t, sa_t, sb_t)


def _vt_proj_kernel(wt_ref, h_ref, o_ref):
    o_ref[...] = _dot_nt(wt_ref[...], h_ref[...]).astype(o_ref.dtype)


def _vt_proj(h, wt):
    m, d = h.shape
    n = wt.shape[0]
    tm = _tile(m, 1024, V7X_LANES)
    tn = _tile(n, 1024, V7X_SUBLANES_BF16)
    return pl.pallas_call(
        _vt_proj_kernel,
        out_shape=jax.ShapeDtypeStruct((n, m), BF16),
        grid=(m // tm, n // tn),
        in_specs=[pl.BlockSpec((tn, d), lambda i, j: (j, 0)),
                  pl.BlockSpec((tm, d), lambda i, j: (i, 0))],
        out_specs=pl.BlockSpec((tn, tm), lambda i, j: (j, i)),
        compiler_params=pltpu.CompilerParams(
            dimension_semantics=("arbitrary", "arbitrary"),
            vmem_limit_bytes=_vmem_limit(tm * d * 2 + d * tn * 2 + tm * tn * 2, 2 * tm * tn * 4)),
        name="v_proj_transposed",
    )(wt, h)


def _shift_proj_kernel(h_ref, w_ref, mu_ref, o_ref, carry_ref, *, tiles_per_seq):
    i = pl.program_id(0)
    j = pl.program_id(1)

    @pl.when(i == 0)
    def _():
        carry_ref[j] = jnp.zeros(carry_ref.shape[1:], F32)

    z = _dot(h_ref[...], w_ref[...])
    prev_row = jnp.where(i % tiles_per_seq == 0, 0.0, carry_ref[j, 0:1, :])
    row = lax.broadcasted_iota(jnp.int32, z.shape, 0)
    zp = jnp.where(row == 0, prev_row, pltpu.roll(z, 1, 0))
    carry_ref[j, 0:1, :] = z[z.shape[0] - 1:, :]
    o_ref[...] = z + (zp - z) * mu_ref[...]


def _shift_proj(h, w, mu, *, seq):
    m, d = h.shape
    n = w.shape[1]
    tm = _tile(seq, 1024, V7X_SUBLANES_BF16)
    tn = _tile(n, 1024, V7X_LANES)
    return pl.pallas_call(
        functools.partial(_shift_proj_kernel, tiles_per_seq=seq // tm),
        out_shape=jax.ShapeDtypeStruct((m, n), F32),
        grid=(m // tm, n // tn),
        in_specs=[pl.BlockSpec((tm, d), lambda i, j: (i, 0)),
                  pl.BlockSpec((d, tn), lambda i, j: (0, j)),
                  pl.BlockSpec((1, tn), lambda i, j: (0, j))],
        out_specs=pl.BlockSpec((tm, tn), lambda i, j: (i, j)),
        scratch_shapes=[pltpu.VMEM((n // tn, V7X_SUBLANES_F32, tn), F32)],
        compiler_params=pltpu.CompilerParams(
            dimension_semantics=("arbitrary", "arbitrary"),
            vmem_limit_bytes=_vmem_limit(tm * d * 2 + d * tn * 2 + tm * tn * 4, 8 * n * 4 + 4 * tm * tn * 4)),
        name="rwkv_proj_token_shift",
    )(h, w, mu)


def _attn_kernel(q_ref, k_ref, vt_ref, lq1_ref, lk1_ref, lq2_ref, lk2_ref, sg_ref, o_ref,
                 m_ref, l_ref, acc_ref, *, tq, hd, lambda_init):
    qi = pl.program_id(2)
    qs = [q_ref[h * tq:(h + 1) * tq, :] for h in range(2)]
    m_ref[...] = jnp.full(m_ref.shape, -jnp.inf, F32)
    l_ref[...] = jnp.zeros(l_ref.shape, F32)
    acc_ref[...] = jnp.zeros(acc_ref.shape, F32)

    def tile(start, work):
        kt = k_ref[pl.ds(start, tq), :]
        vt = vt_ref[:, pl.ds(start, tq)]
        chains = [(h, c, diag) for h, diag in work for c in range(2)]
        ss = [_dot_nt(kt[:, c * hd:(c + 1) * hd], qs[h][:, c * hd:(c + 1) * hd]) for h, c, _ in chains]
        if any(diag for _, _, diag in chains):
            sh = ATTN_CHUNK.bit_length() - 1
            ck = lax.shift_right_logical(lax.broadcasted_iota(jnp.int32, ss[0].shape, 0), sh)
            rq = lax.shift_right_logical(lax.broadcasted_iota(jnp.int32, ss[0].shape, 1), sh)
            ss = [jnp.where(ck <= rq, s, -jnp.inf) if diag else s for s, (_, _, diag) in zip(ss, chains)]
        m_olds = [m_ref[h, c] for h, c, _ in chains]
        m_news = [jnp.maximum(m_old, jnp.max(s, axis=0, keepdims=True)) for m_old, s in zip(m_olds, ss)]
        ps = [jnp.exp2(s - m_new) for s, m_new in zip(ss, m_news)]
        alphas = [jnp.exp2(m_old - m_new) for m_old, m_new in zip(m_olds, m_news)]
        vt1 = jnp.concatenate([vt, jnp.ones((V7X_SUBLANES_BF16, vt.shape[1]), BF16)], axis=0)
        nv = vt.shape[0]
        pvs = [_dot(vt1, p.astype(BF16)) for p in ps]
        for (h, c, _), alpha, pv, m_new in zip(chains, alphas, pvs, m_news):
            l_ref[h, c] = alpha * l_ref[h, c] + pv[nv:nv + 1]
            acc_ref[h, c] = alpha * acc_ref[h, c] + pv[:nv]
            m_ref[h, c] = m_new

    def body(j, carry):
        tile(pl.multiple_of(j * tq, tq), [(0, False), (1, False)])
        return carry

    lax.fori_loop(0, 2 * qi, body, 0)
    tile(pl.multiple_of(2 * qi * tq, tq), [(0, True), (1, False)])
    tile(pl.multiple_of((2 * qi + 1) * tq, tq), [(1, True)])

    lam = (jnp.exp(jnp.sum(lq1_ref[...] * lk1_ref[...], axis=-1, keepdims=True))
           - jnp.exp(jnp.sum(lq2_ref[...] * lk2_ref[...], axis=-1, keepdims=True)) + lambda_init)
    for h in range(2):
        o = acc_ref[h, 0] / l_ref[h, 0] - lam * (acc_ref[h, 1] / l_ref[h, 1])
        o = o * lax.rsqrt(jnp.mean(o * o, axis=0, keepdims=True) + RMS_EPS) * sg_ref[...] * (1.0 - lambda_init)
        o_ref[h * tq:(h + 1) * tq, :] = o.T.astype(o_ref.dtype)


def _diff_attention(zqk, vt, lq1, lk1, lq2, lk2, subln_g, *, batch, seq, heads, hd, lambda_init):
    m = zqk.shape[0]
    vd = 2 * hd
    tq = _tile(seq // 2, 512, V7X_LANES)
    nq = seq // (2 * tq)
    vec = pl.BlockSpec((1, hd), lambda b, h, i: (0, 0))
    return pl.pallas_call(
        functools.partial(_attn_kernel, tq=tq, hd=hd, lambda_init=lambda_init),
        out_shape=jax.ShapeDtypeStruct((m, heads * vd), BF16),
        grid=(batch, heads, nq),
        in_specs=[pl.BlockSpec((2 * tq, vd), lambda b, h, i: (b * nq + i, h)),
                  pl.BlockSpec((seq, vd), lambda b, h, i: (b, heads + h)),
                  pl.BlockSpec((vd, seq), lambda b, h, i: (h, b)),
                  vec, vec, vec, vec,
                  pl.BlockSpec((vd, 1), lambda b, h, i: (0, 0))],
        out_specs=pl.BlockSpec((2 * tq, vd), lambda b, h, i: (b * nq + i, h)),
        scratch_shapes=[pltpu.VMEM((2, 2, 1, tq), F32), pltpu.VMEM((2, 2, 1, tq), F32),
                        pltpu.VMEM((2, 2, vd, tq), F32)],
        compiler_params=pltpu.CompilerParams(
            dimension_semantics=("arbitrary", "arbitrary", "arbitrary"),
            vmem_limit_bytes=_vmem_limit(4 * tq * vd * 2 + 2 * seq * vd * 2,
                                         4 * tq * vd * 4 + 20 * tq * tq * 4)),
        name="diff_attention",
    )(zqk, zqk, vt, lq1, lk1, lq2, lk2, subln_g)


def _split3_bf16(x):
    h1 = x.astype(BF16)
    r1 = x - h1.astype(F32)
    h2 = r1.astype(BF16)
    h3 = (r1 - h2.astype(F32)).astype(BF16)
    return h1, h2, h3


def _rwkv_kernel(z_ref, zlc_ref, zln_ref, w0_ref, w2_ref, a0_ref, a2_ref, g2_ref,
                 kk_ref, ka_ref, rk_ref, lnw_ref, lnb_ref, o_ref,
                 st_ref, *side_refs, rw, hn, nc):
    c = pl.program_id(1)
    g = pl.program_id(0) * nc + c
    lt = z_ref.shape[0]
    ln = V7X_LANES
    sides = (side_refs[:4], side_refs[4:])

    @pl.when(c == 0)
    def _():
        st_ref[...] = jnp.zeros(st_ref.shape, F32)

    def side_paths(zlm, dst):
        lw_ref, cum_ref, as_ref, gt_ref = dst
        dl = w2_ref.shape[0]
        al = a2_ref.shape[0]
        wpre = w0_ref[...] + _dot(jnp.tanh(zlm[:, :dl]).astype(BF16), w2_ref[...])
        softplus = jnp.maximum(-wpre, 0.0) + jnp.log(1.0 + jnp.exp(-jnp.abs(wpre)))
        lw = -jnp.exp(-softplus - 0.5)
        lw_ref[...] = lw
        tri = (lax.broadcasted_iota(jnp.int32, (lt, lt), 0)
               >= lax.broadcasted_iota(jnp.int32, (lt, lt), 1)).astype(BF16)
        h1, h2, h3 = _split3_bf16(lw)
        cum_ref[...] = _dot(tri, h1) + _dot(tri, h2) + _dot(tri, h3)
        as_ref[...] = jax.nn.sigmoid(a0_ref[...] + _dot(zlm[:, dl:dl + al].astype(BF16), a2_ref[...]))
        gt_ref[...] = _dot(jax.nn.sigmoid(zlm[:, dl + al:]).astype(BF16), g2_ref[...])

    @pl.when(g == 0)
    def _():
        side_paths(zlc_ref[...], sides[0])

    lane = lax.broadcasted_iota(jnp.int32, (lt, ln), 1)
    first = lane < hn
    r2 = lax.broadcasted_iota(jnp.int32, (2 * hn, 2 * hn), 0)
    c2 = lax.broadcasted_iota(jnp.int32, (2 * hn, 2 * hn), 1)
    hsh = hn.bit_length() - 1
    same = lax.shift_right_logical(r2, hsh) == lax.shift_right_logical(c2, hsh)
    strict = same & (r2 > c2)
    incl = same & (r2 >= c2)
    eye = (r2 == c2).astype(F32)
    zeros = jnp.zeros((2 * hn, ln), F32)

    def headsum(x):
        s0 = jnp.sum(jnp.where(first, x, 0.0), axis=-1, keepdims=True)
        s1 = jnp.sum(jnp.where(first, 0.0, x), axis=-1, keepdims=True)
        return jnp.where(first, s0, s1)

    def bd(x):
        return jnp.concatenate([jnp.where(first, x, 0.0), jnp.where(first, 0.0, x)], axis=0)

    def fold(x):
        return x[:hn] + x[hn:]

    def pairs(offs, src):
        lw_ref, cum_ref, as_ref, gt_ref = src
        n2 = 2 * hn
        sls = [pl.ds(off, ln) for off in offs]
        rs = [z_ref[:, pl.ds(off, ln)] for off in offs]
        ks = [z_ref[:, pl.ds(rw + off, ln)] for off in offs]
        vs = [z_ref[:, pl.ds(2 * rw + off, ln)] for off in offs]
        asgs = [as_ref[:, sl] for sl in sls]
        kks = [k * kk_ref[:, sl] for k, sl in zip(ks, sls)]
        kks = [kk / jnp.maximum(jnp.sqrt(headsum(kk * kk)), 1e-12) for kk in kks]
        kns = [k * (1.0 + (asg - 1.0) * ka_ref[:, sl]) for k, asg, sl in zip(ks, asgs, sls)]
        bns = [kk * asg for kk, asg in zip(kks, asgs)]

        lws = [lw_ref[:, sl] for sl in sls]
        cums = [cum_ref[:, sl] for sl in sls]
        tots = [cum[lt - 1:lt, :] for cum in cums]
        e_invs = [jnp.exp(-cum) for cum in cums]
        ats = [bd(-kk * jnp.exp(cum - lw)) for kk, cum, lw in zip(kks, cums, lws)]
        rts = [bd(r * jnp.exp(cum)) for r, cum in zip(rs, cums)]
        vbs = [bd(v) for v in vs]

        fulls = [_dot_nt(jnp.concatenate([at, rt], axis=0).astype(BF16),
                         jnp.concatenate([bd(bn * e), bd(kn * e)], axis=0).astype(BF16))
                 for at, rt, bn, kn, e in zip(ats, rts, bns, kns, e_invs)]
        a_abs = [jnp.where(strict, f[:n2, :n2], 0.0) for f in fulls]
        a_aks = [jnp.where(strict, f[:n2, n2:], 0.0) for f in fulls]
        a_rs = [jnp.concatenate([jnp.where(incl, f[n2:, :n2], 0.0), jnp.where(incl, f[n2:, n2:], 0.0)],
                                axis=1).astype(BF16) for f in fulls]

        akvs = [_dot(a_ak.astype(BF16), vb.astype(BF16)) for a_ak, vb in zip(a_aks, vbs)]
        zqs = [jnp.concatenate([at, akv], axis=1) for at, akv in zip(ats, akvs)]
        apows = a_abs
        steps = max(1, (lt - 1).bit_length())
        for it in range(steps):
            if it + 1 < steps:
                aws = [_dot(ap.astype(BF16), jnp.concatenate([zq, ap], axis=1).astype(BF16))
                       for ap, zq in zip(apows, zqs)]
                zqs = [zq + aw[:, :2 * ln] for zq, aw in zip(zqs, aws)]
                apows = [aw[:, 2 * ln:] for aw in aws]
            else:
                zqs = [zq + _dot(ap.astype(BF16), zq.astype(BF16)) for ap, zq in zip(apows, zqs)]

        rhs2s = [jnp.concatenate([zq, jnp.concatenate([zeros, vb], axis=1)], axis=0).astype(BF16)
                 for zq, vb in zip(zqs, vbs)]
        rys = [_dot(a_r, rhs2) for a_r, rhs2 in zip(a_rs, rhs2s)]
        e_ends = [jnp.exp(tot - cum) for tot, cum in zip(tots, cums)]
        ghs = [_dot_tn(jnp.concatenate([bd(bn * e), bd(kn * e)], axis=0).astype(BF16), rhs2)
               for bn, kn, e, rhs2 in zip(bns, kns, e_ends, rhs2s)]
        lhs4s = [jnp.concatenate([fold(rt + ry[:, :ln]), fold(gh[:, :ln] + eye * jnp.exp(tot))],
                                 axis=0).astype(BF16)
                 for rt, ry, gh, tot in zip(rts, rys, ghs, tots)]
        outs = [_dot(lhs4, bd(st_ref[:, sl]).astype(BF16)) for lhs4, sl in zip(lhs4s, sls)]
        for sl, out, ry, gh, r, kn, v in zip(sls, outs, rys, ghs, rs, kns, vs):
            y = out[:hn] + fold(ry[:, ln:])
            st_ref[:, sl] = out[hn:] + fold(gh[:, ln:])
            mean = headsum(y) * (1.0 / hn)
            yc = y - mean
            var = headsum(yc * yc) * (1.0 / hn)
            yn = yc * lax.rsqrt(var + RW_LN_EPS) * lnw_ref[:, sl] + lnb_ref[:, sl]
            bonus = headsum(r * kn * rk_ref[:, sl]) * v
            o_ref[:, sl] = ((yn + bonus) * gt_ref[:, sl]).astype(o_ref.dtype)

    def step(cur, nxt):
        side_paths(zln_ref[...], nxt)
        pairs([u * ln for u in range(rw // ln)], cur)

    for parity in range(2):
        @pl.when(g % 2 == parity)
        def _():
            step(sides[parity], sides[1 - parity])


def _rwkv(zr, zl, w0, w2, a0, a2, g2, k_k, k_a, r_k, ln_w, ln_b, *, batch, seq, rw, hn):
    m, zc = zr.shape
    lc = zl.shape[1]
    assert hn * 2 == V7X_LANES and RW_CHUNK == hn and seq % RW_CHUNK == 0 and zc == 3 * rw
    lt = RW_CHUNK
    nc = seq // lt
    nsteps = batch * nc
    row = lambda n: pl.BlockSpec((1, n), lambda b, c: (0, 0))
    full = lambda a: pl.BlockSpec(a.shape, lambda b, c: (0, 0))
    return pl.pallas_call(
        functools.partial(_rwkv_kernel, rw=rw, hn=hn, nc=nc),
        out_shape=jax.ShapeDtypeStruct((m, rw), BF16),
        grid=(batch, nc),
        in_specs=[pl.BlockSpec((lt, zc), lambda b, c: (b * nc + c, 0)),
                  pl.BlockSpec((lt, lc), lambda b, c: (b * nc + c, 0)),
                  pl.BlockSpec((lt, lc), lambda b, c: (jnp.minimum(b * nc + c + 1, nsteps - 1), 0)),
                  row(rw), full(w2), row(rw), full(a2), full(g2),
                  row(rw), row(rw), row(rw), row(rw), row(rw)],
        out_specs=pl.BlockSpec((lt, rw), lambda b, c: (b * nc + c, 0)),
        scratch_shapes=[pltpu.VMEM((hn, rw), F32)] + [pltpu.VMEM((lt, rw), F32)] * 8,
        compiler_params=pltpu.CompilerParams(
            dimension_semantics=("arbitrary", "arbitrary"),
            vmem_limit_bytes=_vmem_limit(lt * (zc + 2 * lc) * 4 + (w2.size + a2.size + g2.size) * 2 + lt * rw * 2,
                                         9 * lt * rw * 4 + (8 << 20))),
        name="rwkv7_chunked",
    )(zr, zl, zl, w0, w2, a0, a2, g2, k_k, k_a, r_k, ln_w, ln_b)


def _pad_to(a, axis, n):
    pad = n - a.shape[axis]
    if pad == 0:
        return a
    widths = [(0, 0)] * a.ndim
    widths[axis] = (0, pad)
    return jnp.pad(a, widths)


def _ffn(h, x, w_gate, w_up, w_down, gpost, gnext, later_weight):
    act, (wd, later) = _gateup(h, w_gate, w_up, [w_down, later_weight])
    xn, hn = _down([act], wd, x, gpost, gnext, 0.5)
    return xn, hn, later


def kernel(x, p, ffn1_pre_g, ffn1_w_gate, ffn1_w_up, ffn1_w_down, ffn1_post_g, mix_pre_g, w_in, diff_lambda_q1, diff_lambda_k1, diff_lambda_q2, diff_lambda_k2, diff_subln_g, rwkv_mu, rwkv_w0, rwkv_w2, rwkv_a0, rwkv_a2, rwkv_g2, rwkv_k_k, rwkv_k_a, rwkv_r_k, rwkv_ln_w, rwkv_ln_b, w_out, mix_post_g, ffn2_pre_g, ffn2_w_gate, ffn2_w_up, ffn2_w_down, ffn2_post_g, ple_pre_g, ple_w_gate, ple_w_proj, ple_post_g):
    batch, seq, d = x.shape
    depth = p.shape[0]
    m = batch * seq
    hd = diff_subln_g.shape[-1] // 2
    rw = rwkv_w2.shape[-1]
    da = w_out.shape[1] - rw
    heads = da // (2 * hd)
    qk = heads * 2 * hd
    rheads, hn = rwkv_r_k.shape[1], rwkv_r_k.shape[2]
    dl, al, gl = rwkv_w2.shape[1], rwkv_a2.shape[1], rwkv_g2.shape[1]
    dlp, alp, glp = (_round_up(n, V7X_LANES) for n in (dl, al, gl))
    rot = hd // 4
    assert rheads * hn == rw and da == rw and w_in.shape[2] == 2 * qk + da + 3 * rw + dl + al + gl

    inv = ROPE_THETA ** (-jnp.arange(0, rot, 2, dtype=F32) / rot)
    ang = jnp.arange(seq).astype(F32)[:, None] * inv[None, :]
    half = rot // 2
    ones = jnp.ones((seq, hd - rot), F32)
    zeros_h = jnp.zeros((seq, half), F32)
    zeros_r = jnp.zeros((seq, hd - rot), F32)
    cos_t = jnp.concatenate([jnp.cos(ang), jnp.cos(ang), ones], axis=1)
    sa_t = jnp.concatenate([-jnp.sin(ang), zeros_h, zeros_r], axis=1)
    sb_t = jnp.concatenate([zeros_h, jnp.sin(ang), zeros_r], axis=1)
    assert half == 16 and hd == V7X_LANES

    row = lambda a: a.reshape(1, -1).astype(F32)
    xf = x.reshape(m, d)
    for i in range(depth):
        lambda_init = 0.8 - 0.6 * math.exp(-0.3 * i)

        h = _prenorm(xf, row(ffn1_pre_g[i]))
        xf, h, w_o = _ffn(h, xf, ffn1_w_gate[i], ffn1_w_up[i], ffn1_w_down[i],
                          row(ffn1_post_g[i]), row(mix_pre_g[i]), w_out[i])

        wi = w_in[i]
        c0 = 2 * qk + da + 3 * rw
        w_l = jnp.concatenate([_pad_to(wi[:, c0:c0 + dl], 1, dlp),
                               _pad_to(wi[:, c0 + dl:c0 + dl + al], 1, alp),
                               _pad_to(wi[:, c0 + dl + al:], 1, glp)], axis=1).astype(BF16)
        mu = rwkv_mu[i]
        mu_l = jnp.concatenate([_pad_to(mu[3 * rw:3 * rw + dl], 0, dlp),
                                _pad_to(mu[3 * rw + dl:3 * rw + dl + al], 0, alp),
                                _pad_to(mu[3 * rw + dl + al:], 0, glp)])
        zqk = _qk_proj(h, wi[:, :2 * qk].astype(BF16), cos_t, sa_t, sb_t, seq=seq, qk_cols=qk,
                       qscale=hd ** -0.5 * math.log2(math.e))
        vt = _vt_proj(h, wi[:, 2 * qk:2 * qk + da].astype(BF16).T)
        zr = _shift_proj(h, wi[:, 2 * qk + da:c0].astype(BF16), row(mu[:3 * rw]), seq=seq)
        zl = _shift_proj(h, w_l, row(mu_l), seq=seq)

        o_diff = _diff_attention(zqk, vt, row(diff_lambda_q1[i]), row(diff_lambda_k1[i]),
                                 row(diff_lambda_q2[i]), row(diff_lambda_k2[i]),
                                 diff_subln_g[i].reshape(-1, 1).astype(F32),
                                 batch=batch, seq=seq, heads=heads, hd=hd, lambda_init=lambda_init)
        o_rwkv = _rwkv(zr, zl, row(rwkv_w0[i]),
                     _pad_to(rwkv_w2[i].astype(BF16), 0, dlp), row(rwkv_a0[i]),
                     _pad_to(rwkv_a2[i].astype(BF16), 0, alp), _pad_to(rwkv_g2[i].astype(BF16), 0, glp),
                     row(rwkv_k_k[i]), row(rwkv_k_a[i]), row(rwkv_r_k[i]),
                     row(rwkv_ln_w[i]), row(rwkv_ln_b[i]), batch=batch, seq=seq, rw=rw, hn=hn)

        xf, h = _down([o_diff, o_rwkv], w_o, xf, row(mix_post_g[i]), row(ffn2_pre_g[i]), 1.0)
        xf, h, w_pg = _ffn(h, xf, ffn2_w_gate[i], ffn2_w_up[i], ffn2_w_down[i],
                           row(ffn2_post_g[i]), row(ple_pre_g[i]), ple_w_gate[i])
        xf = _ple(h, w_pg, xf, p[i].reshape(m, -1).astype(BF16),
                  ple_w_proj[i].astype(BF16), row(ple_post_g[i]))
    return xf.reshape(batch, seq, d)
```

```python
import functools
import math

import jax
import jax.numpy as jnp
from jax import lax
from jax.experimental import pallas as pl
from jax.experimental.pallas import tpu as pltpu

F32 = jnp.float32
BF16 = jnp.bfloat16

RMS_EPS = 1e-6
RW_LN_EPS = 64e-5
ATTN_CHUNK = 64
ROPE_THETA = 500000.0
RW_CHUNK = 64

V7X_LANES = 128
V7X_SUBLANES_F32 = 8
V7X_SUBLANES_BF16 = 16
V7X_VMEM_CAP_BYTES = 58 * 1024 * 1024


def _tile(n, pref, align):
    t = (min(n, pref) // align) * align
    while t >= align:
        if n % t == 0:
            return t
        t -= align
    return n


def _round_up(n, m):
    return (n + m - 1) // m * m


def _vmem_limit(pipelined_bytes, resident_bytes):
    need = 2 * pipelined_bytes + resident_bytes + (4 << 20)
    return int(min(V7X_VMEM_CAP_BYTES, max(need, 16 << 20)))


def _rms(x, g, eps=RMS_EPS):
    return x * lax.rsqrt(jnp.mean(x * x, axis=-1, keepdims=True) + eps) * g


def _dot(a, b):
    return jnp.dot(a, b, preferred_element_type=F32)


def _dot_nt(a, b):
    return lax.dot_general(a, b, (((1,), (1,)), ((), ())), preferred_element_type=F32)


def _dot_tn(a, b):
    return lax.dot_general(a, b, (((0,), (0,)), ((), ())), preferred_element_type=F32)


def _prenorm_kernel(x_ref, g_ref, h_ref):
    h_ref[...] = _rms(x_ref[...], g_ref[...]).astype(h_ref.dtype)


def _prenorm(x, g):
    m, d = x.shape
    tm = _tile(m, 256, V7X_SUBLANES_F32)
    return pl.pallas_call(
        _prenorm_kernel,
        out_shape=jax.ShapeDtypeStruct((m, d), BF16),
        grid=(m // tm,),
        in_specs=[pl.BlockSpec((tm, d), lambda i: (i, 0)),
                  pl.BlockSpec((1, d), lambda i: (0, 0))],
        out_specs=pl.BlockSpec((tm, d), lambda i: (i, 0)),
        compiler_params=pltpu.CompilerParams(
            dimension_semantics=("arbitrary",),
            vmem_limit_bytes=_vmem_limit(tm * d * 6, 3 * tm * d * 4)),
        name="prenorm",
    )(x, g)


def _gateup_kernel(h_ref, wg_ref, wu_ref, *refs):
    ncast = len(refs) // 2
    o_ref = refs[ncast]
    wg = wg_ref[...].astype(BF16)
    wu = wu_ref[...].astype(BF16)
    parts = 4
    part = h_ref.shape[0] // parts
    for r in range(parts):
        rows = pl.ds(r * part, part)
        h = h_ref[rows, :]
        g = _dot(h, wg)
        u = _dot(h, wu)
        o_ref[rows, :] = (g * jax.nn.sigmoid(g) * u).astype(o_ref.dtype)
    for src_ref, dst_ref in zip(refs[:ncast], refs[ncast + 1:]):
        dst_ref[...] = src_ref[...].astype(dst_ref.dtype)


def _gateup(h, wg, wu, casts):
    m, d = h.shape
    f = wg.shape[1]
    tm = _tile(m, 2048, V7X_SUBLANES_BF16)
    tn = _tile(f, 256, V7X_LANES)
    nj = f // tn
    steps = (m // tm) * nj
    cast_specs = []
    for w in casts:
        slab = _round_up(pl.cdiv(w.shape[0], steps), V7X_SUBLANES_BF16)
        while w.shape[0] % slab:
            slab += V7X_SUBLANES_BF16
        last = w.shape[0] // slab - 1
        cast_specs.append(pl.BlockSpec((slab, w.shape[1]),
                                       lambda i, j, last=last: (jnp.minimum(i * nj + j, last), 0)))
    cast_bytes = sum(sp.block_shape[0] * sp.block_shape[1] * 6 for sp in cast_specs)
    res = pl.pallas_call(
        _gateup_kernel,
        out_shape=(jax.ShapeDtypeStruct((m, f), BF16), *(jax.ShapeDtypeStruct(w.shape, BF16) for w in casts)),
        grid=(m // tm, nj),
        in_specs=[pl.BlockSpec((tm, d), lambda i, j: (i, 0)),
                  pl.BlockSpec((d, tn), lambda i, j: (0, j)),
                  pl.BlockSpec((d, tn), lambda i, j: (0, j)),
                  *cast_specs],
        out_specs=(pl.BlockSpec((tm, tn), lambda i, j: (i, j)), *cast_specs),
        compiler_params=pltpu.CompilerParams(
            dimension_semantics=("arbitrary", "arbitrary"),
            vmem_limit_bytes=_vmem_limit(tm * d * 2 + 2 * d * tn * 4 + tm * tn * 2 + cast_bytes,
                                         2 * d * tn * 2 + 4 * tm * tn * 4)),
        name="swiglu_up",
    )(h, wg, wu, *casts)
    return res[0], res[1:]


def _deferred_rows(a_refs, w_ref, acc_ref, *, ntiles, nk, last, nchunks, epilogue):
    i = pl.program_id(0)
    k = pl.program_id(1)
    tk = a_refs[0].shape[1]
    per = nk // len(a_refs)

    def a_tile():
        tile = a_refs[-1][...]
        for n in range(len(a_refs) - 2, -1, -1):
            tile = jnp.where(k < (n + 1) * per, a_refs[n][...], tile)
        return tile

    cur = acc_ref.at[i % 2]
    prev = acc_ref.at[(i + 1) % 2]
    rc = acc_ref.shape[1] // nchunks
    full_end = nk - (last < tk)
    assert nchunks <= full_end
    mm = i < ntiles
    epi = (i >= 1) & (k < nchunks)

    def run_epilogue():
        epilogue(prev[pl.ds(pl.multiple_of(k * rc, rc), rc), :])

    @pl.when(mm & (k == 0) & (i == 0))
    def _():
        cur[...] = _dot(a_tile(), w_ref[...])

    @pl.when(mm & (k == 0) & (i >= 1))
    def _():
        run_epilogue()
        cur[...] = _dot(a_tile(), w_ref[...])

    @pl.when(mm & (k > 0) & (k < full_end) & epi)
    def _():
        run_epilogue()
        cur[...] += _dot(a_tile(), w_ref[...])

    @pl.when(mm & (k > 0) & (k < full_end) & jnp.logical_not(epi))
    def _():
        cur[...] += _dot(a_tile(), w_ref[...])

    if last < tk:
        @pl.when(mm & (k == nk - 1))
        def _():
            cur[...] += _dot(a_refs[0][:, :last], w_ref[:last, :])

    @pl.when(jnp.logical_not(mm) & (k < nchunks))
    def _():
        run_epilogue()


def _deferred_specs(m, kdim, d, nsrc=1):
    tm = _tile(m, 1024, V7X_SUBLANES_BF16)
    tk = 512
    ntiles = m // tm
    nk = pl.cdiv(kdim, tk)
    last = kdim - (nk - 1) * tk
    assert last % V7X_LANES == 0 and nk >= 2
    nchunks = 1 << ((nk - (last < tk)).bit_length() - 1)
    nchunks = min(nchunks, tm // V7X_SUBLANES_F32)
    rc = tm // nchunks
    per = nk // nsrc
    assert per * nsrc == nk and (nsrc == 1 or last == tk)
    kstep = lambda i, k: jnp.where(i < ntiles, k, nk - 1)
    a_specs = [pl.BlockSpec((tm, tk), lambda i, k, n=n: (jnp.minimum(i, ntiles - 1),
                                                         jnp.clip(kstep(i, k) - n * per, 0, per - 1)))
               for n in range(nsrc)]
    w_spec = pl.BlockSpec((tk, d), lambda i, k: (kstep(i, k), 0))
    chunk = lambda i, k: (jnp.where(i >= 1, (i - 1) * nchunks + jnp.minimum(k, nchunks - 1), 0), 0)
    return tm, tk, ntiles, nk, last, nchunks, rc, a_specs, w_spec, chunk


def _down_kernel(*refs, scale, nsrc, **tiling):
    a_refs = refs[:nsrc]
    w_ref, x_ref, gpost_ref, gnext_ref, o_ref, h_ref, acc_ref = refs[nsrc:]

    def epilogue(f):
        xn = x_ref[...] + scale * _rms(f, gpost_ref[...])
        o_ref[...] = xn
        h_ref[...] = _rms(xn, gnext_ref[...]).astype(h_ref.dtype)

    _deferred_rows(a_refs, w_ref, acc_ref, epilogue=epilogue, **tiling)


def _down(a_parts, w, x, gpost, gnext, scale):
    m = a_parts[0].shape[0]
    kdim, d = w.shape
    tm, tk, ntiles, nk, last, nchunks, rc, a_specs, w_spec, chunk = _deferred_specs(m, kdim, d, len(a_parts))
    return pl.pallas_call(
        functools.partial(_down_kernel, scale=scale, nsrc=len(a_parts),
                          ntiles=ntiles, nk=nk, last=last, nchunks=nchunks),
        out_shape=(jax.ShapeDtypeStruct((m, d), F32), jax.ShapeDtypeStruct((m, d), BF16)),
        grid=(ntiles + 1, nk),
        in_specs=[*a_specs, w_spec,
                  pl.BlockSpec((rc, d), chunk),
                  pl.BlockSpec((1, d), lambda i, k: (0, 0)),
                  pl.BlockSpec((1, d), lambda i, k: (0, 0))],
        out_specs=(pl.BlockSpec((rc, d), chunk), pl.BlockSpec((rc, d), chunk)),
        scratch_shapes=[pltpu.VMEM((2, tm, d), F32)],
        compiler_params=pltpu.CompilerParams(
            dimension_semantics=("arbitrary", "arbitrary"),
            vmem_limit_bytes=_vmem_limit(len(a_parts) * tm * tk * 2 + tk * d * 2 + rc * d * 10,
                                         2 * tm * d * 4 + 6 * rc * d * 4)),
        name="down_norm_residual",
    )(*a_parts, w, x, gpost, gnext)


def _ple_kernel(a_ref, w_ref, x_ref, p_ref, wp_ref, gpost_ref, o_ref, acc_ref, **tiling):
    def epilogue(f):
        proj = _dot(p_ref[...], wp_ref[...])
        o_ref[...] = x_ref[...] + _rms(proj * jax.nn.sigmoid(f), gpost_ref[...])

    _deferred_rows([a_ref], w_ref, acc_ref, epilogue=epilogue, **tiling)


def _ple(a, w, x, p, wp, gpost):
    m, kdim = a.shape
    d = w.shape[1]
    pd = p.shape[1]
    tm, tk, ntiles, nk, last, nchunks, rc, (a_spec,), w_spec, chunk = _deferred_specs(m, kdim, d)
    return pl.pallas_call(
        functools.partial(_ple_kernel, ntiles=ntiles, nk=nk, last=last, nchunks=nchunks),
        out_shape=jax.ShapeDtypeStruct((m, d), F32),
        grid=(ntiles + 1, nk),
        in_specs=[a_spec, w_spec,
                  pl.BlockSpec((rc, d), chunk),
                  pl.BlockSpec((rc, pd), chunk),
                  pl.BlockSpec((pd, d), lambda i, k: (0, 0)),
                  pl.BlockSpec((1, d), lambda i, k: (0, 0))],
        out_specs=pl.BlockSpec((rc, d), chunk),
        scratch_shapes=[pltpu.VMEM((2, tm, d), F32)],
        compiler_params=pltpu.CompilerParams(
            dimension_semantics=("arbitrary", "arbitrary"),
            vmem_limit_bytes=_vmem_limit(tm * tk * 2 + tk * d * 2 + rc * d * 8 + rc * pd * 2 + pd * d * 2,
                                         2 * tm * d * 4 + 6 * rc * d * 4)),
        name="ple_gate_norm_residual",
    )(a, w, x, p, wp, gpost)


def _qk_kernel(h_ref, w_ref, cos_ref, sa_ref, sb_ref, o_ref, *, q_tiles, qscale):
    j = pl.program_id(1)
    w = w_ref[...].astype(BF16)
    sc = jnp.where(j < q_tiles, qscale, 1.0).astype(F32)
    parts = 4
    part = h_ref.shape[0] // parts
    for r in range(parts):
        rows = pl.ds(r * part, part)
        z = _dot(h_ref[rows, :], w)
        c, sa, sb = cos_ref[rows, :], sa_ref[rows, :], sb_ref[rows, :]
        outs = []
        for g in range(z.shape[1] // V7X_LANES):
            zg = z[:, g * V7X_LANES:(g + 1) * V7X_LANES]
            rg = zg * c + pltpu.roll(zg, V7X_LANES - 16, 1) * sa + pltpu.roll(zg, 16, 1) * sb
            outs.append(rg * sc)
        o_ref[rows, :] = jnp.concatenate(outs, axis=1).astype(o_ref.dtype)


def _qk_proj(h, w, cos_t, sa_t, sb_t, *, seq, qk_cols, qscale):
    m, d = h.shape
    n = 2 * qk_cols
    tm = _tile(seq, 1024, V7X_SUBLANES_BF16)
    tn = _tile(qk_cols, 1024, V7X_LANES)
    tpb = seq // tm
    return pl.pallas_call(
        functools.partial(_qk_kernel, q_tiles=qk_cols // tn, qscale=qscale),
        out_shape=jax.ShapeDtypeStruct((m, n), BF16),
        grid=(m // tm, n // tn),
        in_specs=[pl.BlockSpec((tm, d), lambda i, j: (i, 0)),
                  pl.BlockSpec((d, tn), lambda i, j: (0, j)),
                  pl.BlockSpec((tm, V7X_LANES), lambda i, j: (i % tpb, 0)),
                  pl.BlockSpec((tm, V7X_LANES), lambda i, j: (i % tpb, 0)),
                  pl.BlockSpec((tm, V7X_LANES), lambda i, j: (i % tpb, 0))],
        out_specs=pl.BlockSpec((tm, tn), lambda i, j: (i, j)),
        compiler_params=pltpu.CompilerParams(
            dimension_semantics=("arbitrary", "arbitrary"),
            vmem_limit_bytes=_vmem_limit(tm * d * 2 + d * tn * 4 + tm * tn * 2 + 3 * tm * V7X_LANES * 4,
                                         d * tn * 2 + 5 * tm * tn * 4)),
        name="qk_proj_rotary",
    )(h, w, cos_t, sa_t, sb_t)


def _vt_proj_kernel(wt_ref, h_ref, o_ref):
    o_ref[...] = _dot_nt(wt_ref[...], h_ref[...]).astype(o_ref.dtype)


def _vt_proj(h, wt):
    m, d = h.shape
    n = wt.shape[0]
    tm = _tile(m, 1024, V7X_LANES)
    tn = _tile(n, 1024, V7X_SUBLANES_BF16)
    return pl.pallas_call(
        _vt_proj_kernel,
        out_shape=jax.ShapeDtypeStruct((n, m), BF16),
        grid=(m // tm, n // tn),
        in_specs=[pl.BlockSpec((tn, d), lambda i, j: (j, 0)),
                  pl.BlockSpec((tm, d), lambda i, j: (i, 0))],
        out_specs=pl.BlockSpec((tn, tm), lambda i, j: (j, i)),
        compiler_params=pltpu.CompilerParams(
            dimension_semantics=("arbitrary", "arbitrary"),
            vmem_limit_bytes=_vmem_limit(tm * d * 2 + d * tn * 2 + tm * tn * 2, 2 * tm * tn * 4)),
        name="v_proj_transposed",
    )(wt, h)


def _shift_proj_kernel(h_ref, w_ref, mu_ref, o_ref, carry_ref, *, tiles_per_seq):
    i = pl.program_id(0)
    j = pl.program_id(1)

    @pl.when(i == 0)
    def _():
        carry_ref[j] = jnp.zeros(carry_ref.shape[1:], F32)

    z = _dot(h_ref[...], w_ref[...])
    prev_row = jnp.where(i % tiles_per_seq == 0, 0.0, carry_ref[j, 0:1, :])
    row = lax.broadcasted_iota(jnp.int32, z.shape, 0)
    zp = jnp.where(row == 0, prev_row, pltpu.roll(z, 1, 0))
    carry_ref[j, 0:1, :] = z[z.shape[0] - 1:, :]
    o_ref[...] = z + (zp - z) * mu_ref[...]


def _shift_proj(h, w, mu, *, seq):
    m, d = h.shape
    n = w.shape[1]
    tm = _tile(seq, 1024, V7X_SUBLANES_BF16)
    tn = _tile(n, 1024, V7X_LANES)
    return pl.pallas_call(
        functools.partial(_shift_proj_kernel, tiles_per_seq=seq // tm),
        out_shape=jax.ShapeDtypeStruct((m, n), F32),
        grid=(m // tm, n // tn),
        in_specs=[pl.BlockSpec((tm, d), lambda i, j: (i, 0)),
                  pl.BlockSpec((d, tn), lambda i, j: (0, j)),
                  pl.BlockSpec((1, tn), lambda i, j: (0, j))],
        out_specs=pl.BlockSpec((tm, tn), lambda i, j: (i, j)),
        scratch_shapes=[pltpu.VMEM((n // tn, V7X_SUBLANES_F32, tn), F32)],
        compiler_params=pltpu.CompilerParams(
            dimension_semantics=("arbitrary", "arbitrary"),
            vmem_limit_bytes=_vmem_limit(tm * d * 2 + d * tn * 2 + tm * tn * 4, 8 * n * 4 + 4 * tm * tn * 4)),
        name="rwkv_proj_token_shift",
    )(h, w, mu)


def _attn_kernel(q_ref, k_ref, vt_ref, lq1_ref, lk1_ref, lq2_ref, lk2_ref, sg_ref, o_ref,
                 m_ref, l_ref, acc_ref, *, tq, hd, lambda_init):
    qi = pl.program_id(2)
    qs = [q_ref[h * tq:(h + 1) * tq, :] for h in range(2)]
    m_ref[...] = jnp.full(m_ref.shape, -jnp.inf, F32)
    l_ref[...] = jnp.zeros(l_ref.shape, F32)
    acc_ref[...] = jnp.zeros(acc_ref.shape, F32)

    def tile(start, work):
        kt = k_ref[pl.ds(start, tq), :]
        vt = vt_ref[:, pl.ds(start, tq)]
        chains = [(h, c, diag) for h, diag in work for c in range(2)]
        ss = [_dot_nt(kt[:, c * hd:(c + 1) * hd], qs[h][:, c * hd:(c + 1) * hd]) for h, c, _ in chains]
        if any(diag for _, _, diag in chains):
            sh = ATTN_CHUNK.bit_length() - 1
            ck = lax.shift_right_logical(lax.broadcasted_iota(jnp.int32, ss[0].shape, 0), sh)
            rq = lax.shift_right_logical(lax.broadcasted_iota(jnp.int32, ss[0].shape, 1), sh)
            ss = [jnp.where(ck <= rq, s, -jnp.inf) if diag else s for s, (_, _, diag) in zip(ss, chains)]
        m_olds = [m_ref[h, c] for h, c, _ in chains]
        m_news = [jnp.maximum(m_old, jnp.max(s, axis=0, keepdims=True)) for m_old, s in zip(m_olds, ss)]
        ps = [jnp.exp2(s - m_new) for s, m_new in zip(ss, m_news)]
        alphas = [jnp.exp2(m_old - m_new) for m_old, m_new in zip(m_olds, m_news)]
        vt1 = jnp.concatenate([vt, jnp.ones((V7X_SUBLANES_BF16, vt.shape[1]), BF16)], axis=0)
        nv = vt.shape[0]
        pvs = [_dot(vt1, p.astype(BF16)) for p in ps]
        for (h, c, _), alpha, pv, m_new in zip(chains, alphas, pvs, m_news):
            l_ref[h, c] = alpha * l_ref[h, c] + pv[nv:nv + 1]
            acc_ref[h, c] = alpha * acc_ref[h, c] + pv[:nv]
            m_ref[h, c] = m_new

    def body(j, carry):
        tile(pl.multiple_of(j * tq, tq), [(0, False), (1, False)])
        return carry

    lax.fori_loop(0, 2 * qi, body, 0)
    tile(pl.multiple_of(2 * qi * tq, tq), [(0, True), (1, False)])
    tile(pl.multiple_of((2 * qi + 1) * tq, tq), [(1, True)])

    lam = (jnp.exp(jnp.sum(lq1_ref[...] * lk1_ref[...], axis=-1, keepdims=True))
           - jnp.exp(jnp.sum(lq2_ref[...] * lk2_ref[...], axis=-1, keepdims=True)) + lambda_init)
    for h in range(2):
        o = acc_ref[h, 0] / l_ref[h, 0] - lam * (acc_ref[h, 1] / l_ref[h, 1])
        o = o * lax.rsqrt(jnp.mean(o * o, axis=0, keepdims=True) + RMS_EPS) * sg_ref[...] * (1.0 - lambda_init)
        o_ref[h * tq:(h + 1) * tq, :] = o.T.astype(o_ref.dtype)


def _diff_attention(zqk, vt, lq1, lk1, lq2, lk2, subln_g, *, batch, seq, heads, hd, lambda_init):
    m = zqk.shape[0]
    vd = 2 * hd
    tq = _tile(seq // 2, 512, V7X_LANES)
    nq = seq // (2 * tq)
    vec = pl.BlockSpec((1, hd), lambda b, h, i: (0, 0))
    return pl.pallas_call(
        functools.partial(_attn_kernel, tq=tq, hd=hd, lambda_init=lambda_init),
        out_shape=jax.ShapeDtypeStruct((m, heads * vd), BF16),
        grid=(batch, heads, nq),
        in_specs=[pl.BlockSpec((2 * tq, vd), lambda b, h, i: (b * nq + i, h)),
                  pl.BlockSpec((seq, vd), lambda b, h, i: (b, heads + h)),
                  pl.BlockSpec((vd, seq), lambda b, h, i: (h, b)),
                  vec, vec, vec, vec,
                  pl.BlockSpec((vd, 1), lambda b, h, i: (0, 0))],
        out_specs=pl.BlockSpec((2 * tq, vd), lambda b, h, i: (b * nq + i, h)),
        scratch_shapes=[pltpu.VMEM((2, 2, 1, tq), F32), pltpu.VMEM((2, 2, 1, tq), F32),
                        pltpu.VMEM((2, 2, vd, tq), F32)],
        compiler_params=pltpu.CompilerParams(
            dimension_semantics=("arbitrary", "arbitrary", "arbitrary"),
            vmem_limit_bytes=_vmem_limit(4 * tq * vd * 2 + 2 * seq * vd * 2,
                                         4 * tq * vd * 4 + 20 * tq * tq * 4)),
        name="diff_attention",
    )(zqk, zqk, vt, lq1, lk1, lq2, lk2, subln_g)


def _split3_bf16(x):
    h1 = x.astype(BF16)
    r1 = x - h1.astype(F32)
    h2 = r1.astype(BF16)
    h3 = (r1 - h2.astype(F32)).astype(BF16)
    return h1, h2, h3


def _rwkv_kernel(z_ref, zlc_ref, zln_ref, w0_ref, w2_ref, a0_ref, a2_ref, g2_ref,
                 kk_ref, ka_ref, rk_ref, lnw_ref, lnb_ref, o_ref,
                 st_ref, *side_refs, rw, hn, nc):
    c = pl.program_id(1)
    g = pl.program_id(0) * nc + c
    lt = z_ref.shape[0]
    ln = V7X_LANES
    sides = (side_refs[:4], side_refs[4:])

    @pl.when(c == 0)
    def _():
        st_ref[...] = jnp.zeros(st_ref.shape, F32)

    def side_paths(zlm, dst):
        lw_ref, cum_ref, as_ref, gt_ref = dst
        dl = w2_ref.shape[0]
        al = a2_ref.shape[0]
        wpre = w0_ref[...] + _dot(jnp.tanh(zlm[:, :dl]).astype(BF16), w2_ref[...])
        softplus = jnp.maximum(-wpre, 0.0) + jnp.log(1.0 + jnp.exp(-jnp.abs(wpre)))
        lw = -jnp.exp(-softplus - 0.5)
        lw_ref[...] = lw
        tri = (lax.broadcasted_iota(jnp.int32, (lt, lt), 0)
               >= lax.broadcasted_iota(jnp.int32, (lt, lt), 1)).astype(BF16)
        h1, h2, h3 = _split3_bf16(lw)
        cum_ref[...] = _dot(tri, h1) + _dot(tri, h2) + _dot(tri, h3)
        as_ref[...] = jax.nn.sigmoid(a0_ref[...] + _dot(zlm[:, dl:dl + al].astype(BF16), a2_ref[...]))
        gt_ref[...] = _dot(jax.nn.sigmoid(zlm[:, dl + al:]).astype(BF16), g2_ref[...])

    @pl.when(g == 0)
    def _():
        side_paths(zlc_ref[...], sides[0])

    lane = lax.broadcasted_iota(jnp.int32, (lt, ln), 1)
    first = lane < hn
    r2 = lax.broadcasted_iota(jnp.int32, (2 * hn, 2 * hn), 0)
    c2 = lax.broadcasted_iota(jnp.int32, (2 * hn, 2 * hn), 1)
    hsh = hn.bit_length() - 1
    same = lax.shift_right_logical(r2, hsh) == lax.shift_right_logical(c2, hsh)
    strict = same & (r2 > c2)
    incl = same & (r2 >= c2)
    eye = (r2 == c2).astype(F32)
    zeros = jnp.zeros((2 * hn, ln), F32)

    def headsum(x):
        s0 = jnp.sum(jnp.where(first, x, 0.0), axis=-1, keepdims=True)
        s1 = jnp.sum(jnp.where(first, 0.0, x), axis=-1, keepdims=True)
        return jnp.where(first, s0, s1)

    def bd(x):
        return jnp.concatenate([jnp.where(first, x, 0.0), jnp.where(first, 0.0, x)], axis=0)

    def fold(x):
        return x[:hn] + x[hn:]

    def pairs(offs, src):
        lw_ref, cum_ref, as_ref, gt_ref = src
        n2 = 2 * hn
        sls = [pl.ds(off, ln) for off in offs]
        rs = [z_ref[:, pl.ds(off, ln)] for off in offs]
        ks = [z_ref[:, pl.ds(rw + off, ln)] for off in offs]
        vs = [z_ref[:, pl.ds(2 * rw + off, ln)] for off in offs]
        asgs = [as_ref[:, sl] for sl in sls]
        kks = [k * kk_ref[:, sl] for k, sl in zip(ks, sls)]
        kks = [kk / jnp.maximum(jnp.sqrt(headsum(kk * kk)), 1e-12) for kk in kks]
        kns = [k * (1.0 + (asg - 1.0) * ka_ref[:, sl]) for k, asg, sl in zip(ks, asgs, sls)]
        bns = [kk * asg for kk, asg in zip(kks, asgs)]

        lws = [lw_ref[:, sl] for sl in sls]
        cums = [cum_ref[:, sl] for sl in sls]
        tots = [cum[lt - 1:lt, :] for cum in cums]
        e_invs = [jnp.exp(-cum) for cum in cums]
        ats = [bd(-kk * jnp.exp(cum - lw)) for kk, cum, lw in zip(kks, cums, lws)]
        rts = [bd(r * jnp.exp(cum)) for r, cum in zip(rs, cums)]
        vbs = [bd(v) for v in vs]

        fulls = [_dot_nt(jnp.concatenate([at, rt], axis=0).astype(BF16),
                         jnp.concatenate([bd(bn * e), bd(kn * e)], axis=0).astype(BF16))
                 for at, rt, bn, kn, e in zip(ats, rts, bns, kns, e_invs)]
        a_abs = [jnp.where(strict, f[:n2, :n2], 0.0) for f in fulls]
        a_aks = [jnp.where(strict, f[:n2, n2:], 0.0) for f in fulls]
        a_rs = [jnp.concatenate([jnp.where(incl, f[n2:, :n2], 0.0), jnp.where(incl, f[n2:, n2:], 0.0)],
                                axis=1).astype(BF16) for f in fulls]

        akvs = [_dot(a_ak.astype(BF16), vb.astype(BF16)) for a_ak, vb in zip(a_aks, vbs)]
        zqs = [jnp.concatenate([at, akv], axis=1) for at, akv in zip(ats, akvs)]
        apows = a_abs
        steps = max(1, (lt - 1).bit_length())
        for it in range(steps):
            if it + 1 < steps:
                aws = [_dot(ap.astype(BF16), jnp.concatenate([zq, ap], axis=1).astype(BF16))
                       for ap, zq in zip(apows, zqs)]
                zqs = [zq + aw[:, :2 * ln] for zq, aw in zip(zqs, aws)]
                apows = [aw[:, 2 * ln:] for aw in aws]
            else:
                zqs = [zq + _dot(ap.astype(BF16), zq.astype(BF16)) for ap, zq in zip(apows, zqs)]

        rhs2s = [jnp.concatenate([zq, jnp.concatenate([zeros, vb], axis=1)], axis=0).astype(BF16)
                 for zq, vb in zip(zqs, vbs)]
        rys = [_dot(a_r, rhs2) for a_r, rhs2 in zip(a_rs, rhs2s)]
        e_ends = [jnp.exp(tot - cum) for tot, cum in zip(tots, cums)]
        ghs = [_dot_tn(jnp.concatenate([bd(bn * e), bd(kn * e)], axis=0).astype(BF16), rhs2)
               for bn, kn, e, rhs2 in zip(bns, kns, e_ends, rhs2s)]
        lhs4s = [jnp.concatenate([fold(rt + ry[:, :ln]), fold(gh[:, :ln] + eye * jnp.exp(tot))],
                                 axis=0).astype(BF16)
                 for rt, ry, gh, tot in zip(rts, rys, ghs, tots)]
        outs = [_dot(lhs4, bd(st_ref[:, sl]).astype(BF16)) for lhs4, sl in zip(lhs4s, sls)]
        for sl, out, ry, gh, r, kn, v in zip(sls, outs, rys, ghs, rs, kns, vs):
            y = out[:hn] + fold(ry[:, ln:])
            st_ref[:, sl] = out[hn:] + fold(gh[:, ln:])
            mean = headsum(y) * (1.0 / hn)
            yc = y - mean
            var = headsum(yc * yc) * (1.0 / hn)
            yn = yc * lax.rsqrt(var + RW_LN_EPS) * lnw_ref[:, sl] + lnb_ref[:, sl]
            bonus = headsum(r * kn * rk_ref[:, sl]) * v
            o_ref[:, sl] = ((yn + bonus) * gt_ref[:, sl]).astype(o_ref.dtype)

    def step(cur, nxt):
        side_paths(zln_ref[...], nxt)
        pairs([u * ln for u in range(rw // ln)], cur)

    for parity in range(2):
        @pl.when(g % 2 == parity)
        def _():
            step(sides[parity], sides[1 - parity])


def _rwkv(zr, zl, w0, w2, a0, a2, g2, k_k, k_a, r_k, ln_w, ln_b, *, batch, seq, rw, hn):
    m, zc = zr.shape
    lc = zl.shape[1]
    assert hn * 2 == V7X_LANES and RW_CHUNK == hn and seq % RW_CHUNK == 0 and zc == 3 * rw
    lt = RW_CHUNK
    nc = seq // lt
    nsteps = batch * nc
    row = lambda n: pl.BlockSpec((1, n), lambda b, c: (0, 0))
    full = lambda a: pl.BlockSpec(a.shape, lambda b, c: (0, 0))
    return pl.pallas_call(
        functools.partial(_rwkv_kernel, rw=rw, hn=hn, nc=nc),
        out_shape=jax.ShapeDtypeStruct((m, rw), BF16),
        grid=(batch, nc),
        in_specs=[pl.BlockSpec((lt, zc), lambda b, c: (b * nc + c, 0)),
                  pl.BlockSpec((lt, lc), lambda b, c: (b * nc + c, 0)),
                  pl.BlockSpec((lt, lc), lambda b, c: (jnp.minimum(b * nc + c + 1, nsteps - 1), 0)),
                  row(rw), full(w2), row(rw), full(a2), full(g2),
                  row(rw), row(rw), row(rw), row(rw), row(rw)],
        out_specs=pl.BlockSpec((lt, rw), lambda b, c: (b * nc + c, 0)),
        scratch_shapes=[pltpu.VMEM((hn, rw), F32)] + [pltpu.VMEM((lt, rw), F32)] * 8,
        compiler_params=pltpu.CompilerParams(
            dimension_semantics=("arbitrary", "arbitrary"),
            vmem_limit_bytes=_vmem_limit(lt * (zc + 2 * lc) * 4 + (w2.size + a2.size + g2.size) * 2 + lt * rw * 2,
                                         9 * lt * rw * 4 + (8 << 20))),
        name="rwkv7_chunked",
    )(zr, zl, zl, w0, w2, a0, a2, g2, k_k, k_a, r_k, ln_w, ln_b)


def _pad_to(a, axis, n):
    pad = n - a.shape[axis]
    if pad == 0:
        return a
    widths = [(0, 0)] * a.ndim
    widths[axis] = (0, pad)
    return jnp.pad(a, widths)


def _ffn(h, x, w_gate, w_up, w_down, gpost, gnext, later_weight):
    act, (wd, later) = _gateup(h, w_gate, w_up, [w_down, later_weight])
    xn, hn = _down([act], wd, x, gpost, gnext, 0.5)
    return xn, hn, later


def kernel(x, p, ffn1_pre_g, ffn1_w_gate, ffn1_w_up, ffn1_w_down, ffn1_post_g, mix_pre_g, w_in, diff_lambda_q1, diff_lambda_k1, diff_lambda_q2, diff_lambda_k2, diff_subln_g, rwkv_mu, rwkv_w0, rwkv_w2, rwkv_a0, rwkv_a2, rwkv_g2, rwkv_k_k, rwkv_k_a, rwkv_r_k, rwkv_ln_w, rwkv_ln_b, w_out, mix_post_g, ffn2_pre_g, ffn2_w_gate, ffn2_w_up, ffn2_w_down, ffn2_post_g, ple_pre_g, ple_w_gate, ple_w_proj, ple_post_g):
    batch, seq, d = x.shape
    depth = p.shape[0]
    m = batch * seq
    hd = diff_subln_g.shape[-1] // 2
    rw = rwkv_w2.shape[-1]
    da = w_out.shape[1] - rw
    heads = da // (2 * hd)
    qk = heads * 2 * hd
    rheads, hn = rwkv_r_k.shape[1], rwkv_r_k.shape[2]
    dl, al, gl = rwkv_w2.shape[1], rwkv_a2.shape[1], rwkv_g2.shape[1]
    dlp, alp, glp = (_round_up(n, V7X_LANES) for n in (dl, al, gl))
    rot = hd // 4
    assert rheads * hn == rw and da == rw and w_in.shape[2] == 2 * qk + da + 3 * rw + dl + al + gl

    inv = ROPE_THETA ** (-jnp.arange(0, rot, 2, dtype=F32) / rot)
    ang = jnp.arange(seq).astype(F32)[:, None] * inv[None, :]
    half = rot // 2
    ones = jnp.ones((seq, hd - rot), F32)
    zeros_h = jnp.zeros((seq, half), F32)
    zeros_r = jnp.zeros((seq, hd - rot), F32)
    cos_t = jnp.concatenate([jnp.cos(ang), jnp.cos(ang), ones], axis=1)
    sa_t = jnp.concatenate([-jnp.sin(ang), zeros_h, zeros_r], axis=1)
    sb_t = jnp.concatenate([zeros_h, jnp.sin(ang), zeros_r], axis=1)
    assert half == 16 and hd == V7X_LANES

    row = lambda a: a.reshape(1, -1).astype(F32)
    xf = x.reshape(m, d)
    for i in range(depth):
        lambda_init = 0.8 - 0.6 * math.exp(-0.3 * i)

        h = _prenorm(xf, row(ffn1_pre_g[i]))
        xf, h, w_o = _ffn(h, xf, ffn1_w_gate[i], ffn1_w_up[i], ffn1_w_down[i],
                          row(ffn1_post_g[i]), row(mix_pre_g[i]), w_out[i])

        wi = w_in[i]
        c0 = 2 * qk + da + 3 * rw
        w_l = jnp.concatenate([_pad_to(wi[:, c0:c0 + dl], 1, dlp),
                               _pad_to(wi[:, c0 + dl:c0 + dl + al], 1, alp),
                               _pad_to(wi[:, c0 + dl + al:], 1, glp)], axis=1).astype(BF16)
        mu = rwkv_mu[i]
        mu_l = jnp.concatenate([_pad_to(mu[3 * rw:3 * rw + dl], 0, dlp),
                                _pad_to(mu[3 * rw + dl:3 * rw + dl + al], 0, alp),
                                _pad_to(mu[3 * rw + dl + al:], 0, glp)])
        zqk = _qk_proj(h, wi[:, :2 * qk].astype(BF16), cos_t, sa_t, sb_t, seq=seq, qk_cols=qk,
                       qscale=hd ** -0.5 * math.log2(math.e))
        vt = _vt_proj(h, wi[:, 2 * qk:2 * qk + da].astype(BF16).T)
        zr = _shift_proj(h, wi[:, 2 * qk + da:c0].astype(BF16), row(mu[:3 * rw]), seq=seq)
        zl = _shift_proj(h, w_l, row(mu_l), seq=seq)

        o_diff = _diff_attention(zqk, vt, row(diff_lambda_q1[i]), row(diff_lambda_k1[i]),
                                 row(diff_lambda_q2[i]), row(diff_lambda_k2[i]),
                                 diff_subln_g[i].reshape(-1, 1).astype(F32),
                                 batch=batch, seq=seq, heads=heads, hd=hd, lambda_init=lambda_init)
        o_rwkv = _rwkv(zr, zl, row(rwkv_w0[i]),
                     _pad_to(rwkv_w2[i].astype(BF16), 0, dlp), row(rwkv_a0[i]),
                     _pad_to(rwkv_a2[i].astype(BF16), 0, alp), _pad_to(rwkv_g2[i].astype(BF16), 0, glp),
                     row(rwkv_k_k[i]), row(rwkv_k_a[i]), row(rwkv_r_k[i]),
                     row(rwkv_ln_w[i]), row(rwkv_ln_b[i]), batch=batch, seq=seq, rw=rw, hn=hn)

        xf, h = _down([o_diff, o_rwkv], w_o, xf, row(mix_post_g[i]), row(ffn2_pre_g[i]), 1.0)
        xf, h, w_pg = _ffn(h, xf, ffn2_w_gate[i], ffn2_w_up[i], ffn2_w_down[i],
                           row(ffn2_post_g[i]), row(ple_pre_g[i]), ple_w_gate[i])
        xf = _ple(h, w_pg, xf, p[i].reshape(m, -1).astype(BF16),
                  ple_w_proj[i].astype(BF16), row(ple_post_g[i]))
    return xf.reshape(batch, seq, d)
```
